```python
import math
import jax, jax.numpy as jnp
from jax import lax
import numpy as np

D_MODEL = 1024
BATCH = 4
SEQ = 4096
DEPTH = 2
DEC_BATCH = 128
DEC_SEQ = 4
PAST_LEN = 8192
PAGE_SIZE = 128

MLA_HEADS = 8
MLA_NOPE_DIM = 64
MLA_ROPE_DIM = 32
MLA_V_DIM = 64
MLA_Q_RANK = 256
MLA_KV_RANK = 256
ROPE_THETA = 10000.0
NSA_HEADS = 8
NSA_KV_HEADS = 2
NSA_HEAD_DIM = 64
NSA_CMP_BLOCK = 32
NSA_SEL_BLOCK = 64
NSA_TOP_N = 16
NSA_WINDOW = 512
SWA_HEADS = 16
SWA_KV_HEADS = 4
SWA_HEAD_DIM = 64
SWA_WINDOW = 128
REL_BUCKETS = 32
REL_MAX_DISTANCE = 128
REL_HEADS = max(NSA_HEADS, SWA_HEADS)
D_FF = ((8 * D_MODEL + 3 * 256 - 1) // (3 * 256)) * 256

NSA_GROUP = NSA_HEADS // NSA_KV_HEADS
SWA_GROUP = SWA_HEADS // SWA_KV_HEADS
MLA_IN = MLA_Q_RANK + MLA_KV_RANK + MLA_ROPE_DIM
NSA_KV_WIDTH = NSA_KV_HEADS * 2 * NSA_HEAD_DIM
NSA_IN = NSA_HEADS * NSA_HEAD_DIM + 3 * NSA_KV_WIDTH + 3 * NSA_HEADS
D_IN_AB = MLA_IN + NSA_IN
D_OUT_AB = MLA_HEADS * MLA_V_DIM + NSA_HEADS * NSA_HEAD_DIM
D_IN_C = (SWA_HEADS + 2 * SWA_KV_HEADS) * SWA_HEAD_DIM
D_OUT_C = SWA_HEADS * SWA_HEAD_DIM
MLA_SCALE = (MLA_NOPE_DIM + MLA_ROPE_DIM) ** -0.5
NSA_SCALE = NSA_HEAD_DIM ** -0.5
Q_BLOCK = 128
NORM_EPS = 1e-6
MASK_VALUE = -1e30
SELECT_BIG = 1e9
F32 = jnp.float32

kernel_name = 'hybrid_mla_nsa_swa_adaln_decode_step'


def _split_last(x, sizes):
    parts, start = [], 0
    for n in sizes:
        parts.append(x[..., start:start + n])
        start += n
    return parts


def rms_norm(x, gain=None):
    xf = x.astype(F32)
    y = xf * lax.rsqrt(jnp.mean(xf * xf, axis=-1, keepdims=True) + NORM_EPS)
    if gain is not None:
        y = y * gain.astype(F32)
    return y.astype(x.dtype)


def ada_modulation(c, w, b):
    m = (jax.nn.silu(c) @ w + b)[:, None, :]
    return jnp.split(m, 6, axis=-1)


def modulate(x, shift, scale):
    return rms_norm(x) * (1 + scale) + shift


def swiglu(h, w_in, w_out):
    g, u = jnp.split(h @ w_in, 2, axis=-1)
    return (jax.nn.silu(g) * u) @ w_out


def t5_bucket(dist):
    n = jnp.maximum(dist, 0)
    exact = REL_BUCKETS // 2
    scaled = jnp.log(jnp.maximum(n, 1).astype(F32) / exact) / math.log(REL_MAX_DISTANCE / exact)
    large = jnp.minimum(exact + (scaled * (REL_BUCKETS - exact)).astype(jnp.int32), REL_BUCKETS - 1)
    return jnp.where(n < exact, n, large)


def rel_bias(table, dist, n_heads):
    return table[:, :n_heads][t5_bucket(dist)].astype(F32)


def rope(x, pos):
    half = x.shape[-1] // 2
    freq = ROPE_THETA ** (-jnp.arange(half, dtype=F32) / half)
    ang = pos.astype(F32)[:, None] * freq[None, :]
    cos = jnp.cos(ang)[None, :, None, :]
    sin = jnp.sin(ang)[None, :, None, :]
    xf = x.astype(F32)
    x1, x2 = xf[..., :half], xf[..., half:]
    return jnp.concatenate([x1 * cos - x2 * sin, x1 * sin + x2 * cos], axis=-1).astype(x.dtype)


def softmax_sink(s, sink):
    if sink is None:
        return jax.nn.softmax(s, axis=-1)
    col = jnp.broadcast_to(sink.astype(F32)[:, :, None, None], s.shape[:-1] + (1,))
    return jax.nn.softmax(jnp.concatenate([s, col], axis=-1), axis=-1)[..., :-1]


def banded_prompt(q, k, v, window, table, sinks):
    B, T, G, R, Dh = q.shape
    nb = T // Q_BLOCK
    nw = -(-window // Q_BLOCK)
    kw = (nw + 1) * Q_BLOCK
    pad = ((0, 0), (nw * Q_BLOCK, 0), (0, 0), (0, 0))
    kb = jnp.pad(k, pad).reshape(B, nw + nb, Q_BLOCK, G, Dh)
    vb = jnp.pad(v, pad).reshape(B, nw + nb, Q_BLOCK, G, Dh)
    k_slab = jnp.concatenate([kb[:, i:i + nb] for i in range(nw + 1)], axis=2)
    v_slab = jnp.concatenate([vb[:, i:i + nb] for i in range(nw + 1)], axis=2)
    qb = q.reshape(B, nb, Q_BLOCK, G, R, Dh)
    s = jnp.einsum('bnqgrd,bnkgd->bngrqk', qb, k_slab, preferred_element_type=F32) * (Dh ** -0.5)
    start = jnp.arange(nb, dtype=jnp.int32)[:, None] * Q_BLOCK
    qpos = start + jnp.arange(Q_BLOCK, dtype=jnp.int32)[None, :]
    kpos = start - nw * Q_BLOCK + jnp.arange(kw, dtype=jnp.int32)[None, :]
    dist = qpos[:, :, None] - kpos[:, None, :]
    mask = (dist >= 0) & (dist < window) & (kpos[:, None, :] >= 0)
    bias = rel_bias(table, dist, G * R).reshape(nb, Q_BLOCK, kw, G, R).transpose(0, 3, 4, 1, 2)
    s = jnp.where(mask[:, None, None], s + bias, MASK_VALUE)
    p = softmax_sink(s, sinks)
    o = jnp.einsum('bngrqk,bnkgd->bnqgrd', p.astype(v.dtype), v_slab)
    return o.reshape(B, T, G, R, Dh)


def window_attend_sample(q, q_pos, buf, new, buf_pos, window, table, sinks):
    B, S, G, R, Dh = q.shape
    kv = jnp.concatenate([buf, new], axis=1)
    kpos = jnp.concatenate([buf_pos, q_pos])
    s = jnp.einsum('bsgrd,bkgd->bgrsk', q, kv[..., 0, :], preferred_element_type=F32) * (Dh ** -0.5)
    dist = q_pos[:, None] - kpos[None, :]
    mask = (dist >= 0) & (dist < window)
    bias = rel_bias(table, dist, G * R).reshape(S, -1, G, R).transpose(2, 3, 0, 1)
    s = jnp.where(mask, s + bias, MASK_VALUE)
    p = softmax_sink(s, sinks)
    return jnp.einsum('bgrsk,bkgd->bsgrd', p.astype(new.dtype), kv[..., 1, :])


def mla_project(a_in, pos, q_norm, w_uq, kv_norm):
    B, S, _ = a_in.shape
    q_lat, kv_lat, k_r = _split_last(a_in, [MLA_Q_RANK, MLA_KV_RANK, MLA_ROPE_DIM])
    q = (rms_norm(q_lat, q_norm) @ w_uq).reshape(B, S, MLA_HEADS, MLA_NOPE_DIM + MLA_ROPE_DIM)
    q_nope = q[..., :MLA_NOPE_DIM]
    q_rope = rope(q[..., MLA_NOPE_DIM:], pos)
    ckv = rms_norm(kv_lat, kv_norm)
    kr = rope(k_r[:, :, None, :], pos)[:, :, 0, :]
    return q_nope, q_rope, ckv, kr


def mla_attend_prompt(q_nope, q_rope, ckv, kr, w_uk, w_uv):
    B, T, H, _ = q_nope.shape
    k_nope = jnp.einsum('btc,chd->bthd', ckv, w_uk)
    v = jnp.einsum('btc,chd->bthd', ckv, w_uv)
    nb = T // Q_BLOCK
    qn = jnp.moveaxis(q_nope.reshape(B, nb, Q_BLOCK, H, MLA_NOPE_DIM), 1, 0)
    qr = jnp.moveaxis(q_rope.reshape(B, nb, Q_BLOCK, H, MLA_ROPE_DIM), 1, 0)
    kpos = jnp.arange(T, dtype=jnp.int32)

    def block(args):
        qn_b, qr_b, i = args
        qpos = i * Q_BLOCK + jnp.arange(Q_BLOCK, dtype=jnp.int32)
        s = (jnp.einsum('bqhd,bkhd->bhqk', qn_b, k_nope, preferred_element_type=F32)
             + jnp.einsum('bqhr,bkr->bhqk', qr_b, kr, preferred_element_type=F32)) * MLA_SCALE
        s = jnp.where(kpos[None, :] <= qpos[:, None], s, MASK_VALUE)
        p = jax.nn.softmax(s, axis=-1)
        return jnp.einsum('bhqk,bkhd->bqhd', p.astype(v.dtype), v)

    o = lax.map(block, (qn, qr, jnp.arange(nb, dtype=jnp.int32)))
    return jnp.moveaxis(o, 0, 1).reshape(B, T, H, MLA_V_DIM)


def mla_attend_sample(q_nope, q_rope, new_rows, past_rows, w_uk, w_uv):
    S = q_nope.shape[1]
    P = past_rows.shape[1]
    q_cat = jnp.concatenate([jnp.einsum('bshd,chd->bshc', q_nope, w_uk), q_rope], axis=-1)
    s_past = jnp.einsum('bshc,bpc->bhsp', q_cat, past_rows, preferred_element_type=F32) * MLA_SCALE
    s_new = jnp.einsum('bshc,bnc->bhsn', q_cat, new_rows, preferred_element_type=F32) * MLA_SCALE
    causal = jnp.arange(S)[None, :] <= jnp.arange(S)[:, None]
    p = jax.nn.softmax(jnp.concatenate([s_past, jnp.where(causal, s_new, MASK_VALUE)], axis=-1), axis=-1)
    dt = past_rows.dtype
    o_full = (jnp.einsum('bhsp,bpc->bshc', p[..., :P].astype(dt), past_rows)
              + jnp.einsum('bhsn,bnc->bshc', p[..., P:].astype(dt), new_rows))
    return jnp.einsum('bshc,chd->bshd', o_full[..., :MLA_KV_RANK], w_uv)


def nsa_compress(rows, w_cmp, pe_cmp):
    B, T = rows.shape[:2]
    n = T // NSA_CMP_BLOCK
    blk = rows[:, :n * NSA_CMP_BLOCK].reshape(B, n, NSA_CMP_BLOCK, NSA_KV_HEADS, 2, NSA_HEAD_DIM)
    blk = blk + pe_cmp[:, None, :, :].astype(rows.dtype)
    return jnp.einsum('bnlgcd,lcde->bngce', blk, w_cmp)


def nsa_compress_select(q, q_pos, kvc, n_sel, table):
    B, S, G, R, _ = q.shape
    NC = kvc.shape[1]
    end = jnp.arange(NC, dtype=jnp.int32) * NSA_CMP_BLOCK + (NSA_CMP_BLOCK - 1)
    dist = q_pos[:, None] - end[None, :]
    valid = dist >= 0
    bias = rel_bias(table, dist, NSA_HEADS).reshape(S, NC, G, R).transpose(0, 2, 3, 1)
    s = jnp.einsum('bsgrd,bngd->bsgrn', q, kvc[..., 0, :], preferred_element_type=F32) * NSA_SCALE + bias
    s = jnp.where(valid[:, None, None, :], s, MASK_VALUE)
    p = jax.nn.softmax(s, axis=-1)
    p = jnp.where(valid.any(axis=-1)[:, None, None, None], p, 0.0)
    o = jnp.einsum('bsgrn,bngd->bsgrd', p.astype(kvc.dtype), kvc[..., 1, :])
    ratio = NSA_SEL_BLOCK // NSA_CMP_BLOCK
    imp = p.sum(axis=3)
    imp = jnp.pad(imp, ((0, 0), (0, 0), (0, 0), (0, n_sel * ratio - NC))).reshape(B, S, G, n_sel, ratio).sum(-1)
    blk = jnp.arange(n_sel, dtype=jnp.int32)[None, :]
    cur = (q_pos // NSA_SEL_BLOCK)[:, None]
    forced = (blk == 0) | (blk == cur) | (blk == cur - 1)
    causal = blk * NSA_SEL_BLOCK <= q_pos[:, None]
    score = jnp.where(causal[:, None, :], jnp.where(forced[:, None, :], SELECT_BIG, imp), -SELECT_BIG)
    top_s, top_i = lax.top_k(score, min(NSA_TOP_N, n_sel))
    return o, top_i, top_s > -0.5 * SELECT_BIG


def nsa_selected(q, q_pos, kv, idx, ok, table):
    G = q.shape[2]
    k = kv[..., 0, :]
    v = kv[..., 1, :]
    s = jnp.einsum('bsgrd,bsgkld->bsgrkl', q, k, preferred_element_type=F32) * NSA_SCALE
    kpos = idx[..., None] * NSA_SEL_BLOCK + jnp.arange(NSA_SEL_BLOCK, dtype=jnp.int32)
    dist = q_pos[None, :, None, None, None] - kpos
    tb = table[:, :NSA_HEADS].reshape(REL_BUCKETS, G, NSA_GROUP)
    g_ix = jnp.arange(G)[None, None, :, None, None]
    bias = jnp.moveaxis(tb[t5_bucket(dist), g_ix], -1, 3).astype(F32)
    mask = (ok[..., None] & (dist >= 0))[:, :, :, None]
    s = jnp.where(mask, s + bias, MASK_VALUE)
    b, sq, g, r, kk, ll = s.shape
    p = jax.nn.softmax(s.reshape(b, sq, g, r, kk * ll), axis=-1).reshape(s.shape)
    return jnp.einsum('bsgrkl,bsgkld->bsgrd', p.astype(v.dtype), v)


def project_ab(h, w_in):
    B, S, _ = h.shape
    a_in, q, cmp_rows, sel_rows, win_rows, g = _split_last(
        h @ w_in, [MLA_IN, NSA_HEADS * NSA_HEAD_DIM, NSA_KV_WIDTH, NSA_KV_WIDTH, NSA_KV_WIDTH, 3 * NSA_HEADS])
    rows = (B, S, NSA_KV_HEADS, 2, NSA_HEAD_DIM)
    q = q.reshape(B, S, NSA_KV_HEADS, NSA_GROUP, NSA_HEAD_DIM)
    gates = jax.nn.sigmoid(g).reshape(B, S, NSA_KV_HEADS, NSA_GROUP, 3)
    return a_in, q, cmp_rows.reshape(rows), sel_rows.reshape(rows), win_rows.reshape(rows), gates


def combine_ab(o_a, o_cmp, o_sel, o_win, gates, w_out):
    B, S = o_a.shape[:2]
    o_b = gates[..., 0:1] * o_cmp + gates[..., 1:2] * o_sel + gates[..., 2:3] * o_win
    return jnp.concatenate([o_a.reshape(B, S, -1), o_b.reshape(B, S, -1)], axis=-1) @ w_out


def mixer_ab_prompt(h, w_in, mla_q_norm, mla_w_uq, mla_kv_norm, mla_w_uk, mla_w_uv, nsa_w_cmp, nsa_pe_cmp, w_out, table):
    B, T, _ = h.shape
    pos = jnp.arange(T, dtype=jnp.int32)
    a_in, q, cmp_rows, sel_rows, win_rows, gates = project_ab(h, w_in)
    q_nope, q_rope, ckv, kr = mla_project(a_in, pos, mla_q_norm, mla_w_uq, mla_kv_norm)
    o_a = mla_attend_prompt(q_nope, q_rope, ckv, kr, mla_w_uk, mla_w_uv)
    kvc = nsa_compress(cmp_rows, nsa_w_cmp, nsa_pe_cmp)
    n_sel = T // NSA_SEL_BLOCK
    sel_blocks = sel_rows.reshape(B, n_sel, NSA_SEL_BLOCK, NSA_KV_HEADS, 2, NSA_HEAD_DIM)
    nb = T // Q_BLOCK
    q_blocks = jnp.moveaxis(q.reshape(B, nb, Q_BLOCK, NSA_KV_HEADS, NSA_GROUP, NSA_HEAD_DIM), 1, 0)
    b_ix = jnp.arange(B)[:, None, None, None]
    g_ix = jnp.arange(NSA_KV_HEADS)[None, None, :, None]

    def block(args):
        qb, i = args
        qpos = i * Q_BLOCK + jnp.arange(Q_BLOCK, dtype=jnp.int32)
        o_c, idx, ok = nsa_compress_select(qb, qpos, kvc, n_sel, table)
        kv = sel_blocks[b_ix, idx, :, g_ix]
        return o_c, nsa_selected(qb, qpos, kv, idx, ok, table)

    o_cmp, o_sel = lax.map(block, (q_blocks, jnp.arange(nb, dtype=jnp.int32)))
    shp = (B, T, NSA_KV_HEADS, NSA_GROUP, NSA_HEAD_DIM)
    o_cmp = jnp.moveaxis(o_cmp, 0, 1).reshape(shp)
    o_sel = jnp.moveaxis(o_sel, 0, 1).reshape(shp)
    o_win = banded_prompt(q, win_rows[..., 0, :], win_rows[..., 1, :], NSA_WINDOW, table, None)
    o = combine_ab(o_a, o_cmp, o_sel, o_win, gates, w_out)
    mla_rows = jnp.concatenate([ckv, kr], axis=-1)
    win_state = win_rows[:, T - min(NSA_WINDOW, T):]
    return o, mla_rows, cmp_rows, sel_rows, win_state


def mixer_ab_sample(h, cache_mla, cache_nsa_cmp, cache_nsa_sel, state_nsa_win, page_table, w_in, mla_q_norm, mla_w_uq,
                    mla_kv_norm, mla_w_uk, mla_w_uv, nsa_w_cmp, nsa_pe_cmp, w_out, table):
    Bd, S, _ = h.shape
    pos = PAST_LEN + jnp.arange(S, dtype=jnp.int32)
    a_in, q, cmp_new, sel_new, win_new, gates = project_ab(h, w_in)
    q_nope, q_rope, ckv, kr = mla_project(a_in, pos, mla_q_norm, mla_w_uq, mla_kv_norm)
    mla_new = jnp.concatenate([ckv, kr], axis=-1)
    mla_past = cache_mla[page_table].reshape(Bd, PAST_LEN, MLA_KV_RANK + MLA_ROPE_DIM)
    o_a = mla_attend_sample(q_nope, q_rope, mla_new, mla_past, mla_w_uk, mla_w_uv)
    cmp_past = cache_nsa_cmp[page_table].reshape(Bd, PAST_LEN, NSA_KV_HEADS, 2, NSA_HEAD_DIM)
    kvc = jnp.concatenate([nsa_compress(cmp_past, nsa_w_cmp, nsa_pe_cmp),
                           nsa_compress(cmp_new, nsa_w_cmp, nsa_pe_cmp)], axis=1)
    n_past = PAST_LEN // NSA_SEL_BLOCK
    n_new = -(-S // NSA_SEL_BLOCK)
    o_cmp, idx, ok = nsa_compress_select(q, pos, kvc, n_past + n_new, table)
    spp = PAGE_SIZE // NSA_SEL_BLOCK
    pool = cache_nsa_sel.reshape(-1, spp, NSA_SEL_BLOCK, NSA_KV_HEADS, 2, NSA_HEAD_DIM)
    b_ix = jnp.arange(Bd)[:, None, None, None]
    g_ix = jnp.arange(NSA_KV_HEADS)[None, None, :, None]
    past_idx = jnp.minimum(idx, n_past - 1)
    phys = page_table[b_ix, past_idx // spp]
    kv_past = pool[phys, past_idx % spp, :, g_ix]
    new_blocks = jnp.pad(sel_new, ((0, 0), (0, n_new * NSA_SEL_BLOCK - S), (0, 0), (0, 0), (0, 0)))
    new_blocks = new_blocks.reshape(Bd, n_new, NSA_SEL_BLOCK, NSA_KV_HEADS, 2, NSA_HEAD_DIM)
    kv_new = new_blocks[b_ix, jnp.clip(idx - n_past, 0, n_new - 1), :, g_ix]
    kv = jnp.where((idx < n_past)[..., None, None, None], kv_past, kv_new)
    o_sel = nsa_selected(q, pos, kv, idx, ok, table)
    wb = state_nsa_win.shape[1]
    buf_pos = PAST_LEN - wb + jnp.arange(wb, dtype=jnp.int32)
    o_win = window_attend_sample(q, pos, state_nsa_win, win_new, buf_pos, NSA_WINDOW, table, None)
    o = combine_ab(o_a, o_cmp, o_sel, o_win, gates, w_out)
    win_state = jnp.concatenate([state_nsa_win, win_new], axis=1)[:, S:]
    return o, mla_new, cmp_new, sel_new, win_state


def project_c(h, w_in):
    B, S, _ = h.shape
    q, kv = _split_last(h @ w_in, [SWA_HEADS * SWA_HEAD_DIM, 2 * SWA_KV_HEADS * SWA_HEAD_DIM])
    return (q.reshape(B, S, SWA_KV_HEADS, SWA_GROUP, SWA_HEAD_DIM),
            kv.reshape(B, S, SWA_KV_HEADS, 2, SWA_HEAD_DIM))


def mixer_c_prompt(h, w_in, sinks, w_out, table):
    B, T, _ = h.shape
    q, kv = project_c(h, w_in)
    o = banded_prompt(q, kv[..., 0, :], kv[..., 1, :], SWA_WINDOW, table, sinks.reshape(SWA_KV_HEADS, SWA_GROUP))
    return o.reshape(B, T, -1) @ w_out, kv[:, T - min(SWA_WINDOW, T):]


def mixer_c_sample(h, state_swa, w_in, sinks, w_out, table):
    Bd, S, _ = h.shape
    pos = PAST_LEN + jnp.arange(S, dtype=jnp.int32)
    q, kv = project_c(h, w_in)
    wb = state_swa.shape[1]
    buf_pos = PAST_LEN - wb + jnp.arange(wb, dtype=jnp.int32)
    o = window_attend_sample(q, pos, state_swa, kv, buf_pos, SWA_WINDOW, table,
                             sinks.reshape(SWA_KV_HEADS, SWA_GROUP))
    return o.reshape(Bd, S, -1) @ w_out, jnp.concatenate([state_swa, kv], axis=1)[:, S:]


def setup_inputs(seed: int = 0) -> dict:
    key = jax.random.key(seed)
    keys = iter(jax.random.split(key, 40))

    def nrm(shape, scale):
        return jax.random.normal(next(keys), shape, jnp.float32) * scale

    n_pages = PAST_LEN // PAGE_SIZE
    n_used = DEC_BATCH * n_pages
    n_pool = n_used + max(1, n_used // 4)
    page_table = jax.random.permutation(next(keys), n_pool)[:n_used].reshape(DEC_BATCH, n_pages).astype(jnp.int32)
    kv_row = (NSA_KV_HEADS, 2, NSA_HEAD_DIM)
    return {
        'x_prompt': nrm((BATCH, SEQ, D_MODEL), 1.0),
        'x_sample': nrm((DEC_BATCH, DEC_SEQ, D_MODEL), 1.0),
        'cache_mla': nrm((n_pool, PAGE_SIZE, MLA_KV_RANK + MLA_ROPE_DIM), 1.0),
        'cache_nsa_cmp': nrm((n_pool, PAGE_SIZE) + kv_row, 1.0),
        'cache_nsa_sel': nrm((n_pool, PAGE_SIZE) + kv_row, 1.0),
        'state_nsa_win': nrm((DEC_BATCH, min(NSA_WINDOW, PAST_LEN)) + kv_row, 1.0),
        'state_swa': nrm((DEC_BATCH, min(SWA_WINDOW, PAST_LEN), SWA_KV_HEADS, 2, SWA_HEAD_DIM), 1.0),
        'page_table': page_table,
        'c_prompt': nrm((BATCH, D_MODEL), 1.0),
        'c_sample': nrm((DEC_BATCH, D_MODEL), 1.0),
        'rel_bias_table': nrm((REL_BUCKETS, REL_HEADS), 0.5),
        'w_ada_0': nrm((D_MODEL, 6 * D_MODEL), D_MODEL ** -0.5),
        'b_ada_0': nrm((6 * D_MODEL,), 0.02),
        'w_in_0': nrm((D_MODEL, D_IN_AB), D_MODEL ** -0.5),
        'mla_q_norm': 1.0 + nrm((MLA_Q_RANK,), 0.05),
        'mla_w_uq': nrm((MLA_Q_RANK, MLA_HEADS * (MLA_NOPE_DIM + MLA_ROPE_DIM)), MLA_Q_RANK ** -0.5),
        'mla_kv_norm': 1.0 + nrm((MLA_KV_RANK,), 0.05),
        'mla_w_uk': nrm((MLA_KV_RANK, MLA_HEADS, MLA_NOPE_DIM), MLA_KV_RANK ** -0.5),
        'mla_w_uv': nrm((MLA_KV_RANK, MLA_HEADS, MLA_V_DIM), MLA_KV_RANK ** -0.5),
        'nsa_w_cmp': nrm((NSA_CMP_BLOCK, 2, NSA_HEAD_DIM, NSA_HEAD_DIM), (NSA_CMP_BLOCK * NSA_HEAD_DIM) ** -0.5),
        'nsa_pe_cmp': nrm((NSA_CMP_BLOCK, 2, NSA_HEAD_DIM), 0.1),
        'w_out_0': nrm((D_OUT_AB, D_MODEL), D_OUT_AB ** -0.5),
        'w_ffn_in_0': nrm((D_MODEL, 2 * D_FF), D_MODEL ** -0.5),
        'w_ffn_out_0': nrm((D_FF, D_MODEL), D_FF ** -0.5),
        'w_ada_1': nrm((D_MODEL, 6 * D_MODEL), D_MODEL ** -0.5),
        'b_ada_1': nrm((6 * D_MODEL,), 0.02),
        'w_in_1': nrm((D_MODEL, D_IN_C), D_MODEL ** -0.5),
        'swa_sinks': nrm((SWA_HEADS,), 1.0),
        'w_out_1': nrm((D_OUT_C, D_MODEL), D_OUT_C ** -0.5),
        'w_ffn_in_1': nrm((D_MODEL, 2 * D_FF), D_MODEL ** -0.5),
        'w_ffn_out_1': nrm((D_FF, D_MODEL), D_FF ** -0.5),
        'final_norm': 1.0 + nrm((D_MODEL,), 0.05),
    }


def reference(x_prompt, x_sample, cache_mla, cache_nsa_cmp, cache_nsa_sel, state_nsa_win, state_swa, page_table,
              c_prompt, c_sample, rel_bias_table, w_ada_0, b_ada_0, w_in_0, mla_q_norm, mla_w_uq, mla_kv_norm,
              mla_w_uk, mla_w_uv, nsa_w_cmp, nsa_pe_cmp, w_out_0, w_ffn_in_0, w_ffn_out_0, w_ada_1, b_ada_1, w_in_1,
              swa_sinks, w_out_1, w_ffn_in_1, w_ffn_out_1, final_norm):
    layer_params = ((w_ada_0, b_ada_0, w_ffn_in_0, w_ffn_out_0), (w_ada_1, b_ada_1, w_ffn_in_1, w_ffn_out_1))
    yp, ys = x_prompt, x_sample
    for layer in range(DEPTH):
        w_ada, b_ada, w_fi, w_fo = layer_params[layer]
        sh_p, sc_p, gt_p, fsh_p, fsc_p, fgt_p = ada_modulation(c_prompt, w_ada, b_ada)
        sh_s, sc_s, gt_s, fsh_s, fsc_s, fgt_s = ada_modulation(c_sample, w_ada, b_ada)
        hp = modulate(yp, sh_p, sc_p)
        hs = modulate(ys, sh_s, sc_s)
        if layer % 2 == 0:
            op, mla_p, cmp_p, sel_p, win_p = mixer_ab_prompt(
                hp, w_in_0, mla_q_norm, mla_w_uq, mla_kv_norm, mla_w_uk, mla_w_uv, nsa_w_cmp, nsa_pe_cmp,
                w_out_0, rel_bias_table)
            os_, mla_s, cmp_s, sel_s, win_s = mixer_ab_sample(
                hs, cache_mla, cache_nsa_cmp, cache_nsa_sel, state_nsa_win, page_table, w_in_0, mla_q_norm,
                mla_w_uq, mla_kv_norm, mla_w_uk, mla_w_uv, nsa_w_cmp, nsa_pe_cmp, w_out_0, rel_bias_table)
        else:
            op, swa_p = mixer_c_prompt(hp, w_in_1, swa_sinks, w_out_1, rel_bias_table)
            os_, swa_s = mixer_c_sample(hs, state_swa, w_in_1, swa_sinks, w_out_1, rel_bias_table)
        yp = yp + gt_p * op
        ys = ys + gt_s * os_
        yp = yp + fgt_p * swiglu(modulate(yp, fsh_p, fsc_p), w_fi, w_fo)
        ys = ys + fgt_s * swiglu(modulate(ys, fsh_s, fsc_s), w_fi, w_fo)
    y_prompt = rms_norm(yp, final_norm)
    y_sample = rms_norm(ys, final_norm)
    return (y_prompt, y_sample, mla_p, mla_s, cmp_p, cmp_s, sel_p, sel_s, win_p, win_s, swa_p, swa_s)
```

```python
import functools
import math

import jax
import jax.numpy as jnp
from jax import lax
from jax.experimental import pallas as pl
from jax.experimental.pallas import tpu as pltpu

F32 = jnp.float32
BF16 = jnp.bfloat16

MLA_HEADS, MLA_NOPE, MLA_ROPE, MLA_V = 8, 64, 32, 64
ROPE_THETA = 10000.0
NSA_HEADS, NSA_KV_HEADS, HEAD_DIM = 8, 2, 64
NSA_CMP_BLOCK, NSA_SEL_BLOCK, NSA_TOP_N, NSA_WINDOW = 32, 64, 16, 512
SWA_HEADS, SWA_KV_HEADS, SWA_WINDOW = 16, 4, 128
REL_BUCKETS, REL_MAX_DISTANCE = 32, 128
Q_BLOCK = 128
NORM_EPS = 1e-6
NEG = -1e30
SELECT_BIG = 1e9
MLA_SCALE = (MLA_NOPE + MLA_ROPE) ** -0.5
ATTN_SCALE = HEAD_DIM ** -0.5

LANES = 128
VMEM_LIMIT = 52 * 1024 * 1024


def _cp(*sem):
    return pltpu.CompilerParams(dimension_semantics=sem, vmem_limit_bytes=VMEM_LIMIT)


def _dot(a, b):
    return jnp.dot(a, b, preferred_element_type=F32)


def _dot_t(a, b):
    return lax.dot_general(a, b, (((1,), (1,)), ((), ())), preferred_element_type=F32)


def _rms(x):
    return x * lax.rsqrt(jnp.mean(x * x, axis=-1, keepdims=True) + NORM_EPS)


def _sigmoid(x):
    return 1.0 / (1.0 + jnp.exp(-x))


def _const_spec(shape):
    n = len(shape)
    return pl.BlockSpec(shape, lambda *_: (0,) * n)


def _softmax_update(scores, values, m_ref, l_ref, acc_ref):
    m_prev = m_ref[...]
    m_new = m_prev
    for s in scores:
        m_new = jnp.maximum(m_new, jnp.max(s, axis=-1, keepdims=True))
    alpha = jnp.exp(m_prev - m_new)
    l_new = alpha * l_ref[...]
    acc = alpha * acc_ref[...]
    for s, v in zip(scores, values):
        p = jnp.exp(s - m_new)
        l_new = l_new + jnp.sum(p, axis=-1, keepdims=True)
        acc = acc + _dot(p.astype(BF16), v)
    m_ref[...] = m_new
    l_ref[...] = l_new
    acc_ref[...] = acc


def _softmax_init(m_ref, l_ref, acc_ref):
    m_ref[...] = jnp.full(m_ref.shape, NEG, F32)
    l_ref[...] = jnp.zeros(l_ref.shape, F32)
    acc_ref[...] = jnp.zeros(acc_ref.shape, F32)


def _ada_kernel(c_ref, w_ref, b_ref, o_ref):
    c = c_ref[...]
    a = (c * _sigmoid(c)).astype(BF16)
    o_ref[...] = _dot(a, w_ref[...]) + b_ref[...]


def _ada(c, w, b):
    M, D = c.shape
    N = w.shape[1]
    tn = 1024 if N % 1024 == 0 else N
    return pl.pallas_call(
        _ada_kernel, grid=(N // tn,),
        in_specs=[pl.BlockSpec((M, D), lambda j: (0, 0)), pl.BlockSpec((D, tn), lambda j: (0, j)),
                  pl.BlockSpec((1, tn), lambda j: (0, j))],
        out_specs=pl.BlockSpec((M, tn), lambda j: (0, j)),
        out_shape=jax.ShapeDtypeStruct((M, N), F32), compiler_params=_cp("parallel"), name="ada")(c, w, b)


def _bias_kernel(tab_ref, bkt_ref, o_ref):
    h = pl.program_id(0)
    bkt = bkt_ref[...]
    acc = jnp.full(bkt.shape, NEG, F32)
    for b in range(REL_BUCKETS):
        acc = jnp.where(bkt == b, tab_ref[b, h], acc)
    o_ref[0] = acc


def _bias_tiles(table, buckets, n_heads):
    R, C = buckets.shape
    tr = R
    for cand in (512, 256, 128):
        if R > cand and R % cand == 0:
            tr = cand
            break
    return pl.pallas_call(
        _bias_kernel,
        grid_spec=pltpu.PrefetchScalarGridSpec(
            num_scalar_prefetch=1, grid=(n_heads, R // tr),
            in_specs=[pl.BlockSpec((tr, C), lambda h, r, tab: (r, 0))],
            out_specs=pl.BlockSpec((1, tr, C), lambda h, r, tab: (h, r, 0))),
        out_shape=jax.ShapeDtypeStruct((n_heads, R, C), F32),
        compiler_params=_cp("parallel", "parallel"), name="rel_bias")(table, buckets)


def _t5_bucket(dist):
    n = jnp.maximum(dist, 0)
    exact = REL_BUCKETS // 2
    scaled = jnp.log(jnp.maximum(n, 1).astype(F32) / exact) / math.log(REL_MAX_DISTANCE / exact)
    large = jnp.minimum(exact + (scaled * (REL_BUCKETS - exact)).astype(jnp.int32), REL_BUCKETS - 1)
    return jnp.where(n < exact, n, large)


def _masked_bucket(dist, valid):
    return jnp.where(valid, _t5_bucket(dist), -1).astype(jnp.int32)


def _modulate(x, shift, scale):
    return _rms(x) * (1.0 + scale) + shift


def _proj0_kernel(x_ref, sh_ref, sc_ref, w0_ref, qn_ref, kvn_ref, wuq_ref, wuqs_ref, cq_ref, sq_ref, cm_ref,
                  sm_ref, wa_ref, wb_ref, *outs, sample):
    h = _modulate(x_ref[0], sh_ref[0], sc_ref[0]).astype(BF16)
    y = _dot(h, w0_ref[...])
    qn = (_rms(y[:, 0:256]) * qn_ref[...]).astype(BF16)
    ckv = _rms(y[:, 256:512]) * kvn_ref[...]
    misc = y[:, 512:640] * cm_ref[...] + y[:, 640:768] * sm_ref[...]
    cq = jnp.concatenate([cq_ref[...]] * MLA_HEADS, axis=1)
    sq = jnp.concatenate([sq_ref[...]] * MLA_HEADS, axis=1)
    q_rot = ((_dot(qn, wuq_ref[...]) * cq + _dot(qn, wuqs_ref[...]) * sq) * MLA_SCALE).astype(BF16)
    ckv_b = ckv.astype(BF16)
    if sample:
        rows_ref, qcat_ref, qnsa_ref, cmp_ref, sel_ref, win_ref, gate_ref = outs
        qcat_ref[0] = _dot(q_rot, wa_ref[...]).astype(BF16)
    else:
        rows_ref, qmla_ref, kcat_ref, vmla_ref, qnsa_ref, cmp_ref, sel_ref, win_ref, selb_ref, winb_ref, gate_ref = outs
        qmla_ref[0] = q_rot
        kcat_ref[0] = _dot(jnp.concatenate([ckv_b, misc.astype(BF16)], axis=1), wa_ref[...]).astype(BF16)
        vmla_ref[0] = _dot(ckv_b, wb_ref[...]).astype(BF16)
        selb_ref[0] = y[:, 1536:1792].astype(BF16)
        winb_ref[0] = y[:, 1792:2048].astype(BF16)
    rows_ref[0, :, 0:256] = ckv
    rows_ref[0, :, 256:288] = misc[:, 0:MLA_ROPE]
    qnsa_ref[0] = y[:, 768:1280].astype(BF16)
    cmp_ref[0] = y[:, 1280:1536]
    sel_ref[0] = y[:, 1536:1792]
    win_ref[0] = y[:, 1792:2048]
    gate_ref[0] = _sigmoid(misc)


def _proj0(x, shift, scale, w, tabs, sample, tm):
    Bx, Tx, D = x.shape
    per_token = shift.shape[1] != 1
    mod_spec = (pl.BlockSpec((1, tm, D), lambda b, i: (b, i, 0)) if per_token
                else pl.BlockSpec((1, 1, D), lambda b, i: (b, 0, 0)))
    tok = lambda n: pl.BlockSpec((1, tm, n), lambda b, i: (b, i, 0))
    tab_spec = pl.BlockSpec((tm, LANES), lambda b, i: (i, 0))
    wa, wb = (w["a_abs"], w["wuv"]) if sample else (w["kcat"], w["wuv"])
    in_specs = [tok(D), mod_spec, mod_spec, _const_spec(w["w0"].shape), _const_spec((1, 256)), _const_spec((1, 256)),
                _const_spec(w["wuq"].shape), _const_spec(w["wuqs"].shape), tab_spec, tab_spec, tab_spec, tab_spec,
                _const_spec(wa.shape), _const_spec(wb.shape)]
    sd = lambda n, dt: jax.ShapeDtypeStruct((Bx, Tx, n), dt)
    if sample:
        out_shape = [sd(288, F32), sd(MLA_HEADS * 288, BF16), sd(512, BF16), sd(256, F32), sd(256, F32), sd(256, F32),
                     sd(LANES, F32)]
    else:
        out_shape = [sd(288, F32), sd(1024, BF16), sd(1024, BF16), sd(512, BF16), sd(512, BF16), sd(256, F32),
                     sd(256, F32), sd(256, F32), sd(256, BF16), sd(256, BF16), sd(LANES, F32)]
    out_specs = [tok(s.shape[-1]) for s in out_shape]
    return pl.pallas_call(
        functools.partial(_proj0_kernel, sample=sample), grid=(Bx, Tx // tm), in_specs=in_specs, out_specs=out_specs,
        out_shape=out_shape, compiler_params=_cp("parallel", "parallel"),
        name="proj0_sample" if sample else "proj0_prompt")(
            x, shift, scale, w["w0"], w["qn"], w["kvn"], w["wuq"], w["wuqs"], *tabs, wa, wb)


def _proj1_kernel(x_ref, sh_ref, sc_ref, w_ref, q_ref, kv_ref, kvb_ref):
    h = _modulate(x_ref[0], sh_ref[0], sc_ref[0]).astype(BF16)
    y = _dot(h, w_ref[...])
    nq = q_ref.shape[-1]
    q_ref[0] = y[:, :nq].astype(BF16)
    kv_ref[0] = y[:, nq:]
    kvb_ref[0] = y[:, nq:].astype(BF16)


def _proj1(x, shift, scale, w1, nq, tm):
    Bx, Tx, D = x.shape
    nkv = w1.shape[1] - nq
    per_token = shift.shape[1] != 1
    mod_spec = (pl.BlockSpec((1, tm, D), lambda b, i: (b, i, 0)) if per_token
                else pl.BlockSpec((1, 1, D), lambda b, i: (b, 0, 0)))
    tok = lambda n: pl.BlockSpec((1, tm, n), lambda b, i: (b, i, 0))
    return pl.pallas_call(
        _proj1_kernel, grid=(Bx, Tx // tm), in_specs=[tok(D), mod_spec, mod_spec, _const_spec(w1.shape)],
        out_specs=[tok(nq), tok(nkv), tok(nkv)],
        out_shape=[jax.ShapeDtypeStruct((Bx, Tx, nq), BF16), jax.ShapeDtypeStruct((Bx, Tx, nkv), F32),
                   jax.ShapeDtypeStruct((Bx, Tx, nkv), BF16)],
        compiler_params=_cp("parallel", "parallel"), name="proj1")(x, shift, scale, w1)


def _mla_prompt_kernel(q_ref, k_ref, v_ref, o_ref, m_sc, l_sc, acc_sc, *, tq, tk):
    i = pl.program_id(2)
    q = q_ref[0]
    qpos = i * tq + lax.broadcasted_iota(jnp.int32, (tq, tk), 0)
    kloc = lax.broadcasted_iota(jnp.int32, (tq, tk), 1)
    n_full = i * (tq // tk)
    outs = []
    for e in range(2):
        qe = q[:, LANES * e:LANES * (e + 1)]
        _softmax_init(m_sc, l_sc, acc_sc)

        def step(j, masked, qe=qe, e=e):
            start = pl.multiple_of(j * tk, tk)
            ke = k_ref[0, pl.ds(start, tk), LANES * e:LANES * (e + 1)]
            s = _dot_t(qe, ke)
            if masked:
                s = jnp.where(j * tk + kloc <= qpos, s, NEG)
            _softmax_update([s], [v_ref[0, pl.ds(start, tk), :]], m_sc, l_sc, acc_sc)

        def body(j, carry):
            step(j, False)
            return carry

        lax.fori_loop(0, n_full, body, 0)
        for d in range(tq // tk):
            step(n_full + d, True)
        outs.append(acc_sc[...] / l_sc[...])
    lane = lax.broadcasted_iota(jnp.int32, (tq, LANES), 1)
    o_ref[0] = jnp.where(lane < MLA_V, outs[0], outs[1]).astype(o_ref.dtype)


def _mla_prompt(q, k, v, tq, tk):
    B, T, _ = q.shape
    return pl.pallas_call(
        functools.partial(_mla_prompt_kernel, tq=tq, tk=tk), grid=(B, MLA_HEADS // 2, T // tq),
        in_specs=[pl.BlockSpec((1, tq, 2 * LANES), lambda b, h, i: (b, i, h)),
                  pl.BlockSpec((1, T, 2 * LANES), lambda b, h, i: (b, 0, h)),
                  pl.BlockSpec((1, T, LANES), lambda b, h, i: (b, 0, h))],
        out_specs=pl.BlockSpec((1, tq, LANES), lambda b, h, i: (b, i, h)),
        out_shape=jax.ShapeDtypeStruct((B, T, MLA_HEADS * MLA_V), BF16),
        scratch_shapes=[pltpu.VMEM((tq, 1), F32), pltpu.VMEM((tq, 1), F32), pltpu.VMEM((tq, LANES), F32)],
        compiler_params=_cp("parallel", "parallel", "parallel"), name="mla_prompt")(q, k, v)


def _compress_body(x, w_ref, pe_ref, o_ref):
    y = _dot(x.astype(BF16), w_ref[...]) + _dot(pe_ref[...].astype(BF16), w_ref[...])[0:1]
    o_ref[0] = y.astype(o_ref.dtype)


def _compress_prompt_kernel(x_ref, w_ref, pe_ref, o_ref):
    _compress_body(x_ref[0], w_ref, pe_ref, o_ref)


def _compress_paged_kernel(pt_ref, *refs, pps, rows):
    x_refs = refs[:pps]
    w_ref, pe_ref, o_ref, x_sc = refs[pps:]
    for k in range(pps):
        x_sc[rows * k:rows * (k + 1), :] = x_refs[k][0]
    _compress_body(x_sc[...], w_ref, pe_ref, o_ref)


def _compress_prompt(x, wbig, pe_rows):
    B, NC, K = x.shape
    return pl.pallas_call(
        _compress_prompt_kernel, grid=(B,),
        in_specs=[pl.BlockSpec((1, NC, K), lambda b: (b, 0, 0)), _const_spec(wbig.shape), _const_spec(pe_rows.shape)],
        out_specs=pl.BlockSpec((1, NC, 256), lambda b: (b, 0, 0)),
        out_shape=jax.ShapeDtypeStruct((B, NC, 256), BF16), compiler_params=_cp("parallel"),
        name="compress_prompt")(x, wbig, pe_rows)


def _compress_paged(page_table, cache, wbig, pe_rows, pps):
    Bd, n_pages = page_table.shape
    _, rows, K = cache.shape
    page_specs = [pl.BlockSpec((1, rows, K), lambda b, c, pt, k=k: (pt[b, c * pps + k], 0, 0)) for k in range(pps)]
    return pl.pallas_call(
        functools.partial(_compress_paged_kernel, pps=pps, rows=rows),
        grid_spec=pltpu.PrefetchScalarGridSpec(
            num_scalar_prefetch=1, grid=(Bd, n_pages // pps),
            in_specs=page_specs + [pl.BlockSpec(wbig.shape, lambda b, c, pt: (0, 0)),
                                   pl.BlockSpec(pe_rows.shape, lambda b, c, pt: (0, 0))],
            out_specs=pl.BlockSpec((1, rows * pps, 256), lambda b, c, pt: (b, c, 0)),
            scratch_shapes=[pltpu.VMEM((rows * pps, K), F32)]),
        out_shape=jax.ShapeDtypeStruct((Bd, n_pages * rows, 256), BF16),
        compiler_params=_cp("parallel", "parallel"), name="compress_paged")(
            page_table, *([cache] * pps), wbig, pe_rows)


def _cmp_select_kernel(q_ref, kv_ref, bias_ref, code_ref, o_ref, sel_ref, *, tq, n_rank, first_pos):
    i = pl.program_id(1)
    G, R = NSA_KV_HEADS, NSA_HEADS // NSA_KV_HEADS
    M = R * tq
    NC = kv_ref.shape[1]
    n_lanes = code_ref.shape[-1]
    code = code_ref[0]
    row_pos = first_pos + i * tq + lax.broadcasted_iota(jnp.int32, (M, 1), 0) % tq
    any_valid = (row_pos >= NSA_CMP_BLOCK - 1).astype(F32)
    lane = lax.broadcasted_iota(jnp.int32, (tq, n_lanes), 1)
    for g in range(G):
        kv = kv_ref[0, :, LANES * g:LANES * (g + 1)]
        s = _dot_t(q_ref[0, g, 0], kv) + bias_ref[0, g]
        m = jnp.max(s, axis=-1, keepdims=True)
        p = jnp.exp(s - m)
        p = p * (any_valid / jnp.sum(p, axis=-1, keepdims=True))
        o_ref[0, g, 0] = _dot(p.astype(BF16), kv).astype(o_ref.dtype)
        imp = p[0:tq]
        for r in range(1, R):
            imp = imp + p[r * tq:(r + 1) * tq]
        imp = imp[:, :NC // 2] + imp[:, NC // 2:]
        if n_lanes > NC // 2:
            imp = jnp.concatenate([imp, jnp.zeros((tq, n_lanes - NC // 2), F32)], axis=1)
        score = jnp.where(code == 0.0, imp, code)
        rank = jnp.zeros((tq, n_lanes), F32)
        for jp in range(n_rank):
            col = score[:, jp:jp + 1]
            ahead = jnp.where(col > score, 1.0, jnp.where(col == score, jnp.where(lane > jp, 1.0, 0.0), 0.0))
            rank = rank + ahead
        chosen = jnp.where(rank < float(NSA_TOP_N), jnp.where(code > -0.5 * SELECT_BIG, 1.0, 0.0), 0.0)
        sel_ref[0, g, 0] = chosen.astype(sel_ref.dtype)


def _cmp_select(q_stack, kvc, bias, code, tq, n_rank, first_pos):
    B, G, nq, M, _ = q_stack.shape
    NC = kvc.shape[1]
    n_lanes = code.shape[-1]
    return pl.pallas_call(
        functools.partial(_cmp_select_kernel, tq=tq, n_rank=n_rank, first_pos=first_pos), grid=(B, nq),
        in_specs=[pl.BlockSpec((1, G, 1, M, LANES), lambda b, i: (b, 0, i, 0, 0)),
                  pl.BlockSpec((1, NC, G * LANES), lambda b, i: (b, 0, 0)),
                  pl.BlockSpec((1, G, M, NC), lambda b, i: (i, 0, 0, 0)),
                  pl.BlockSpec((1, tq, n_lanes), lambda b, i: (i, 0, 0))],
        out_specs=[pl.BlockSpec((1, G, 1, M, LANES), lambda b, i: (b, 0, i, 0, 0)),
                   pl.BlockSpec((1, G, 1, tq, n_lanes), lambda b, i: (b, 0, i, 0, 0))],
        out_shape=[jax.ShapeDtypeStruct((B, G, nq, M, LANES), BF16),
                   jax.ShapeDtypeStruct((B, G, nq, tq, n_lanes), BF16)],
        compiler_params=_cp("parallel", "parallel"), name="cmp_select")(q_stack, kvc, bias, code)


def _sel_prompt_kernel(q_ref, kv_ref, sel_ref, e_ref, bias_ref, o_ref, m_sc, l_sc, acc_sc, *, tq):
    i = pl.program_id(1)
    G, R = NSA_KV_HEADS, NSA_HEADS // NSA_KV_HEADS
    for g in range(G):
        q = q_ref[0, g, 0]
        sel = sel_ref[0, g, 0]
        _softmax_init(m_sc, l_sc, acc_sc)

        def step(j, kind, q=q, sel=sel, g=g):
            start = pl.multiple_of(j * tq, tq)
            kv = kv_ref[0, pl.ds(start, tq), LANES * g:LANES * (g + 1)]
            keep = _dot(sel, e_ref[j])
            keep = jnp.concatenate([keep] * R, axis=0)
            s = jnp.where(keep > 0.5, _dot_t(q, kv) + bias_ref[g, kind], NEG)
            _softmax_update([s], [kv], m_sc, l_sc, acc_sc)

        def body(j, carry):
            step(j, 0)
            return carry

        lax.fori_loop(0, jnp.maximum(i - 1, 0), body, 0)

        @pl.when(i >= 1)
        def _():
            step(i - 1, 1)

        step(i, 2)
        o_ref[0, g, 0] = (acc_sc[...] / l_sc[...]).astype(o_ref.dtype)


def _sel_prompt(q_stack, kv, sel, expand, bias, tq):
    B, G, nq, M, _ = q_stack.shape
    T = kv.shape[1]
    n_sel = sel.shape[-1]
    return pl.pallas_call(
        functools.partial(_sel_prompt_kernel, tq=tq), grid=(B, nq),
        in_specs=[pl.BlockSpec((1, G, 1, M, LANES), lambda b, i: (b, 0, i, 0, 0)),
                  pl.BlockSpec((1, T, G * LANES), lambda b, i: (b, 0, 0)),
                  pl.BlockSpec((1, G, 1, tq, n_sel), lambda b, i: (b, 0, i, 0, 0)),
                  _const_spec(expand.shape), _const_spec(bias.shape)],
        out_specs=pl.BlockSpec((1, G, 1, M, LANES), lambda b, i: (b, 0, i, 0, 0)),
        out_shape=jax.ShapeDtypeStruct((B, G, nq, M, LANES), BF16),
        scratch_shapes=[pltpu.VMEM((M, 1), F32), pltpu.VMEM((M, 1), F32), pltpu.VMEM((M, LANES), F32)],
        compiler_params=_cp("parallel", "parallel"), name="sel_prompt")(q_stack, kv, sel, expand, bias)


def _banded_kernel(q_ref, kv_ref, bias_ref, sink_ref, o_ref, *, tq, nw, G):
    i = pl.program_id(1)
    for g in range(G):
        q = q_ref[0, g, 0]
        scores, vals = [], []
        for jj in range(nw + 1):
            kb = i - nw + jj
            start = pl.multiple_of(jnp.maximum(kb, 0) * tq, tq)
            kv = kv_ref[0, pl.ds(start, tq), LANES * g:LANES * (g + 1)]
            edge = jnp.where(kb >= 0, 0.0, NEG)
            scores.append(_dot_t(q, kv) + bias_ref[g, :, tq * jj:tq * (jj + 1)] + edge)
            vals.append(kv)
        sink = sink_ref[g]
        m = sink
        for s in scores:
            m = jnp.maximum(m, jnp.max(s, axis=-1, keepdims=True))
        l = jnp.exp(sink - m)
        acc = None
        for s, v in zip(scores, vals):
            p = jnp.exp(s - m)
            l = l + jnp.sum(p, axis=-1, keepdims=True)
            pv = _dot(p.astype(BF16), v)
            acc = pv if acc is None else acc + pv
        o_ref[0, g, 0] = (acc / l).astype(o_ref.dtype)


def _banded(q_stack, kv, bias, sinks, tq, nw):
    B, G, nq, M, _ = q_stack.shape
    T = kv.shape[1]
    return pl.pallas_call(
        functools.partial(_banded_kernel, tq=tq, nw=nw, G=G), grid=(B, nq),
        in_specs=[pl.BlockSpec((1, G, 1, M, LANES), lambda b, i: (b, 0, i, 0, 0)),
                  pl.BlockSpec((1, T, G * LANES), lambda b, i: (b, 0, 0)),
                  _const_spec(bias.shape), _const_spec(sinks.shape)],
        out_specs=pl.BlockSpec((1, G, 1, M, LANES), lambda b, i: (b, 0, i, 0, 0)),
        out_shape=jax.ShapeDtypeStruct((B, G, nq, M, LANES), BF16),
        compiler_params=_cp("parallel", "parallel"), name="banded")(q_stack, kv, bias, sinks)


def _mla_decode_kernel(pt_ref, q_ref, *refs, pps):
    page_refs = refs[:pps]
    new_ref, nmask_ref, o_ref, m_sc, l_sc, acc_sc = refs[pps:]
    c = pl.program_id(1)

    @pl.when(c == 0)
    def _():
        _softmax_init(m_sc, l_sc, acc_sc)

    q = q_ref[0]
    rows = [r[0].astype(BF16) for r in page_refs]
    nv = acc_sc.shape[-1]
    _softmax_update([_dot_t(q, r) for r in rows], [r[:, :nv] for r in rows], m_sc, l_sc, acc_sc)

    @pl.when(c == pl.num_programs(1) - 1)
    def _():
        nr = new_ref[0].astype(BF16)
        _softmax_update([_dot_t(q, nr) + nmask_ref[...]], [nr[:, :nv]], m_sc, l_sc, acc_sc)
        o_ref[0] = (acc_sc[...] / l_sc[...]).astype(o_ref.dtype)


def _mla_decode(page_table, q_cat, cache, new_rows, new_mask, pps):
    Bd, n_pages = page_table.shape
    _, page, width = cache.shape
    M = q_cat.shape[1]
    nv = width - MLA_ROPE
    page_specs = [pl.BlockSpec((1, page, width), lambda b, c, pt, k=k: (pt[b, c * pps + k], 0, 0)) for k in range(pps)]
    return pl.pallas_call(
        functools.partial(_mla_decode_kernel, pps=pps),
        grid_spec=pltpu.PrefetchScalarGridSpec(
            num_scalar_prefetch=1, grid=(Bd, n_pages // pps),
            in_specs=[pl.BlockSpec((1, M, width), lambda b, c, pt: (b, 0, 0))] + page_specs + [
                pl.BlockSpec((1,) + new_rows.shape[1:], lambda b, c, pt: (b, 0, 0)),
                pl.BlockSpec(new_mask.shape, lambda b, c, pt: (0, 0))],
            out_specs=pl.BlockSpec((1, M, nv), lambda b, c, pt: (b, 0, 0)),
            scratch_shapes=[pltpu.VMEM((M, 1), F32), pltpu.VMEM((M, 1), F32), pltpu.VMEM((M, nv), F32)]),
        out_shape=jax.ShapeDtypeStruct((Bd, M, nv), BF16),
        compiler_params=_cp("parallel", "arbitrary"), name="mla_decode")(
            page_table, q_cat, *([cache] * pps), new_rows, new_mask)


def _sel_decode_kernel(pt_ref, q_ref, sel_ref, e_ref, bias_ref, *refs, pps, page):
    page_refs = refs[:pps]
    new_ref, nbias_ref, o_ref, m_sc, l_sc, acc_sc = refs[pps:]
    c = pl.program_id(1)
    G = NSA_KV_HEADS
    reps = q_ref.shape[2] // sel_ref.shape[2]

    @pl.when(c == 0)
    def _():
        _softmax_init(m_sc, l_sc, acc_sc)

    for g in range(G):
        q = q_ref[0, g]
        keep = _dot(sel_ref[0, g], e_ref[c])
        keep = jnp.concatenate([keep] * reps, axis=0)
        bias = bias_ref[c, g]
        scores, vals = [], []
        for k in range(pps):
            kv = page_refs[k][0, :, LANES * g:LANES * (g + 1)].astype(BF16)
            lo, hi = page * k, page * (k + 1)
            scores.append(jnp.where(keep[:, lo:hi] > 0.5, _dot_t(q, kv) + bias[:, lo:hi], NEG))
            vals.append(kv)
        _softmax_update(scores, vals, m_sc.at[g], l_sc.at[g], acc_sc.at[g])

    @pl.when(c == pl.num_programs(1) - 1)
    def _():
        for g in range(G):
            kv = new_ref[0, :, LANES * g:LANES * (g + 1)].astype(BF16)
            _softmax_update([_dot_t(q_ref[0, g], kv) + nbias_ref[g]], [kv], m_sc.at[g], l_sc.at[g], acc_sc.at[g])
            o_ref[0, g] = (acc_sc[g] / l_sc[g]).astype(o_ref.dtype)


def _sel_decode(page_table, q_stack, sel, expand, bias, cache, new_rows, new_bias, pps):
    Bd, n_pages = page_table.shape
    _, page, width = cache.shape
    _, G, M, _ = q_stack.shape
    page_specs = [pl.BlockSpec((1, page, width), lambda b, c, pt, k=k: (pt[b, c * pps + k], 0, 0)) for k in range(pps)]
    cst = lambda a: pl.BlockSpec(a.shape, lambda b, c, pt: (0,) * a.ndim)
    per_b = lambda a: pl.BlockSpec((1,) + a.shape[1:], lambda b, c, pt: (b,) + (0,) * (a.ndim - 1))
    return pl.pallas_call(
        functools.partial(_sel_decode_kernel, pps=pps, page=page),
        grid_spec=pltpu.PrefetchScalarGridSpec(
            num_scalar_prefetch=1, grid=(Bd, n_pages // pps),
            in_specs=[per_b(q_stack), per_b(sel), cst(expand), cst(bias)] + page_specs + [per_b(new_rows), cst(new_bias)],
            out_specs=pl.BlockSpec((1, G, M, LANES), lambda b, c, pt: (b, 0, 0, 0)),
            scratch_shapes=[pltpu.VMEM((G, M, 1), F32), pltpu.VMEM((G, M, 1), F32), pltpu.VMEM((G, M, LANES), F32)]),
        out_shape=jax.ShapeDtypeStruct((Bd, G, M, LANES), BF16),
        compiler_params=_cp("parallel", "arbitrary"), name="sel_decode")(
            page_table, q_stack, sel, expand, bias, *([cache] * pps), new_rows, new_bias)


def _win_decode_kernel(q_ref, buf_ref, new_ref, bias_ref, nbias_ref, sink_ref, o_ref, *, G):
    for g in range(G):
        q = q_ref[0, g]
        kb = buf_ref[0, :, LANES * g:LANES * (g + 1)].astype(BF16)
        kn = new_ref[0, :, LANES * g:LANES * (g + 1)].astype(BF16)
        sb = _dot_t(q, kb) + bias_ref[g]
        sn = _dot_t(q, kn) + nbias_ref[g]
        sink = sink_ref[g]
        m = jnp.maximum(jnp.maximum(jnp.max(sb, axis=-1, keepdims=True), jnp.max(sn, axis=-1, keepdims=True)), sink)
        pb = jnp.exp(sb - m)
        pn = jnp.exp(sn - m)
        l = jnp.exp(sink - m) + jnp.sum(pb, axis=-1, keepdims=True) + jnp.sum(pn, axis=-1, keepdims=True)
        acc = _dot(pb.astype(BF16), kb) + _dot(pn.astype(BF16), kn)
        o_ref[0, g] = (acc / l).astype(o_ref.dtype)


def _win_decode(q_stack, buf, new_rows, bias, new_bias, sinks):
    Bd, G, M, _ = q_stack.shape
    per_b = lambda a: pl.BlockSpec((1,) + a.shape[1:], lambda b: (b,) + (0,) * (a.ndim - 1))
    return pl.pallas_call(
        functools.partial(_win_decode_kernel, G=G), grid=(Bd,),
        in_specs=[per_b(q_stack), per_b(buf), per_b(new_rows), _const_spec(bias.shape), _const_spec(new_bias.shape),
                  _const_spec(sinks.shape)],
        out_specs=pl.BlockSpec((1, G, M, LANES), lambda b: (b, 0, 0, 0)),
        out_shape=jax.ShapeDtypeStruct((Bd, G, M, LANES), BF16),
        compiler_params=_cp("parallel"), name="win_decode")(q_stack, buf, new_rows, bias, new_bias, sinks)


def _ffn_chunk(f):
    for cand in range(min(f, 1536) // LANES, 0, -1):
        if f % (cand * LANES) == 0:
            return cand * LANES
    return f


def _out_ffn_kernel(*refs, mode, final, n_mods):
    it = iter(refs)
    y_ref = next(it)
    if mode == "c":
        attn_in = next(it)[0]
    else:
        oa_ref, ocmp_ref, osel_ref, owin_ref, gate_ref, eg_ref = (next(it) for _ in range(6))
        if mode == "ab_sample":
            wuv_ref = next(it)
    wout_ref = next(it)
    gt_ref, fsh_ref, fsc_ref, fgt_ref = (next(it) for _ in range(4))
    wg_ref, wu_ref, wo_ref = (next(it) for _ in range(3))
    gain_ref = next(it) if final else None
    o_ref = next(it)

    if mode == "c":
        attn = _dot(attn_in, wout_ref[...])
    else:
        gates = gate_ref[0]
        g_hi = gates.astype(BF16)
        g_lo = (gates - g_hi.astype(F32)).astype(BF16)
        o_b = None
        for br, ref in enumerate((ocmp_ref, osel_ref, owin_ref)):
            ge = _dot(g_hi, eg_ref[br]) + _dot(g_lo, eg_ref[br])
            term = ge * ref[0].astype(F32)
            o_b = term if o_b is None else o_b + term
        o_a = oa_ref[0]
        if mode == "ab_sample":
            o_a = _dot(o_a, wuv_ref[...]).astype(BF16)
        na = o_a.shape[-1]
        attn = _dot(o_a, wout_ref[0:na, :]) + _dot(o_b.astype(BF16), wout_ref[na:, :])
    y1 = y_ref[0] + gt_ref[0] * attn
    h = _modulate(y1, fsh_ref[0], fsc_ref[0]).astype(BF16)
    f = wg_ref.shape[1]
    fc = _ffn_chunk(f)
    acc = None
    for k in range(f // fc):
        g = _dot(h, wg_ref[:, fc * k:fc * (k + 1)])
        u = _dot(h, wu_ref[:, fc * k:fc * (k + 1)])
        a = (g * _sigmoid(g) * u).astype(BF16)
        part = _dot(a, wo_ref[fc * k:fc * (k + 1), :])
        acc = part if acc is None else acc + part
    y2 = y1 + fgt_ref[0] * acc
    if final:
        y2 = _rms(y2) * gain_ref[...]
    o_ref[0] = y2


def _out_ffn(y, attn_parts, weights, mods, ffn_w, gain, mode, tm):
    Bx, Tx, D = y.shape
    per_token = mods[0].shape[1] != 1
    mod_spec = (pl.BlockSpec((1, tm, D), lambda b, i: (b, i, 0)) if per_token
                else pl.BlockSpec((1, 1, D), lambda b, i: (b, 0, 0)))
    tok = lambda a: pl.BlockSpec((1, tm, a.shape[-1]), lambda b, i: (b, i, 0))
    single = lambda a: pl.BlockSpec(a.shape, lambda b, i: (0,) * a.ndim, pipeline_mode=pl.Buffered(1))
    args = [y] + list(attn_parts) + list(weights) + list(mods) + list(ffn_w)
    in_specs = ([tok(y)] + [tok(a) for a in attn_parts] + [single(a) for a in weights] + [mod_spec] * 4
                + [single(a) for a in ffn_w])
    final = gain is not None
    if final:
        args.append(gain)
        in_specs.append(single(gain))
    return pl.pallas_call(
        functools.partial(_out_ffn_kernel, mode=mode, final=final, n_mods=4), grid=(Bx, Tx // tm),
        in_specs=in_specs, out_specs=tok(y), out_shape=jax.ShapeDtypeStruct(y.shape, F32),
        compiler_params=_cp("parallel", "parallel"), name="out_ffn_" + mode)(*args)


def _stack_heads(q, G, R, tq):
    B, T, _ = q.shape
    q = q.reshape(B, T // tq, tq, G, R, HEAD_DIM).transpose(0, 3, 1, 4, 2, 5).reshape(B, G, T // tq, R * tq, HEAD_DIM)
    return jnp.pad(q, ((0, 0),) * 4 + ((0, LANES - HEAD_DIM),))


def _unstack_heads(o, R, tq):
    B, G, nq, _, _ = o.shape
    o = o[..., HEAD_DIM:].reshape(B, G, nq, R, tq, HEAD_DIM).transpose(0, 2, 4, 1, 3, 5)
    return o.reshape(B, nq * tq, G * R * HEAD_DIM)


def _stack_heads_sample(q, G, R, S, s_pad):
    Bd = q.shape[0] // S
    q = q.reshape(Bd, S, G, R, HEAD_DIM).transpose(0, 2, 3, 1, 4)
    q = jnp.pad(q, ((0, 0), (0, 0), (0, 0), (0, s_pad - S), (0, LANES - HEAD_DIM)))
    return q.reshape(Bd, G, R * s_pad, LANES)


def _unstack_heads_sample(o, R, S, s_pad):
    Bd, G, _, _ = o.shape
    o = o[..., HEAD_DIM:].reshape(Bd, G, R, s_pad, HEAD_DIM)[:, :, :, :S].transpose(0, 3, 1, 2, 4)
    return o.reshape(1, Bd * S, G * R * HEAD_DIM)


def _group_rows(t, G):
    H, rows, C = t.shape
    return t.reshape(G, (H // G) * rows, C)


def _even_odd(n):
    return jnp.concatenate([jnp.arange(0, n, 2), jnp.arange(1, n, 2)]).astype(jnp.int32)


def _rope_tables(pos):
    half = MLA_ROPE // 2
    freq = ROPE_THETA ** (-jnp.arange(half, dtype=F32) / half)
    ang = pos.astype(F32)[:, None] * freq[None, :]
    cos, sin = jnp.cos(ang), jnp.sin(ang)
    n = pos.shape[0]
    one, zero = jnp.ones, jnp.zeros
    cq = jnp.concatenate([one((n, MLA_NOPE), F32), cos, cos, zero((n, LANES - MLA_NOPE - MLA_ROPE), F32)], axis=1)
    sq = jnp.concatenate([zero((n, MLA_NOPE), F32), sin, sin, zero((n, LANES - MLA_NOPE - MLA_ROPE), F32)], axis=1)
    n_gate = 3 * NSA_HEADS
    cm = jnp.concatenate([cos, cos, one((n, n_gate), F32), zero((n, LANES - MLA_ROPE - n_gate), F32)], axis=1)
    sm = jnp.concatenate([sin, sin, zero((n, LANES - MLA_ROPE), F32)], axis=1)
    return cq, sq, cm, sm


def _layer0_weights(w_in_0, mla_q_norm, mla_w_uq, mla_kv_norm, mla_w_uk, mla_w_uv):
    D = w_in_0.shape[0]
    qr, kvr = mla_q_norm.shape[0], mla_kv_norm.shape[0]
    half = MLA_ROPE // 2
    o_kr = qr + kvr
    o_q = o_kr + MLA_ROPE
    o_cmp = o_q + NSA_HEADS * HEAD_DIM
    kvw = NSA_KV_HEADS * 2 * HEAD_DIM
    o_g = o_cmp + 3 * kvw
    w_kr = w_in_0[:, o_kr:o_q]
    w_g = w_in_0[:, o_g:]
    z = lambda n: jnp.zeros((D, n), F32)
    misc_a = jnp.concatenate([w_kr, w_g, z(LANES - MLA_ROPE - w_g.shape[1])], axis=1)
    misc_b = jnp.concatenate([-w_kr[:, half:], w_kr[:, :half], z(LANES - MLA_ROPE)], axis=1)
    w0 = jnp.concatenate([w_in_0[:, :o_kr], misc_a, misc_b, w_in_0[:, o_q:o_cmp] * ATTN_SCALE, w_in_0[:, o_cmp:o_g]],
                         axis=1).astype(BF16)
    H = MLA_HEADS
    wq = mla_w_uq.reshape(qr, H, MLA_NOPE + MLA_ROPE)
    nope, x1, x2 = wq[..., :MLA_NOPE], wq[..., MLA_NOPE:MLA_NOPE + half], wq[..., MLA_NOPE + half:]
    zq = lambda n: jnp.zeros((qr, H, n), F32)
    pad = LANES - MLA_NOPE - MLA_ROPE
    wuq = jnp.concatenate([nope, x1, x2, zq(pad)], axis=-1).reshape(qr, H * LANES).astype(BF16)
    wuqs = jnp.concatenate([zq(MLA_NOPE), -x2, x1, zq(pad)], axis=-1).reshape(qr, H * LANES).astype(BF16)
    k_top = jnp.pad(mla_w_uk, ((0, 0), (0, 0), (0, LANES - MLA_NOPE))).reshape(kvr, H * LANES)
    place = jnp.zeros((LANES, H, LANES), F32).at[jnp.arange(MLA_ROPE), :, MLA_NOPE + jnp.arange(MLA_ROPE)].set(1.0)
    kcat = jnp.concatenate([k_top, place.reshape(LANES, H * LANES)], axis=0).astype(BF16)
    wuv = mla_w_uv.reshape(kvr, H * MLA_V).astype(BF16)
    width = kvr + MLA_ROPE
    blk = jnp.zeros((H, LANES, width), F32)
    blk = blk.at[:, :MLA_NOPE, :kvr].set(jnp.transpose(mla_w_uk, (1, 2, 0)))
    blk = blk.at[:, MLA_NOPE + jnp.arange(MLA_ROPE), kvr + jnp.arange(MLA_ROPE)].set(1.0)
    eye = jnp.eye(H, dtype=F32)
    a_abs = jnp.einsum("hij,hk->hikj", blk, eye).reshape(H * LANES, H * width).astype(BF16)
    wuv_bd = jnp.einsum("chd,hk->hckd", mla_w_uv, eye).reshape(H * kvr, H * MLA_V).astype(BF16)
    return dict(w0=w0, qn=mla_q_norm.reshape(1, qr), kvn=mla_kv_norm.reshape(1, kvr), wuq=wuq, wuqs=wuqs, kcat=kcat,
                wuv=wuv, a_abs=a_abs, wuv_bd=wuv_bd)


def _gate_expand():
    h = jnp.arange(NSA_HEADS)
    mats = []
    for br in range(3):
        m = jnp.zeros((LANES, NSA_HEADS, HEAD_DIM), F32).at[MLA_ROPE + 3 * h + br, h, :].set(1.0)
        mats.append(m.reshape(LANES, NSA_HEADS * HEAD_DIM))
    return jnp.stack(mats).astype(BF16)


def kernel(x_prompt, x_sample, cache_mla, cache_nsa_cmp, cache_nsa_sel, state_nsa_win, state_swa, page_table, c_prompt, c_sample, rel_bias_table, w_ada_0, b_ada_0, w_in_0, mla_q_norm, mla_w_uq, mla_kv_norm, mla_w_uk, mla_w_uv, nsa_w_cmp, nsa_pe_cmp, w_out_0, w_ffn_in_0, w_ffn_out_0, w_ada_1, b_ada_1, w_in_1, swa_sinks, w_out_1, w_ffn_in_1, w_ffn_out_1, final_norm):
    B, T, D = x_prompt.shape
    Bd, S, _ = x_sample.shape
    n_pool, PAGE, mla_w = cache_mla.shape
    n_pages = page_table.shape[1]
    PAST = n_pages * PAGE
    G, R = NSA_KV_HEADS, NSA_HEADS // NSA_KV_HEADS
    G1, R1 = SWA_KV_HEADS, SWA_HEADS // SWA_KV_HEADS
    tq = Q_BLOCK
    nq = T // tq
    NS = Bd * S
    S_PAD = 8
    tm_p = 512 if T % 512 == 0 else 256
    tm_s = 256 if NS % 256 == 0 else NS
    i32 = jnp.int32
    table = rel_bias_table.astype(F32)

    n_c = B + Bd
    c_all = jnp.pad(jnp.concatenate([c_prompt, c_sample], axis=0), ((0, (-n_c) % 8), (0, 0)))

    def mods_for(w_ada, b_ada):
        m = _ada(c_all, w_ada.astype(BF16), b_ada.reshape(1, -1))
        mp = [m[:B, k * D:(k + 1) * D][:, None, :] for k in range(6)]
        ms = [jnp.repeat(m[B:B + Bd, k * D:(k + 1) * D], S, axis=0)[None] for k in range(6)]
        return mp, ms

    mods0_p, mods0_s = mods_for(w_ada_0, b_ada_0)
    mods1_p, mods1_s = mods_for(w_ada_1, b_ada_1)
    xs = x_sample.reshape(1, NS, D)

    w0 = _layer0_weights(w_in_0, mla_q_norm, mla_w_uq, mla_kv_norm, mla_w_uk, mla_w_uv)
    tabs_p = _rope_tables(jnp.arange(T, dtype=i32))
    tabs_s = tuple(jnp.tile(t, (Bd, 1)) for t in _rope_tables(PAST + jnp.arange(S, dtype=i32)))
    (mla_p, q_mla, k_mla, v_mla, qn_p, cmp_p, sel_p, win_p, selb_p, winb_p, gate_p) = _proj0(
        x_prompt, mods0_p[0], mods0_p[1], w0, tabs_p, False, tm_p)
    (mla_s, qcat_s, qn_s, cmp_s, sel_s, win_s, gate_s) = _proj0(xs, mods0_s[0], mods0_s[1], w0, tabs_s, True, tm_s)

    tq_mla = 512 if T % 512 == 0 else 256
    o_a_p = _mla_prompt(q_mla, k_mla, v_mla, tq_mla, 256)

    eye2 = jnp.eye(2, dtype=F32)
    wbig = jnp.einsum("lcde,gh,ck->lgcdhke", nsa_w_cmp, eye2, eye2).reshape(NSA_CMP_BLOCK * 4 * HEAD_DIM, 4 * HEAD_DIM)
    wbig = wbig.astype(BF16)
    pe_rows = jnp.broadcast_to(nsa_pe_cmp[:, None], (NSA_CMP_BLOCK, G, 2, HEAD_DIM)).reshape(1, -1)
    pe_rows = jnp.broadcast_to(pe_rows, (8, pe_rows.shape[1]))
    NC = T // NSA_CMP_BLOCK
    kvc_p = _compress_prompt(cmp_p.reshape(B, NC, -1), wbig, pe_rows)
    order_p = _even_odd(NC)
    kvc_p = kvc_p[:, order_p]
    q_stack_p = _stack_heads(qn_p, G, R, tq)

    qpos = jnp.arange(T, dtype=i32)
    dist = qpos[:, None] - (order_p * NSA_CMP_BLOCK + NSA_CMP_BLOCK - 1)[None, :]
    bias_cmp_p = _bias_tiles(table, _masked_bucket(dist, dist >= 0), NSA_HEADS)
    bias_cmp_p = bias_cmp_p.reshape(G, R, nq, tq, NC).transpose(2, 0, 1, 3, 4).reshape(nq, G, R * tq, NC)
    n_sel = T // NSA_SEL_BLOCK
    blk = jnp.arange(n_sel, dtype=i32)[None, :]
    cur = (qpos // NSA_SEL_BLOCK)[:, None]
    forced = (blk == 0) | (blk == cur) | (blk == cur - 1)
    causal = blk * NSA_SEL_BLOCK <= qpos[:, None]
    code_p = jnp.where(causal, jnp.where(forced, SELECT_BIG, 0.0), -SELECT_BIG).astype(F32).reshape(nq, tq, n_sel)
    o_cmp_st, sel_mask_p = _cmp_select(q_stack_p, kvc_p, bias_cmp_p, code_p, tq, n_sel, 0)

    ql = jnp.arange(tq, dtype=i32)[:, None]
    kl = jnp.arange(tq, dtype=i32)[None, :]
    d_diag = ql - kl
    bk3 = jnp.concatenate([jnp.full((tq, tq), REL_BUCKETS - 1, i32), _masked_bucket(d_diag + tq, d_diag + tq >= 0),
                           _masked_bucket(d_diag, d_diag >= 0)], axis=0)
    bias_sel_p = _bias_tiles(table, bk3, NSA_HEADS).reshape(G, R, 3, tq, tq).transpose(0, 2, 1, 3, 4)
    bias_sel_p = bias_sel_p.reshape(G, 3, R * tq, tq)
    key_blk = (jnp.arange(T, dtype=i32) // NSA_SEL_BLOCK).reshape(nq, 1, tq)
    expand_p = (jnp.arange(n_sel, dtype=i32)[None, :, None] == key_blk).astype(BF16)
    o_sel_st = _sel_prompt(q_stack_p, selb_p, sel_mask_p, expand_p, bias_sel_p, tq)

    def window_bias(window, n_heads, n_groups):
        nw = -(-window // tq)
        dw = ql + nw * tq - jnp.arange((nw + 1) * tq, dtype=i32)[None, :]
        t = _bias_tiles(table, _masked_bucket(dw, (dw >= 0) & (dw < window)), n_heads)
        return _group_rows(t, n_groups), nw

    bias_win_p, nw0 = window_bias(NSA_WINDOW, NSA_HEADS, G)
    no_sink0 = jnp.full((G, R * tq, 1), NEG, F32)
    o_win_st = _banded(q_stack_p, winb_p, bias_win_p, no_sink0, tq, nw0)

    o_cmp_p = _unstack_heads(o_cmp_st, R, tq)
    o_sel_p = _unstack_heads(o_sel_st, R, tq)
    o_win_p = _unstack_heads(o_win_st, R, tq)

    pps = 8 if n_pages % 8 == 0 else n_pages
    srow = jnp.arange(S_PAD, dtype=i32)
    s_real = jnp.minimum(srow, S - 1)
    q_cat = qcat_s.reshape(Bd, S * MLA_HEADS, mla_w)
    mla_new = jnp.pad(mla_s.reshape(Bd, S, mla_w), ((0, 0), (0, S_PAD - S), (0, 0)))
    s_of_row = jnp.repeat(jnp.arange(S, dtype=i32), MLA_HEADS)[:, None]
    new_mask = jnp.where((srow[None, :] <= s_of_row) & (srow[None, :] < S), 0.0, NEG).astype(F32)
    o_full_s = _mla_decode(page_table, q_cat, cache_mla, mla_new, new_mask, pps)
    o_full_s = o_full_s.reshape(1, NS, MLA_HEADS * (mla_w - MLA_ROPE))

    rows_pp = PAGE // NSA_CMP_BLOCK
    pps_c = 16 if n_pages % 16 == 0 else pps
    kvc_s = _compress_paged(page_table, cache_nsa_cmp.reshape(n_pool, rows_pp, -1), wbig, pe_rows, pps_c)
    NCs = PAST // NSA_CMP_BLOCK
    order_s = _even_odd(NCs)
    kvc_s = kvc_s[:, order_s]
    q_stack_s = _stack_heads_sample(qn_s[0], G, R, S, S_PAD)
    pos_s = PAST + s_real
    dist_s = pos_s[:, None] - (order_s * NSA_CMP_BLOCK + NSA_CMP_BLOCK - 1)[None, :]
    bias_cmp_s = _group_rows(_bias_tiles(table, _masked_bucket(dist_s, dist_s >= 0), NSA_HEADS), G)[None]
    n_past_blk = PAST // NSA_SEL_BLOCK
    n_sel_s = n_past_blk + -(-S // NSA_SEL_BLOCK)
    sel_lanes = -(-n_sel_s // LANES) * LANES
    blk_s = jnp.arange(sel_lanes, dtype=i32)[None, :]
    cur_s = (pos_s // NSA_SEL_BLOCK)[:, None]
    forced_s = (blk_s == 0) | (blk_s == cur_s) | (blk_s == cur_s - 1)
    causal_s = (blk_s * NSA_SEL_BLOCK <= pos_s[:, None]) & (blk_s < n_sel_s)
    code_s = jnp.where(causal_s, jnp.where(forced_s, SELECT_BIG, 0.0), -SELECT_BIG).astype(F32)[None]
    o_cmp_ss, sel_mask_s = _cmp_select(q_stack_s[:, :, None], kvc_s, bias_cmp_s, code_s, S_PAD, n_sel_s, PAST)
    o_cmp_s = _unstack_heads_sample(o_cmp_ss[:, :, 0], R, S, S_PAD)

    W = pps * PAGE
    n_chunks = n_pages // pps
    sel_past = sel_mask_s[:, :, 0, :, :n_past_blk]
    key_blk_s = (jnp.arange(PAST, dtype=i32) // NSA_SEL_BLOCK).reshape(n_chunks, 1, W)
    expand_s = (jnp.arange(n_past_blk, dtype=i32)[None, :, None] == key_blk_s).astype(BF16)
    d_past = pos_s[:, None] - jnp.arange(PAST, dtype=i32)[None, :]
    bias_sel_s = _group_rows(_bias_tiles(table, _masked_bucket(d_past, d_past >= 0), NSA_HEADS), G)
    bias_sel_s = bias_sel_s.reshape(G, R * S_PAD, n_chunks, W).transpose(2, 0, 1, 3)
    d_new = s_real[:, None] - srow[None, :]
    bk_new = _masked_bucket(d_new, (d_new >= 0) & (srow[None, :] < S))
    nbias_nsa = _group_rows(_bias_tiles(table, bk_new, NSA_HEADS), G)
    sel_new8 = jnp.pad(sel_s.reshape(Bd, S, -1), ((0, 0), (0, S_PAD - S), (0, 0)))
    o_sel_ss = _sel_decode(page_table, q_stack_s, sel_past, expand_s, bias_sel_s,
                           cache_nsa_sel.reshape(n_pool, PAGE, -1), sel_new8, nbias_nsa, pps)
    o_sel_s = _unstack_heads_sample(o_sel_ss, R, S, S_PAD)

    def state_bias(wb, window, n_heads, n_groups):
        d = wb + s_real[:, None] - jnp.arange(wb, dtype=i32)[None, :]
        return _group_rows(_bias_tiles(table, _masked_bucket(d, d < window), n_heads), n_groups)

    wb0 = state_nsa_win.shape[1]
    win_new8 = jnp.pad(win_s.reshape(Bd, S, -1), ((0, 0), (0, S_PAD - S), (0, 0)))
    no_sink0_s = jnp.full((G, R * S_PAD, 1), NEG, F32)
    o_win_ss = _win_decode(q_stack_s, state_nsa_win.reshape(Bd, wb0, -1), win_new8,
                           state_bias(wb0, NSA_WINDOW, NSA_HEADS, G), nbias_nsa, no_sink0_s)
    o_win_s = _unstack_heads_sample(o_win_ss, R, S, S_PAD)

    eg = _gate_expand()
    w_out0 = w_out_0.astype(BF16)
    f = w_ffn_out_0.shape[0]
    ffn0 = (w_ffn_in_0[:, :f].astype(BF16), w_ffn_in_0[:, f:].astype(BF16), w_ffn_out_0.astype(BF16))
    y1_p = _out_ffn(x_prompt, [o_a_p, o_cmp_p, o_sel_p, o_win_p, gate_p], [eg, w_out0], mods0_p[2:], ffn0, None,
                    "ab_prompt", tm_p)
    y1_s = _out_ffn(xs, [o_full_s, o_cmp_s, o_sel_s, o_win_s, gate_s], [eg, w0["wuv_bd"], w_out0], mods0_s[2:], ffn0,
                    None, "ab_sample", tm_s)

    nq1 = SWA_HEADS * HEAD_DIM
    w1 = jnp.concatenate([w_in_1[:, :nq1] * ATTN_SCALE, w_in_1[:, nq1:]], axis=1).astype(BF16)
    q1_p, kv1_p, kvb1_p = _proj1(y1_p, mods1_p[0], mods1_p[1], w1, nq1, tm_p)
    q1_s, kv1_s, _ = _proj1(y1_s, mods1_s[0], mods1_s[1], w1, nq1, tm_s)
    bias_swa_p, nw1 = window_bias(SWA_WINDOW, SWA_HEADS, G1)
    sink_p = jnp.repeat(swa_sinks.astype(F32).reshape(G1, R1), tq, axis=1).reshape(G1, R1 * tq, 1)
    o_c_st = _banded(_stack_heads(q1_p, G1, R1, tq), kvb1_p, bias_swa_p, sink_p, tq, nw1)
    o_c_p = _unstack_heads(o_c_st, R1, tq)

    wb1 = state_swa.shape[1]
    kv_new8 = jnp.pad(kv1_s.reshape(Bd, S, -1), ((0, 0), (0, S_PAD - S), (0, 0)))
    d_new1 = s_real[:, None] - srow[None, :]
    nbias_swa = _group_rows(_bias_tiles(table, _masked_bucket(d_new1, (d_new1 >= 0) & (srow[None, :] < S)), SWA_HEADS),
                            G1)
    sink_s = jnp.repeat(swa_sinks.astype(F32).reshape(G1, R1), S_PAD, axis=1).reshape(G1, R1 * S_PAD, 1)
    o_c_ss = _win_decode(_stack_heads_sample(q1_s[0], G1, R1, S, S_PAD), state_swa.reshape(Bd, wb1, -1), kv_new8,
                         state_bias(wb1, SWA_WINDOW, SWA_HEADS, G1), nbias_swa, sink_s)
    o_c_s = _unstack_heads_sample(o_c_ss, R1, S, S_PAD)

    w_out1 = w_out_1.astype(BF16)
    f1 = w_ffn_out_1.shape[0]
    ffn1 = (w_ffn_in_1[:, :f1].astype(BF16), w_ffn_in_1[:, f1:].astype(BF16), w_ffn_out_1.astype(BF16))
    gain = final_norm.reshape(1, D).astype(F32)
    y_prompt = _out_ffn(y1_p, [o_c_p], [w_out1], mods1_p[2:], ffn1, gain, "c", tm_p)
    y_sample = _out_ffn(y1_s, [o_c_s], [w_out1], mods1_s[2:], ffn1, gain, "c", tm_s).reshape(Bd, S, D)

    row5 = lambda a, lead, g: a.reshape(lead + (g, 2, HEAD_DIM))
    win_p5 = row5(win_p, (B, T), G)
    kv1_p5 = row5(kv1_p, (B, T), G1)
    win_s5 = row5(win_s, (Bd, S), G)
    kv1_s5 = row5(kv1_s, (Bd, S), G1)
    return (y_prompt, y_sample, mla_p, mla_s.reshape(Bd, S, mla_w),
            row5(cmp_p, (B, T), G), row5(cmp_s, (Bd, S), G), row5(sel_p, (B, T), G), row5(sel_s, (Bd, S), G),
            win_p5[:, T - min(NSA_WINDOW, T):], jnp.concatenate([state_nsa_win, win_s5], axis=1)[:, S:],
            kv1_p5[:, T - min(SWA_WINDOW, T):], jnp.concatenate([state_swa, kv1_s5], axis=1)[:, S:])
```

```python
import functools
import math

import jax
import jax.numpy as jnp
from jax import lax
from jax.experimental import pallas as pl
from jax.experimental.pallas import tpu as pltpu

F32 = jnp.float32
BF16 = jnp.bfloat16

MLA_HEADS, MLA_NOPE, MLA_ROPE, MLA_V = 8, 64, 32, 64
ROPE_THETA = 10000.0
NSA_HEADS, NSA_KV_HEADS, HEAD_DIM = 8, 2, 64
NSA_CMP_BLOCK, NSA_SEL_BLOCK, NSA_TOP_N, NSA_WINDOW = 32, 64, 16, 512
SWA_HEADS, SWA_KV_HEADS, SWA_WINDOW = 16, 4, 128
REL_BUCKETS, REL_MAX_DISTANCE = 32, 128
Q_BLOCK = 128
NORM_EPS = 1e-6
NEG = -1e30
SELECT_BIG = 1e9
MLA_SCALE = (MLA_NOPE + MLA_ROPE) ** -0.5
ATTN_SCALE = HEAD_DIM ** -0.5

LANES = 128
VMEM_LIMIT = 52 * 1024 * 1024


def _cp(*sem):
    return pltpu.CompilerParams(dimension_semantics=sem, vmem_limit_bytes=VMEM_LIMIT)


def _dot(a, b):
    return jnp.dot(a, b, preferred_element_type=F32)


def _dot_t(a, b):
    return lax.dot_general(a, b, (((1,), (1,)), ((), ())), preferred_element_type=F32)


def _rms(x):
    return x * lax.rsqrt(jnp.mean(x * x, axis=-1, keepdims=True) + NORM_EPS)


def _sigmoid(x):
    return 1.0 / (1.0 + jnp.exp(-x))


def _const_spec(shape):
    n = len(shape)
    return pl.BlockSpec(shape, lambda *_: (0,) * n)


def _softmax_update(scores, values, m_ref, l_ref, acc_ref, transposed_values=False):
    m_prev = m_ref[...]
    m_new = m_prev
    for s in scores:
        m_new = jnp.maximum(m_new, jnp.max(s, axis=-1, keepdims=True))
    alpha = jnp.exp(m_prev - m_new)
    l_new = alpha * l_ref[...]
    acc = alpha * acc_ref[...]
    for s, v in zip(scores, values):
        p = jnp.exp(s - m_new)
        l_new = l_new + jnp.sum(p, axis=-1, keepdims=True)
        pb = p.astype(BF16)
        acc = acc + (_dot_t(pb, v) if transposed_values else _dot(pb, v))
    m_ref[...] = m_new
    l_ref[...] = l_new
    acc_ref[...] = acc


def _lane_tile(x, width):
    reps = width // x.shape[-1]
    return x if reps == 1 else jnp.concatenate([x] * reps, axis=1)


def _flash_step(scores, values, m_ref, acc_ref):
    m_prev = m_ref[...]
    m_cur = None
    for s in scores:
        mx = jnp.max(s, axis=-1, keepdims=True)
        m_cur = mx if m_cur is None else jnp.maximum(m_cur, mx)
    m_new = jnp.maximum(m_prev, m_cur)
    acc = acc_ref[...] * _lane_tile(jnp.exp(m_prev - m_new), acc_ref.shape[-1])
    for s, v in zip(scores, values):
        p = jnp.exp(s - _lane_tile(m_new, s.shape[-1]))
        acc = acc + _dot(p.astype(BF16), v)
    m_ref[...] = m_new
    acc_ref[...] = acc


def _softmax_init(m_ref, l_ref, acc_ref):
    m_ref[...] = jnp.full(m_ref.shape, NEG, F32)
    l_ref[...] = jnp.zeros(l_ref.shape, F32)
    acc_ref[...] = jnp.zeros(acc_ref.shape, F32)


def _ada_kernel(c_ref, w_ref, b_ref, o_ref):
    c = c_ref[...]
    a = (c * _sigmoid(c)).astype(BF16)
    o_ref[...] = _dot(a, w_ref[...]) + b_ref[...]


def _ada(c, w, b):
    M, D = c.shape
    N = w.shape[1]
    tn = 1024 if N % 1024 == 0 else N
    return pl.pallas_call(
        _ada_kernel, grid=(N // tn,),
        in_specs=[pl.BlockSpec((M, D), lambda j: (0, 0)), pl.BlockSpec((D, tn), lambda j: (0, j)),
                  pl.BlockSpec((1, tn), lambda j: (0, j))],
        out_specs=pl.BlockSpec((M, tn), lambda j: (0, j)),
        out_shape=jax.ShapeDtypeStruct((M, N), F32), compiler_params=_cp("parallel"), name="ada")(c, w, b)


def _bias_kernel(tab_ref, bkt_ref, o_ref):
    h = pl.program_id(0)
    bkt = bkt_ref[...]
    acc = jnp.full(bkt.shape, NEG, F32)
    for b in range(REL_BUCKETS):
        acc = jnp.where(bkt == b, tab_ref[b, h], acc)
    o_ref[0] = acc


def _bias_tiles(table, buckets, n_heads):
    R, C = buckets.shape
    tr = R
    for cand in (512, 256, 128):
        if R > cand and R % cand == 0:
            tr = cand
            break
    return pl.pallas_call(
        _bias_kernel,
        grid_spec=pltpu.PrefetchScalarGridSpec(
            num_scalar_prefetch=1, grid=(n_heads, R // tr),
            in_specs=[pl.BlockSpec((tr, C), lambda h, r, tab: (r, 0))],
            out_specs=pl.BlockSpec((1, tr, C), lambda h, r, tab: (h, r, 0))),
        out_shape=jax.ShapeDtypeStruct((n_heads, R, C), F32),
        compiler_params=_cp("parallel", "parallel"), name="rel_bias")(table, buckets)


def _t5_bucket(dist):
    n = jnp.maximum(dist, 0)
    exact = REL_BUCKETS // 2
    scaled = jnp.log(jnp.maximum(n, 1).astype(F32) / exact) / math.log(REL_MAX_DISTANCE / exact)
    large = jnp.minimum(exact + (scaled * (REL_BUCKETS - exact)).astype(jnp.int32), REL_BUCKETS - 1)
    return jnp.where(n < exact, n, large)


def _masked_bucket(dist, valid):
    return jnp.where(valid, _t5_bucket(dist), -1).astype(jnp.int32)


def _modulate(x, shift, scale):
    return _rms(x) * (1.0 + scale) + shift


def _proj0_kernel(x_ref, sh_ref, sc_ref, w0_ref, qn_ref, kvn_ref, wuq_ref, wuqs_ref, cq_ref, sq_ref, cm_ref,
                  sm_ref, wa_ref, wb_ref, vones_ref, *outs, sample):
    h = _modulate(x_ref[0], sh_ref[0], sc_ref[0]).astype(BF16)
    y = _dot(h, w0_ref[...])
    qn = (_rms(y[:, 0:256]) * qn_ref[...]).astype(BF16)
    ckv = _rms(y[:, 256:512]) * kvn_ref[...]
    misc = y[:, 512:640] * cm_ref[...] + y[:, 640:768] * sm_ref[...]
    cq = jnp.concatenate([cq_ref[...]] * MLA_HEADS, axis=1)
    sq = jnp.concatenate([sq_ref[...]] * MLA_HEADS, axis=1)
    q_rot = ((_dot(qn, wuq_ref[...]) * cq + _dot(qn, wuqs_ref[...]) * sq) * MLA_SCALE).astype(BF16)
    ckv_b = ckv.astype(BF16)
    if sample:
        rows_ref, qcat_ref, qnsa_ref, cmp_ref, sel_ref, win_ref, gate_ref = outs
        qcat_ref[0] = _dot(q_rot, wa_ref[...]).astype(BF16)
    else:
        rows_ref, qmla_ref, kcat_ref, vmla_ref, qnsa_ref, cmp_ref, sel_ref, win_ref, selb_ref, winb_ref, gate_ref = outs
        qmla_ref[0] = q_rot
        kcat_ref[0] = _dot(jnp.concatenate([ckv_b, misc.astype(BF16)], axis=1), wa_ref[...]).astype(BF16)
        vmla_ref[0] = (_dot(ckv_b, wb_ref[...]) + vones_ref[...]).astype(BF16)
        selb_ref[0] = y[:, 1536:1792].astype(BF16)
        winb_ref[0] = y[:, 1792:2048].astype(BF16)
    rows_ref[0, :, 0:256] = ckv
    rows_ref[0, :, 256:288] = misc[:, 0:MLA_ROPE]
    qnsa_ref[0] = y[:, 768:1280].astype(BF16)
    cmp_ref[0] = y[:, 1280:1536]
    sel_ref[0] = y[:, 1536:1792]
    win_ref[0] = y[:, 1792:2048]
    gate_ref[0] = _sigmoid(misc)


def _proj0(x, shift, scale, w, tabs, sample, tm):
    Bx, Tx, D = x.shape
    per_token = shift.shape[1] != 1
    mod_spec = (pl.BlockSpec((1, tm, D), lambda b, i: (b, i, 0)) if per_token
                else pl.BlockSpec((1, 1, D), lambda b, i: (b, 0, 0)))
    tok = lambda n: pl.BlockSpec((1, tm, n), lambda b, i: (b, i, 0))
    tab_spec = pl.BlockSpec((tm, LANES), lambda b, i: (i, 0))
    wa, wb = (w["a_abs"], w["wuv"]) if sample else (w["kcat"], w["wuv"])
    in_specs = [tok(D), mod_spec, mod_spec, _const_spec(w["w0"].shape), _const_spec((1, 256)), _const_spec((1, 256)),
                _const_spec(w["wuq"].shape), _const_spec(w["wuqs"].shape), tab_spec, tab_spec, tab_spec, tab_spec,
                _const_spec(wa.shape), _const_spec(wb.shape), _const_spec(w["vones"].shape)]
    sd = lambda n, dt: jax.ShapeDtypeStruct((Bx, Tx, n), dt)
    if sample:
        out_shape = [sd(288, F32), sd(MLA_HEADS * 288, BF16), sd(512, BF16), sd(256, F32), sd(256, F32), sd(256, F32),
                     sd(LANES, F32)]
    else:
        out_shape = [sd(288, F32), sd(1024, BF16), sd(1024, BF16), sd(1024, BF16), sd(512, BF16), sd(256, F32),
                     sd(256, F32), sd(256, F32), sd(256, BF16), sd(256, BF16), sd(LANES, F32)]
    out_specs = [tok(s.shape[-1]) for s in out_shape]
    return pl.pallas_call(
        functools.partial(_proj0_kernel, sample=sample), grid=(Bx, Tx // tm), in_specs=in_specs, out_specs=out_specs,
        out_shape=out_shape, compiler_params=_cp("parallel", "parallel"),
        name="proj0_sample" if sample else "proj0_prompt")(
            x, shift, scale, w["w0"], w["qn"], w["kvn"], w["wuq"], w["wuqs"], *tabs, wa, wb, w["vones"])


def _proj1_kernel(x_ref, sh_ref, sc_ref, w_ref, q_ref, kv_ref, kvb_ref):
    h = _modulate(x_ref[0], sh_ref[0], sc_ref[0]).astype(BF16)
    y = _dot(h, w_ref[...])
    nq = q_ref.shape[-1]
    q_ref[0] = y[:, :nq].astype(BF16)
    kv_ref[0] = y[:, nq:]
    kvb_ref[0] = y[:, nq:].astype(BF16)


def _proj1(x, shift, scale, w1, nq, tm):
    Bx, Tx, D = x.shape
    nkv = w1.shape[1] - nq
    per_token = shift.shape[1] != 1
    mod_spec = (pl.BlockSpec((1, tm, D), lambda b, i: (b, i, 0)) if per_token
                else pl.BlockSpec((1, 1, D), lambda b, i: (b, 0, 0)))
    tok = lambda n: pl.BlockSpec((1, tm, n), lambda b, i: (b, i, 0))
    return pl.pallas_call(
        _proj1_kernel, grid=(Bx, Tx // tm), in_specs=[tok(D), mod_spec, mod_spec, _const_spec(w1.shape)],
        out_specs=[tok(nq), tok(nkv), tok(nkv)],
        out_shape=[jax.ShapeDtypeStruct((Bx, Tx, nq), BF16), jax.ShapeDtypeStruct((Bx, Tx, nkv), F32),
                   jax.ShapeDtypeStruct((Bx, Tx, nkv), BF16)],
        compiler_params=_cp("parallel", "parallel"), name="proj1")(x, shift, scale, w1)


def _mla_prompt_kernel(q_ref, k_ref, v_ref, o_ref, m_sc, acc_sc, *, tq):
    i = pl.program_id(2)
    q = q_ref[0]
    causal = lax.broadcasted_iota(jnp.int32, (tq, tq), 1) <= lax.broadcasted_iota(jnp.int32, (tq, tq), 0)
    m_sc[...] = jnp.full(m_sc.shape, NEG, F32)
    acc_sc[...] = jnp.zeros(acc_sc.shape, F32)

    def step(j, masked):
        start = pl.multiple_of(j * tq, tq)
        kk = k_ref[0, pl.ds(start, tq), :]
        vv = v_ref[0, pl.ds(start, tq), :]
        for e in range(2):
            sl = slice(LANES * e, LANES * (e + 1))
            s = _dot_t(q[:, sl], kk[:, sl])
            if masked:
                s = jnp.where(causal, s, NEG)
            _flash_step([s], [vv[:, sl]], m_sc.at[e], acc_sc.at[e])

    def body(j, carry):
        step(j, False)
        return carry

    lax.fori_loop(0, i, body, 0)
    step(i, True)
    a0, a1 = acc_sc[0], acc_sc[1]
    o0 = a0 / pltpu.roll(a0, MLA_V, 1)
    o1 = a1 / pltpu.roll(a1, MLA_V, 1)
    lane = lax.broadcasted_iota(jnp.int32, (tq, LANES), 1)
    o_ref[0] = jnp.where(lane < MLA_V, o0, o1).astype(o_ref.dtype)


def _mla_prompt(q, k, v, tq):
    B, T, _ = q.shape
    return pl.pallas_call(
        functools.partial(_mla_prompt_kernel, tq=tq), grid=(B, MLA_HEADS // 2, T // tq),
        in_specs=[pl.BlockSpec((1, tq, 2 * LANES), lambda b, h, i: (b, i, h)),
                  pl.BlockSpec((1, T, 2 * LANES), lambda b, h, i: (b, 0, h)),
                  pl.BlockSpec((1, T, 2 * LANES), lambda b, h, i: (b, 0, h))],
        out_specs=pl.BlockSpec((1, tq, LANES), lambda b, h, i: (b, i, h)),
        out_shape=jax.ShapeDtypeStruct((B, T, MLA_HEADS * MLA_V), BF16),
        scratch_shapes=[pltpu.VMEM((2, tq, LANES), F32), pltpu.VMEM((2, tq, LANES), F32)],
        compiler_params=_cp("parallel", "parallel", "parallel"), name="mla_prompt")(q, k, v)


def _pe_bias_kernel(pe_ref, w_ref, o_ref):
    o_ref[...] = _dot(pe_ref[...].astype(BF16), w_ref[...])


def _pe_bias(pe_rows, wbig):
    return pl.pallas_call(_pe_bias_kernel, out_shape=jax.ShapeDtypeStruct((pe_rows.shape[0], wbig.shape[1]), F32),
                          compiler_params=_cp(), name="pe_bias")(pe_rows, wbig)


def _compress_prompt_kernel(x_ref, w_ref, peb_ref, o_ref):
    o_ref[0] = (_dot(x_ref[0].astype(BF16), w_ref[...]) + peb_ref[0:1, :]).astype(o_ref.dtype)


def _compress_prompt(x, wbig, pe_bias):
    B, NC, K = x.shape
    return pl.pallas_call(
        _compress_prompt_kernel, grid=(B,),
        in_specs=[pl.BlockSpec((1, NC, K), lambda b: (b, 0, 0)), _const_spec(wbig.shape), _const_spec(pe_bias.shape)],
        out_specs=pl.BlockSpec((1, NC, 256), lambda b: (b, 0, 0)),
        out_shape=jax.ShapeDtypeStruct((B, NC, 256), BF16), compiler_params=_cp("parallel"),
        name="compress_prompt")(x, wbig, pe_bias)


def _compress_paged_kernel(pt_ref, *refs, pps):
    page_refs = refs[:pps]
    perm_ref, w_ref, peb_ref, o_ref = refs[pps:]
    pairs = []
    for pr in range(pps // 2):
        both = jnp.concatenate([page_refs[2 * pr][0], page_refs[2 * pr + 1][0]], axis=1).astype(BF16)
        pairs.append(_dot_t(perm_ref[...], both))
    n_blk = o_ref.shape[1]
    acc = None
    for l in range(NSA_CMP_BLOCK):
        x = jnp.concatenate([p[8 * l:8 * (l + 1)] for p in pairs], axis=0)
        x = jnp.concatenate([x[:, :LANES], x[:, LANES:]], axis=0).astype(BF16)
        part = _dot(x, w_ref[l])
        acc = part if acc is None else acc + part
    y = jnp.concatenate([acc[:n_blk], acc[n_blk:]], axis=1)
    o_ref[0] = (y + peb_ref[0:1, :]).astype(o_ref.dtype)


def _compress_paged(page_table, cache_t, w_head, pe_bias, pps):
    Bd, n_pages = page_table.shape
    _, feat, page = cache_t.shape
    per_page = page // NSA_CMP_BLOCK
    assert per_page * 2 == 8 and feat == 2 * LANES, "a pair of pages must hold one sublane tile of blocks"
    n_blk = pps * per_page
    r = jnp.arange(2 * page)
    l, p2, n = r // 8, (r // per_page) % 2, r % per_page
    perm = (jnp.arange(2 * page)[None, :] == (p2 * page + n * NSA_CMP_BLOCK + l)[:, None]).astype(BF16)
    page_specs = [pl.BlockSpec((1, feat, page), lambda b, c, pt, k=k: (pt[b, c * pps + k], 0, 0)) for k in range(pps)]
    cst = lambda a: pl.BlockSpec(a.shape, lambda b, c, pt: (0,) * a.ndim)
    return pl.pallas_call(
        functools.partial(_compress_paged_kernel, pps=pps),
        grid_spec=pltpu.PrefetchScalarGridSpec(
            num_scalar_prefetch=1, grid=(Bd, n_pages // pps),
            in_specs=page_specs + [cst(perm), cst(w_head), cst(pe_bias)],
            out_specs=pl.BlockSpec((1, n_blk, feat), lambda b, c, pt: (b, c, 0))),
        out_shape=jax.ShapeDtypeStruct((Bd, n_pages * per_page, feat), BF16),
        compiler_params=_cp("parallel", "parallel"), name="compress_paged")(
            page_table, *([cache_t] * pps), perm, w_head, pe_bias)


def _cmp_select_kernel(q_ref, kv_ref, bias_ref, code_ref, o_ref, sel_ref, *, tq, n_rank, first_pos):
    i = pl.program_id(1)
    G, R = NSA_KV_HEADS, NSA_HEADS // NSA_KV_HEADS
    M = R * tq
    NC = kv_ref.shape[1]
    n_lanes = code_ref.shape[-1]
    code = code_ref[0]
    row_pos = first_pos + i * tq + lax.broadcasted_iota(jnp.int32, (M, 1), 0) % tq
    any_valid = (row_pos >= NSA_CMP_BLOCK - 1).astype(F32)
    lane = lax.broadcasted_iota(jnp.int32, (tq, n_lanes), 1)
    for g in range(G):
        kv = kv_ref[0, :, LANES * g:LANES * (g + 1)]
        s = _dot_t(q_ref[0, g, 0], kv) + bias_ref[0, g]
        m = jnp.max(s, axis=-1, keepdims=True)
        p = jnp.exp(s - m)
        p = p * (any_valid / jnp.sum(p, axis=-1, keepdims=True))
        o_ref[0, g, 0] = _dot(p.astype(BF16), kv).astype(o_ref.dtype)
        imp = p[0:tq]
        for r in range(1, R):
            imp = imp + p[r * tq:(r + 1) * tq]
        imp = imp[:, :NC // 2] + imp[:, NC // 2:]
        if n_lanes > NC // 2:
            imp = jnp.concatenate([imp, jnp.zeros((tq, n_lanes - NC // 2), F32)], axis=1)
        score = jnp.where(code == 0.0, imp, code)
        rank = jnp.zeros((tq, n_lanes), F32)
        for jp in range(n_rank):
            col = score[:, jp:jp + 1]
            ahead = jnp.where(col > score, 1.0, jnp.where(col == score, jnp.where(lane > jp, 1.0, 0.0), 0.0))
            rank = rank + ahead
        sel_neg = jnp.where(rank < float(NSA_TOP_N), jnp.where(code > -0.5 * SELECT_BIG, 0.0, NEG), NEG)
        sel_ref[0, g, 0] = sel_neg.astype(sel_ref.dtype)


def _cmp_select(q_stack, kvc, bias, code, tq, n_rank, first_pos):
    B, G, nq, M, _ = q_stack.shape
    NC = kvc.shape[1]
    n_lanes = code.shape[-1]
    return pl.pallas_call(
        functools.partial(_cmp_select_kernel, tq=tq, n_rank=n_rank, first_pos=first_pos), grid=(B, nq),
        in_specs=[pl.BlockSpec((1, G, 1, M, LANES), lambda b, i: (b, 0, i, 0, 0)),
                  pl.BlockSpec((1, NC, G * LANES), lambda b, i: (b, 0, 0)),
                  pl.BlockSpec((1, G, M, NC), lambda b, i: (i, 0, 0, 0)),
                  pl.BlockSpec((1, tq, n_lanes), lambda b, i: (i, 0, 0))],
        out_specs=[pl.BlockSpec((1, G, 1, M, LANES), lambda b, i: (b, 0, i, 0, 0)),
                   pl.BlockSpec((1, G, 1, tq, n_lanes), lambda b, i: (b, 0, i, 0, 0))],
        out_shape=[jax.ShapeDtypeStruct((B, G, nq, M, LANES), BF16),
                   jax.ShapeDtypeStruct((B, G, nq, tq, n_lanes), BF16)],
        compiler_params=_cp("parallel", "parallel"), name="cmp_select")(q_stack, kvc, bias, code)


def _sel_prompt_kernel(q_ref, kv_ref, oh_ref, bnear_ref, bfar_ref, o_ref, m_sc, acc_sc, *, tq, tk):
    i = pl.program_id(1)
    G = NSA_KV_HEADS
    M = q_ref.shape[3]
    far_end = jnp.maximum(i - 1, 0) * tq
    n_full = far_end // tk
    rem = far_end - n_full * tk
    ones = jnp.ones((tk, LANES), BF16)
    for g in range(G):
        q = q_ref[0, g, 0]
        m_sc[...] = jnp.full(m_sc.shape, NEG, F32)
        acc_sc[...] = jnp.zeros(acc_sc.shape, F32)

        def chunk(start, size, q=q, g=g):
            keys = kv_ref[0, pl.ds(start, size), LANES * g:LANES * (g + 1)]
            s = _dot_t(q, jnp.concatenate([keys, oh_ref[pl.ds(start, size), :]], axis=1))
            return s, jnp.concatenate([keys, ones[:size]], axis=1)

        def body(j, carry, chunk=chunk):
            s, v = chunk(pl.multiple_of(j * tk, tk), tk)
            _flash_step([s], [v], m_sc, acc_sc)
            return carry

        lax.fori_loop(0, n_full, body, 0)

        @pl.when(rem > 0)
        def _(chunk=chunk):
            s, v = chunk(pl.multiple_of(n_full * tk, tk), tk)
            s = jnp.where(lax.broadcasted_iota(jnp.int32, (M, tk), 1) < rem, s, NEG)
            _flash_step([s], [v], m_sc, acc_sc)

        far = bfar_ref[g]
        s0, v0 = chunk(pl.multiple_of(jnp.maximum(i - 1, 0) * tq, tq), tq)
        s0 = s0 + (bnear_ref[g, :, 0:tq] - far) + jnp.where(i >= 1, 0.0, NEG)
        s1, v1 = chunk(pl.multiple_of(i * tq, tq), tq)
        s1 = s1 + (bnear_ref[g, :, tq:2 * tq] - far)
        _flash_step([s0, s1], [v0, v1], m_sc, acc_sc)
        acc = acc_sc[...]
        o_ref[0, g, 0] = (acc[:, :LANES] / acc[:, LANES:]).astype(o_ref.dtype)


def _sel_prompt(q_aug, kv, onehot, bias_near, bias_far, tq, tk):
    B, G, nq, M, W = q_aug.shape
    T = kv.shape[1]
    return pl.pallas_call(
        functools.partial(_sel_prompt_kernel, tq=tq, tk=tk), grid=(B, nq),
        in_specs=[pl.BlockSpec((1, G, 1, M, W), lambda b, i: (b, 0, i, 0, 0)),
                  pl.BlockSpec((1, T, G * LANES), lambda b, i: (b, 0, 0)),
                  _const_spec(onehot.shape), _const_spec(bias_near.shape), _const_spec(bias_far.shape)],
        out_specs=pl.BlockSpec((1, G, 1, M, LANES), lambda b, i: (b, 0, i, 0, 0)),
        out_shape=jax.ShapeDtypeStruct((B, G, nq, M, LANES), BF16),
        scratch_shapes=[pltpu.VMEM((M, LANES), F32), pltpu.VMEM((M, 2 * LANES), F32)],
        compiler_params=_cp("parallel", "parallel"), name="sel_prompt")(q_aug, kv, onehot, bias_near, bias_far)


def _cmp_select_t_kernel(q_ref, kv_ref, kvt_ref, bias_ref, code_ref, o_ref, sel_ref, *, tq):
    i = pl.program_id(1)
    G, R = NSA_KV_HEADS, NSA_HEADS // NSA_KV_HEADS
    M = R * tq
    NC = kv_ref.shape[1]
    n_sel = code_ref.shape[1]
    code = code_ref[0]
    col_pos = i * tq + lax.broadcasted_iota(jnp.int32, (1, M), 1) % tq
    any_valid = (col_pos >= NSA_CMP_BLOCK - 1).astype(F32)
    blk = lax.broadcasted_iota(jnp.int32, (n_sel, tq), 0)
    for g in range(G):
        kv = kv_ref[0, :, LANES * g:LANES * (g + 1)]
        s = _dot_t(kv, q_ref[0, g, 0]) + bias_ref[0, g]
        p = jnp.exp(s - jnp.max(s, axis=0, keepdims=True))
        p = p * (any_valid / jnp.sum(p, axis=0, keepdims=True))
        o_ref[0, g, 0] = _dot(kvt_ref[0, LANES * g:LANES * (g + 1), :], p.astype(BF16)).astype(o_ref.dtype)
        imp = p[:, 0:tq]
        for r in range(1, R):
            imp = imp + p[:, r * tq:(r + 1) * tq]
        imp = imp[:NC // 2] + imp[NC // 2:]
        score = jnp.where(code == 0.0, imp, code)
        rank = jnp.zeros((n_sel, tq), F32)
        for jp in range(n_sel):
            row = score[jp:jp + 1, :]
            rank = rank + jnp.where(row > score, 1.0, jnp.where(row == score, jnp.where(blk > jp, 1.0, 0.0), 0.0))
        sel_neg = jnp.where(rank < float(NSA_TOP_N), jnp.where(code > -0.5 * SELECT_BIG, 0.0, NEG), NEG)
        sel_ref[0, g, 0] = sel_neg.astype(sel_ref.dtype)


def _cmp_select_t(q_stack, kvc, kvc_t, bias_t, code_t, tq):
    B, G, nq, M, _ = q_stack.shape
    NC = kvc.shape[1]
    n_sel = code_t.shape[1]
    return pl.pallas_call(
        functools.partial(_cmp_select_t_kernel, tq=tq), grid=(B, nq),
        in_specs=[pl.BlockSpec((1, G, 1, M, LANES), lambda b, i: (b, 0, i, 0, 0)),
                  pl.BlockSpec((1, NC, G * LANES), lambda b, i: (b, 0, 0)),
                  pl.BlockSpec((1, G * LANES, NC), lambda b, i: (b, 0, 0)),
                  pl.BlockSpec((1, G, NC, M), lambda b, i: (i, 0, 0, 0)),
                  pl.BlockSpec((1, n_sel, tq), lambda b, i: (i, 0, 0))],
        out_specs=[pl.BlockSpec((1, G, 1, LANES, M), lambda b, i: (b, 0, i, 0, 0)),
                   pl.BlockSpec((1, G, 1, n_sel, tq), lambda b, i: (b, 0, i, 0, 0))],
        out_shape=[jax.ShapeDtypeStruct((B, G, nq, LANES, M), BF16),
                   jax.ShapeDtypeStruct((B, G, nq, n_sel, tq), BF16)],
        compiler_params=_cp("parallel", "parallel"), name="cmp_select_t")(q_stack, kvc, kvc_t, bias_t, code_t)


def _banded_kernel(q_ref, kv_ref, bias_ref, sink_ref, o_ref, *, tq, nw, G):
    i = pl.program_id(1)
    for g in range(G):
        q = q_ref[0, g, 0]
        scores, vals = [], []
        for jj in range(nw + 1):
            kb = i - nw + jj
            start = pl.multiple_of(jnp.maximum(kb, 0) * tq, tq)
            kv = kv_ref[0, pl.ds(start, tq), LANES * g:LANES * (g + 1)]
            edge = jnp.where(kb >= 0, 0.0, NEG)
            scores.append(_dot_t(q, kv) + bias_ref[g, :, tq * jj:tq * (jj + 1)] + edge)
            vals.append(kv)
        sink = sink_ref[g]
        m = sink
        for s in scores:
            m = jnp.maximum(m, jnp.max(s, axis=-1, keepdims=True))
        l = jnp.exp(sink - m)
        acc = None
        for s, v in zip(scores, vals):
            p = jnp.exp(s - m)
            l = l + jnp.sum(p, axis=-1, keepdims=True)
            pv = _dot(p.astype(BF16), v)
            acc = pv if acc is None else acc + pv
        o_ref[0, g, 0] = (acc / l).astype(o_ref.dtype)


def _banded(q_stack, kv, bias, sinks, tq, nw):
    B, G, nq, M, _ = q_stack.shape
    T = kv.shape[1]
    return pl.pallas_call(
        functools.partial(_banded_kernel, tq=tq, nw=nw, G=G), grid=(B, nq),
        in_specs=[pl.BlockSpec((1, G, 1, M, LANES), lambda b, i: (b, 0, i, 0, 0)),
                  pl.BlockSpec((1, T, G * LANES), lambda b, i: (b, 0, 0)),
                  _const_spec(bias.shape), _const_spec(sinks.shape)],
        out_specs=pl.BlockSpec((1, G, 1, M, LANES), lambda b, i: (b, 0, i, 0, 0)),
        out_shape=jax.ShapeDtypeStruct((B, G, nq, M, LANES), BF16),
        compiler_params=_cp("parallel", "parallel"), name="banded")(q_stack, kv, bias, sinks)


def _mla_decode_kernel(pt_ref, q_ref, *refs, pps):
    page_refs = refs[:pps]
    new_ref, nmask_ref, o_ref, m_sc, l_sc, acc_sc = refs[pps:]
    c = pl.program_id(1)

    @pl.when(c == 0)
    def _():
        _softmax_init(m_sc, l_sc, acc_sc)

    q = q_ref[0]
    rows = [r[0].astype(BF16) for r in page_refs]
    nv = acc_sc.shape[-1]
    _softmax_update([_dot(q, r) for r in rows], [r[:nv] for r in rows], m_sc, l_sc, acc_sc, transposed_values=True)

    @pl.when(c == pl.num_programs(1) - 1)
    def _():
        nr = new_ref[0].astype(BF16)
        _softmax_update([_dot_t(q, nr) + nmask_ref[...]], [nr[:, :nv]], m_sc, l_sc, acc_sc)
        o_ref[0] = (acc_sc[...] / l_sc[...]).astype(o_ref.dtype)


def _mla_decode(page_table, q_cat, cache_t, new_rows, new_mask, pps):
    Bd, n_pages = page_table.shape
    _, width, page = cache_t.shape
    M = q_cat.shape[1]
    nv = width - MLA_ROPE
    page_specs = [pl.BlockSpec((1, width, page), lambda b, c, pt, k=k: (pt[b, c * pps + k], 0, 0)) for k in range(pps)]
    return pl.pallas_call(
        functools.partial(_mla_decode_kernel, pps=pps),
        grid_spec=pltpu.PrefetchScalarGridSpec(
            num_scalar_prefetch=1, grid=(Bd, n_pages // pps),
            in_specs=[pl.BlockSpec((1, M, width), lambda b, c, pt: (b, 0, 0))] + page_specs + [
                pl.BlockSpec((1,) + new_rows.shape[1:], lambda b, c, pt: (b, 0, 0)),
                pl.BlockSpec(new_mask.shape, lambda b, c, pt: (0, 0))],
            out_specs=pl.BlockSpec((1, M, nv), lambda b, c, pt: (b, 0, 0)),
            scratch_shapes=[pltpu.VMEM((M, 1), F32), pltpu.VMEM((M, 1), F32), pltpu.VMEM((M, nv), F32)]),
        out_shape=jax.ShapeDtypeStruct((Bd, M, nv), BF16),
        compiler_params=_cp("parallel", "arbitrary"), name="mla_decode")(
            page_table, q_cat, *([cache_t] * pps), new_rows, new_mask)


def _sel_decode_kernel(pt_ref, q_ref, oh_ref, blast_ref, bfar_ref, *refs, pps):
    page_refs = refs[:pps]
    new_ref, nbias_ref, o_ref, m_sc, l_sc, acc_sc = refs[pps:]
    c = pl.program_id(1)
    last = c == pl.num_programs(1) - 1
    G = NSA_KV_HEADS

    @pl.when(c == 0)
    def _():
        _softmax_init(m_sc, l_sc, acc_sc)

    M = q_ref.shape[2]
    rows = lambda g: slice(M * g, M * (g + 1))
    scores, vals = [], []
    for k in range(pps):
        kv_t = [page_refs[k][0, LANES * g:LANES * (g + 1), :].astype(BF16) for g in range(G)]
        onehot = oh_ref[c * pps + k]
        s = jnp.concatenate([_dot(q_ref[0, g], jnp.concatenate([kv_t[g], onehot], axis=0)) for g in range(G)], axis=0)
        if k == pps - 1:
            s = s + jnp.where(last, blast_ref[...] - bfar_ref[...], 0.0)
        scores.append(s)
        vals.append(kv_t)
    m_prev = m_sc[...]
    m_new = m_prev
    for s in scores:
        m_new = jnp.maximum(m_new, jnp.max(s, axis=-1, keepdims=True))
    alpha = jnp.exp(m_prev - m_new)
    l_new = alpha * l_sc[...]
    acc = alpha * acc_sc[...]
    for s, kv_t in zip(scores, vals):
        p = jnp.exp(s - m_new)
        l_new = l_new + jnp.sum(p, axis=-1, keepdims=True)
        pb = p.astype(BF16)
        acc = acc + jnp.concatenate([_dot_t(pb[rows(g)], kv_t[g]) for g in range(G)], axis=0)
    m_sc[...] = m_new
    l_sc[...] = l_new
    acc_sc[...] = acc

    @pl.when(last)
    def _():
        kv = [new_ref[0, :, LANES * g:LANES * (g + 1)].astype(BF16) for g in range(G)]
        n_new = kv[0].shape[0]
        s = jnp.concatenate([_dot_t(q_ref[0, g][:, :LANES], kv[g]) for g in range(G)], axis=0)
        s = s + (nbias_ref[...] - bfar_ref[:, :n_new])
        m_fin = jnp.maximum(m_sc[...], jnp.max(s, axis=-1, keepdims=True))
        a_fin = jnp.exp(m_sc[...] - m_fin)
        p = jnp.exp(s - m_fin)
        l_fin = a_fin * l_sc[...] + jnp.sum(p, axis=-1, keepdims=True)
        pb = p.astype(BF16)
        out = a_fin * acc_sc[...] + jnp.concatenate([_dot(pb[rows(g)], kv[g]) for g in range(G)], axis=0)
        out = out / l_fin
        for g in range(G):
            o_ref[0, g] = out[rows(g)].astype(o_ref.dtype)


def _sel_decode(page_table, q_aug, onehot_t, bias_last, bias_far, cache_t, new_rows, new_bias, pps):
    Bd, n_pages = page_table.shape
    _, feat, page = cache_t.shape
    _, G, M, _ = q_aug.shape
    page_specs = [pl.BlockSpec((1, feat, page), lambda b, c, pt, k=k: (pt[b, c * pps + k], 0, 0)) for k in range(pps)]
    cst = lambda a: pl.BlockSpec(a.shape, lambda b, c, pt: (0,) * a.ndim)
    per_b = lambda a: pl.BlockSpec((1,) + a.shape[1:], lambda b, c, pt: (b,) + (0,) * (a.ndim - 1))
    return pl.pallas_call(
        functools.partial(_sel_decode_kernel, pps=pps),
        grid_spec=pltpu.PrefetchScalarGridSpec(
            num_scalar_prefetch=1, grid=(Bd, n_pages // pps),
            in_specs=[per_b(q_aug), cst(onehot_t), cst(bias_last), cst(bias_far)] + page_specs
            + [per_b(new_rows), cst(new_bias)],
            out_specs=pl.BlockSpec((1, G, M, LANES), lambda b, c, pt: (b, 0, 0, 0)),
            scratch_shapes=[pltpu.VMEM((G * M, 1), F32), pltpu.VMEM((G * M, 1), F32), pltpu.VMEM((G * M, LANES), F32)]),
        out_shape=jax.ShapeDtypeStruct((Bd, G, M, LANES), BF16),
        compiler_params=_cp("parallel", "arbitrary"), name="sel_decode")(
            page_table, q_aug, onehot_t, bias_last, bias_far, *([cache_t] * pps), new_rows, new_bias)


def _win_decode_kernel(q_ref, buf_ref, new_ref, bias_ref, nbias_ref, sink_ref, o_ref, *, G):
    for g in range(G):
        q = q_ref[0, g]
        kb = buf_ref[0, LANES * g:LANES * (g + 1), :].astype(BF16)
        kn = new_ref[0, :, LANES * g:LANES * (g + 1)].astype(BF16)
        sb = _dot(q, kb) + bias_ref[g]
        sn = _dot_t(q, kn) + nbias_ref[g]
        sink = sink_ref[g]
        m = jnp.maximum(jnp.maximum(jnp.max(sb, axis=-1, keepdims=True), jnp.max(sn, axis=-1, keepdims=True)), sink)
        pb = jnp.exp(sb - m)
        pn = jnp.exp(sn - m)
        l = jnp.exp(sink - m) + jnp.sum(pb, axis=-1, keepdims=True) + jnp.sum(pn, axis=-1, keepdims=True)
        acc = _dot_t(pb.astype(BF16), kb) + _dot(pn.astype(BF16), kn)
        o_ref[0, g] = (acc / l).astype(o_ref.dtype)


def _win_decode(q_stack, buf, new_rows, bias, new_bias, sinks):
    Bd, G, M, _ = q_stack.shape
    per_b = lambda a: pl.BlockSpec((1,) + a.shape[1:], lambda b: (b,) + (0,) * (a.ndim - 1))
    return pl.pallas_call(
        functools.partial(_win_decode_kernel, G=G), grid=(Bd,),
        in_specs=[per_b(q_stack), per_b(buf), per_b(new_rows), _const_spec(bias.shape), _const_spec(new_bias.shape),
                  _const_spec(sinks.shape)],
        out_specs=pl.BlockSpec((1, G, M, LANES), lambda b: (b, 0, 0, 0)),
        out_shape=jax.ShapeDtypeStruct((Bd, G, M, LANES), BF16),
        compiler_params=_cp("parallel"), name="win_decode")(q_stack, buf, new_rows, bias, new_bias, sinks)


def _ffn_chunk(f):
    for cand in range(min(f, 1536) // LANES, 0, -1):
        if f % (cand * LANES) == 0:
            return cand * LANES
    return f


def _out_ffn_kernel(*refs, mode, final, n_mods):
    it = iter(refs)
    y_ref = next(it)
    if mode == "c":
        attn_in = next(it)[0]
    else:
        oa_ref, ocmp_ref, osel_ref, owin_ref, gate_ref, eg_ref = (next(it) for _ in range(6))
        if mode == "ab_sample":
            wuv_ref = next(it)
    wout_ref = next(it)
    gt_ref, fsh_ref, fsc_ref, fgt_ref = (next(it) for _ in range(4))
    wg_ref, wu_ref, wo_ref = (next(it) for _ in range(3))
    gain_ref = next(it) if final else None
    o_ref = next(it)

    if mode == "c":
        attn = _dot(attn_in, wout_ref[...])
    else:
        gates = gate_ref[0]
        g_hi = gates.astype(BF16)
        g_lo = (gates - g_hi.astype(F32)).astype(BF16)
        o_b = None
        for br, ref in enumerate((ocmp_ref, osel_ref, owin_ref)):
            ge = _dot(g_hi, eg_ref[br]) + _dot(g_lo, eg_ref[br])
            term = ge * ref[0].astype(F32)
            o_b = term if o_b is None else o_b + term
        o_a = oa_ref[0]
        if mode == "ab_sample":
            o_a = _dot(o_a, wuv_ref[...]).astype(BF16)
        na = o_a.shape[-1]
        attn = _dot(o_a, wout_ref[0:na, :]) + _dot(o_b.astype(BF16), wout_ref[na:, :])
    y1 = y_ref[0] + gt_ref[0] * attn
    h = _modulate(y1, fsh_ref[0], fsc_ref[0]).astype(BF16)
    f = wg_ref.shape[1]
    fc = _ffn_chunk(f)
    acc = None
    for k in range(f // fc):
        g = _dot(h, wg_ref[:, fc * k:fc * (k + 1)])
        u = _dot(h, wu_ref[:, fc * k:fc * (k + 1)])
        a = (g * _sigmoid(g) * u).astype(BF16)
        part = _dot(a, wo_ref[fc * k:fc * (k + 1), :])
        acc = part if acc is None else acc + part
    y2 = y1 + fgt_ref[0] * acc
    if final:
        y2 = _rms(y2) * gain_ref[...]
    o_ref[0] = y2


def _out_ffn(y, attn_parts, weights, mods, ffn_w, gain, mode, tm):
    Bx, Tx, D = y.shape
    per_token = mods[0].shape[1] != 1
    mod_spec = (pl.BlockSpec((1, tm, D), lambda b, i: (b, i, 0)) if per_token
                else pl.BlockSpec((1, 1, D), lambda b, i: (b, 0, 0)))
    tok = lambda a: pl.BlockSpec((1, tm, a.shape[-1]), lambda b, i: (b, i, 0))
    single = lambda a: pl.BlockSpec(a.shape, lambda b, i: (0,) * a.ndim, pipeline_mode=pl.Buffered(1))
    args = [y] + list(attn_parts) + list(weights) + list(mods) + list(ffn_w)
    in_specs = ([tok(y)] + [tok(a) for a in attn_parts] + [single(a) for a in weights] + [mod_spec] * 4
                + [single(a) for a in ffn_w])
    final = gain is not None
    if final:
        args.append(gain)
        in_specs.append(single(gain))
    return pl.pallas_call(
        functools.partial(_out_ffn_kernel, mode=mode, final=final, n_mods=4), grid=(Bx, Tx // tm),
        in_specs=in_specs, out_specs=tok(y), out_shape=jax.ShapeDtypeStruct(y.shape, F32),
        compiler_params=_cp("parallel", "parallel"), name="out_ffn_" + mode)(*args)


def _stack_heads(q, G, R, tq):
    B, T, _ = q.shape
    q = q.reshape(B, T // tq, tq, G, R, HEAD_DIM).transpose(0, 3, 1, 4, 2, 5).reshape(B, G, T // tq, R * tq, HEAD_DIM)
    return jnp.pad(q, ((0, 0),) * 4 + ((0, LANES - HEAD_DIM),))


def _unstack_heads(o, R, tq):
    B, G, nq, _, _ = o.shape
    o = o[..., HEAD_DIM:].reshape(B, G, nq, R, tq, HEAD_DIM).transpose(0, 2, 4, 1, 3, 5)
    return o.reshape(B, nq * tq, G * R * HEAD_DIM)


def _stack_heads_sample(q, G, R, S, s_pad):
    Bd = q.shape[0] // S
    q = q.reshape(Bd, S, G, R, HEAD_DIM).transpose(0, 2, 3, 1, 4)
    q = jnp.pad(q, ((0, 0), (0, 0), (0, 0), (0, s_pad - S), (0, LANES - HEAD_DIM)))
    return q.reshape(Bd, G, R * s_pad, LANES)


def _unstack_heads_sample(o, R, S, s_pad):
    Bd, G, _, _ = o.shape
    o = o[..., HEAD_DIM:].reshape(Bd, G, R, s_pad, HEAD_DIM)[:, :, :, :S].transpose(0, 3, 1, 2, 4)
    return o.reshape(1, Bd * S, G * R * HEAD_DIM)


def _group_rows(t, G):
    H, rows, C = t.shape
    return t.reshape(G, (H // G) * rows, C)


def _even_odd(n):
    return jnp.concatenate([jnp.arange(0, n, 2), jnp.arange(1, n, 2)]).astype(jnp.int32)


def _rope_tables(pos):
    half = MLA_ROPE // 2
    freq = ROPE_THETA ** (-jnp.arange(half, dtype=F32) / half)
    ang = pos.astype(F32)[:, None] * freq[None, :]
    cos, sin = jnp.cos(ang), jnp.sin(ang)
    n = pos.shape[0]
    one, zero = jnp.ones, jnp.zeros
    cq = jnp.concatenate([one((n, MLA_NOPE), F32), cos, cos, zero((n, LANES - MLA_NOPE - MLA_ROPE), F32)], axis=1)
    sq = jnp.concatenate([zero((n, MLA_NOPE), F32), sin, sin, zero((n, LANES - MLA_NOPE - MLA_ROPE), F32)], axis=1)
    n_gate = 3 * NSA_HEADS
    cm = jnp.concatenate([cos, cos, one((n, n_gate), F32), zero((n, LANES - MLA_ROPE - n_gate), F32)], axis=1)
    sm = jnp.concatenate([sin, sin, zero((n, LANES - MLA_ROPE), F32)], axis=1)
    return cq, sq, cm, sm


def _layer0_weights(w_in_0, mla_q_norm, mla_w_uq, mla_kv_norm, mla_w_uk, mla_w_uv):
    D = w_in_0.shape[0]
    qr, kvr = mla_q_norm.shape[0], mla_kv_norm.shape[0]
    half = MLA_ROPE // 2
    o_kr = qr + kvr
    o_q = o_kr + MLA_ROPE
    o_cmp = o_q + NSA_HEADS * HEAD_DIM
    kvw = NSA_KV_HEADS * 2 * HEAD_DIM
    o_g = o_cmp + 3 * kvw
    w_kr = w_in_0[:, o_kr:o_q]
    w_g = w_in_0[:, o_g:]
    z = lambda n: jnp.zeros((D, n), F32)
    misc_a = jnp.concatenate([w_kr, w_g, z(LANES - MLA_ROPE - w_g.shape[1])], axis=1)
    misc_b = jnp.concatenate([-w_kr[:, half:], w_kr[:, :half], z(LANES - MLA_ROPE)], axis=1)
    w0 = jnp.concatenate([w_in_0[:, :o_kr], misc_a, misc_b, w_in_0[:, o_q:o_cmp] * ATTN_SCALE, w_in_0[:, o_cmp:o_g]],
                         axis=1).astype(BF16)
    H = MLA_HEADS
    wq = mla_w_uq.reshape(qr, H, MLA_NOPE + MLA_ROPE)
    nope, x1, x2 = wq[..., :MLA_NOPE], wq[..., MLA_NOPE:MLA_NOPE + half], wq[..., MLA_NOPE + half:]
    zq = lambda n: jnp.zeros((qr, H, n), F32)
    pad = LANES - MLA_NOPE - MLA_ROPE
    wuq = jnp.concatenate([nope, x1, x2, zq(pad)], axis=-1).reshape(qr, H * LANES).astype(BF16)
    wuqs = jnp.concatenate([zq(MLA_NOPE), -x2, x1, zq(pad)], axis=-1).reshape(qr, H * LANES).astype(BF16)
    k_top = jnp.pad(mla_w_uk, ((0, 0), (0, 0), (0, LANES - MLA_NOPE))).reshape(kvr, H * LANES)
    place = jnp.zeros((LANES, H, LANES), F32).at[jnp.arange(MLA_ROPE), :, MLA_NOPE + jnp.arange(MLA_ROPE)].set(1.0)
    kcat = jnp.concatenate([k_top, place.reshape(LANES, H * LANES)], axis=0).astype(BF16)
    even = (jnp.arange(H) % 2 == 0)[None, :, None]
    zv = jnp.zeros((kvr, H, LANES - MLA_V), F32)
    wuv = jnp.where(even, jnp.concatenate([mla_w_uv, zv], axis=-1), jnp.concatenate([zv, mla_w_uv], axis=-1))
    wuv = wuv.reshape(kvr, H * LANES).astype(BF16)
    ones_v, zero_v = jnp.ones((1, H, MLA_V), F32), jnp.zeros((1, H, MLA_V), F32)
    vones = jnp.where(even, jnp.concatenate([zero_v, ones_v], axis=-1), jnp.concatenate([ones_v, zero_v], axis=-1))
    vones = vones.reshape(1, H * LANES)
    width = kvr + MLA_ROPE
    blk = jnp.zeros((H, LANES, width), F32)
    blk = blk.at[:, :MLA_NOPE, :kvr].set(jnp.transpose(mla_w_uk, (1, 2, 0)))
    blk = blk.at[:, MLA_NOPE + jnp.arange(MLA_ROPE), kvr + jnp.arange(MLA_ROPE)].set(1.0)
    eye = jnp.eye(H, dtype=F32)
    a_abs = jnp.einsum("hij,hk->hikj", blk, eye).reshape(H * LANES, H * width).astype(BF16)
    wuv_bd = jnp.einsum("chd,hk->hckd", mla_w_uv, eye).reshape(H * kvr, H * MLA_V).astype(BF16)
    return dict(w0=w0, qn=mla_q_norm.reshape(1, qr), kvn=mla_kv_norm.reshape(1, kvr), wuq=wuq, wuqs=wuqs, kcat=kcat,
                wuv=wuv, vones=vones, a_abs=a_abs, wuv_bd=wuv_bd)


def _gate_expand():
    h = jnp.arange(NSA_HEADS)
    mats = []
    for br in range(3):
        m = jnp.zeros((LANES, NSA_HEADS, HEAD_DIM), F32).at[MLA_ROPE + 3 * h + br, h, :].set(1.0)
        mats.append(m.reshape(LANES, NSA_HEADS * HEAD_DIM))
    return jnp.stack(mats).astype(BF16)


def kernel(x_prompt, x_sample, cache_mla, cache_nsa_cmp, cache_nsa_sel, state_nsa_win, state_swa, page_table, c_prompt, c_sample, rel_bias_table, w_ada_0, b_ada_0, w_in_0, mla_q_norm, mla_w_uq, mla_kv_norm, mla_w_uk, mla_w_uv, nsa_w_cmp, nsa_pe_cmp, w_out_0, w_ffn_in_0, w_ffn_out_0, w_ada_1, b_ada_1, w_in_1, swa_sinks, w_out_1, w_ffn_in_1, w_ffn_out_1, final_norm):
    B, T, D = x_prompt.shape
    Bd, S, _ = x_sample.shape
    n_pool, PAGE, mla_w = cache_mla.shape
    n_pages = page_table.shape[1]
    PAST = n_pages * PAGE
    G, R = NSA_KV_HEADS, NSA_HEADS // NSA_KV_HEADS
    G1, R1 = SWA_KV_HEADS, SWA_HEADS // SWA_KV_HEADS
    tq = Q_BLOCK
    nq = T // tq
    NS = Bd * S
    S_PAD = 8
    tm_p = 512 if T % 512 == 0 else 256
    tm_s = 256 if NS % 256 == 0 else NS
    i32 = jnp.int32
    table = rel_bias_table.astype(F32)

    n_c = B + Bd
    c_all = jnp.pad(jnp.concatenate([c_prompt, c_sample], axis=0), ((0, (-n_c) % 8), (0, 0)))

    def mods_for(w_ada, b_ada):
        m = _ada(c_all, w_ada.astype(BF16), b_ada.reshape(1, -1))
        mp = [m[:B, k * D:(k + 1) * D][:, None, :] for k in range(6)]
        ms = [jnp.repeat(m[B:B + Bd, k * D:(k + 1) * D], S, axis=0)[None] for k in range(6)]
        return mp, ms

    mods0_p, mods0_s = mods_for(w_ada_0, b_ada_0)
    mods1_p, mods1_s = mods_for(w_ada_1, b_ada_1)
    xs = x_sample.reshape(1, NS, D)

    w0 = _layer0_weights(w_in_0, mla_q_norm, mla_w_uq, mla_kv_norm, mla_w_uk, mla_w_uv)
    tabs_p = _rope_tables(jnp.arange(T, dtype=i32))
    tabs_s = tuple(jnp.tile(t, (Bd, 1)) for t in _rope_tables(PAST + jnp.arange(S, dtype=i32)))
    (mla_p, q_mla, k_mla, v_mla, qn_p, cmp_p, sel_p, win_p, selb_p, winb_p, gate_p) = _proj0(
        x_prompt, mods0_p[0], mods0_p[1], w0, tabs_p, False, tm_p)
    (mla_s, qcat_s, qn_s, cmp_s, sel_s, win_s, gate_s) = _proj0(xs, mods0_s[0], mods0_s[1], w0, tabs_s, True, tm_s)

    o_a_p = _mla_prompt(q_mla, k_mla, v_mla, 512 if T % 512 == 0 else 256)

    eye2 = jnp.eye(2, dtype=F32)
    wbig = jnp.einsum("lcde,gh,ck->lgcdhke", nsa_w_cmp, eye2, eye2).reshape(NSA_CMP_BLOCK * 4 * HEAD_DIM, 4 * HEAD_DIM)
    wbig = wbig.astype(BF16)
    pe_rows = jnp.broadcast_to(nsa_pe_cmp[:, None], (NSA_CMP_BLOCK, G, 2, HEAD_DIM)).reshape(1, -1)
    pe_rows = jnp.broadcast_to(pe_rows, (8, pe_rows.shape[1]))
    pe_bias = _pe_bias(pe_rows, wbig)
    NC = T // NSA_CMP_BLOCK
    kvc_p = _compress_prompt(cmp_p.reshape(B, NC, -1), wbig, pe_bias)
    order_p = _even_odd(NC)
    kvc_p = kvc_p[:, order_p]
    q_stack_p = _stack_heads(qn_p, G, R, tq)

    qpos = jnp.arange(T, dtype=i32)
    dist = qpos[:, None] - (order_p * NSA_CMP_BLOCK + NSA_CMP_BLOCK - 1)[None, :]
    bias_cmp_p = _bias_tiles(table, _masked_bucket(dist, dist >= 0), NSA_HEADS)
    bias_cmp_p = bias_cmp_p.reshape(G, R, nq, tq, NC).transpose(2, 0, 1, 3, 4).reshape(nq, G, R * tq, NC)
    n_sel = T // NSA_SEL_BLOCK
    blk = jnp.arange(n_sel, dtype=i32)[None, :]
    cur = (qpos // NSA_SEL_BLOCK)[:, None]
    forced = (blk == 0) | (blk == cur) | (blk == cur - 1)
    causal = blk * NSA_SEL_BLOCK <= qpos[:, None]
    code_p = jnp.where(causal, jnp.where(forced, SELECT_BIG, 0.0), -SELECT_BIG).astype(F32).reshape(nq, tq, n_sel)
    assert n_sel <= LANES and n_sel == NC // 2, "selection blocks must fit one lane tile"
    o_cmp_t, sel_neg_t = _cmp_select_t(q_stack_p, kvc_p, kvc_p.transpose(0, 2, 1), bias_cmp_p.transpose(0, 1, 3, 2),
                                       code_p.transpose(0, 2, 1), tq)
    o_cmp_p = o_cmp_t[:, :, :, HEAD_DIM:, :].reshape(B, G, nq, HEAD_DIM, R, tq).transpose(0, 2, 5, 1, 4, 3)
    o_cmp_p = o_cmp_p.reshape(B, T, G * R * HEAD_DIM)

    sel_rows_p = jnp.pad(sel_neg_t.transpose(0, 1, 2, 4, 3), ((0, 0),) * 4 + ((0, LANES - n_sel),))
    q_aug_p = jnp.concatenate([q_stack_p, jnp.tile(sel_rows_p, (1, 1, 1, R, 1))], axis=-1)
    onehot_p = (jnp.arange(LANES, dtype=i32)[None, :] == (qpos // NSA_SEL_BLOCK)[:, None]).astype(BF16)
    ql = jnp.arange(tq, dtype=i32)[:, None]
    d_near = ql + tq - jnp.arange(2 * tq, dtype=i32)[None, :]
    bk_near = jnp.concatenate([_masked_bucket(d_near, d_near >= 0), jnp.full((tq, 2 * tq), REL_BUCKETS - 1, i32)], axis=0)
    near_far = _bias_tiles(table, bk_near, NSA_HEADS)
    bias_near_p = _group_rows(near_far[:, :tq], G)
    bias_far_p = _group_rows(near_far[:, tq:, :LANES], G)
    tk_sel = 512 if T % 512 == 0 else tq
    o_sel_st = _sel_prompt(q_aug_p, selb_p, onehot_p, bias_near_p, bias_far_p, tq, tk_sel)

    def window_bias(window, n_heads, n_groups):
        nw = -(-window // tq)
        dw = ql + nw * tq - jnp.arange((nw + 1) * tq, dtype=i32)[None, :]
        t = _bias_tiles(table, _masked_bucket(dw, (dw >= 0) & (dw < window)), n_heads)
        return _group_rows(t, n_groups), nw

    bias_win_p, nw0 = window_bias(NSA_WINDOW, NSA_HEADS, G)
    no_sink0 = jnp.full((G, R * tq, 1), NEG, F32)
    o_win_st = _banded(q_stack_p, winb_p, bias_win_p, no_sink0, tq, nw0)

    o_sel_p = _unstack_heads(o_sel_st, R, tq)
    o_win_p = _unstack_heads(o_win_st, R, tq)

    pps = 8 if n_pages % 8 == 0 else n_pages
    srow = jnp.arange(S_PAD, dtype=i32)
    s_real = jnp.minimum(srow, S - 1)
    q_cat = qcat_s.reshape(Bd, S * MLA_HEADS, mla_w)
    mla_new = jnp.pad(mla_s.reshape(Bd, S, mla_w), ((0, 0), (0, S_PAD - S), (0, 0)))
    s_of_row = jnp.repeat(jnp.arange(S, dtype=i32), MLA_HEADS)[:, None]
    new_mask = jnp.where((srow[None, :] <= s_of_row) & (srow[None, :] < S), 0.0, NEG).astype(F32)
    pps_m = 16 if n_pages % 16 == 0 else pps
    o_full_s = _mla_decode(page_table, q_cat, cache_mla.transpose(0, 2, 1), mla_new, new_mask, pps_m)
    o_full_s = o_full_s.reshape(1, NS, MLA_HEADS * (mla_w - MLA_ROPE))

    pps_c = 32 if n_pages % 32 == 0 else pps_m
    feat = G * 2 * HEAD_DIM
    cache_cmp_t = cache_nsa_cmp.transpose(0, 2, 3, 4, 1).reshape(n_pool, feat, PAGE)
    w_head = wbig.reshape(NSA_CMP_BLOCK, feat, feat)[:, :LANES, :LANES]
    kvc_s = _compress_paged(page_table, cache_cmp_t, w_head, pe_bias, pps_c)
    NCs = PAST // NSA_CMP_BLOCK
    order_s = _even_odd(NCs)
    kvc_s = kvc_s[:, order_s]
    q_stack_s = _stack_heads_sample(qn_s[0], G, R, S, S_PAD)
    pos_s = PAST + s_real
    dist_s = pos_s[:, None] - (order_s * NSA_CMP_BLOCK + NSA_CMP_BLOCK - 1)[None, :]
    bias_cmp_s = _group_rows(_bias_tiles(table, _masked_bucket(dist_s, dist_s >= 0), NSA_HEADS), G)[None]
    n_past_blk = PAST // NSA_SEL_BLOCK
    n_sel_s = n_past_blk + -(-S // NSA_SEL_BLOCK)
    sel_lanes = -(-n_sel_s // LANES) * LANES
    blk_s = jnp.arange(sel_lanes, dtype=i32)[None, :]
    cur_s = (pos_s // NSA_SEL_BLOCK)[:, None]
    forced_s = (blk_s == 0) | (blk_s == cur_s) | (blk_s == cur_s - 1)
    causal_s = (blk_s * NSA_SEL_BLOCK <= pos_s[:, None]) & (blk_s < n_sel_s)
    code_s = jnp.where(causal_s, jnp.where(forced_s, SELECT_BIG, 0.0), -SELECT_BIG).astype(F32)[None]
    o_cmp_ss, sel_mask_s = _cmp_select(q_stack_s[:, :, None], kvc_s, bias_cmp_s, code_s, S_PAD, n_sel_s, PAST)
    o_cmp_s = _unstack_heads_sample(o_cmp_ss[:, :, 0], R, S, S_PAD)

    assert n_past_blk <= LANES and PAGE >= REL_MAX_DISTANCE, "past selection blocks must fit one lane tile"
    sel_past = jnp.pad(sel_mask_s[:, :, 0, :, :n_past_blk], ((0, 0),) * 3 + ((0, LANES - n_past_blk),))
    q_aug_s = jnp.concatenate([q_stack_s, jnp.tile(sel_past, (1, 1, R, 1))], axis=-1)
    key_blk_s = (jnp.arange(PAST, dtype=i32) // NSA_SEL_BLOCK).reshape(n_pages, 1, PAGE)
    onehot_s = (jnp.arange(LANES, dtype=i32)[None, :, None] == key_blk_s).astype(BF16)
    d_last = pos_s[:, None] - (PAST - PAGE + jnp.arange(PAGE, dtype=i32))[None, :]
    bk_last = jnp.concatenate([_masked_bucket(d_last, d_last >= 0), jnp.full((S_PAD, PAGE), REL_BUCKETS - 1, i32)], axis=0)
    last_far = _bias_tiles(table, bk_last, NSA_HEADS)
    bias_last_s = _group_rows(last_far[:, :S_PAD], G)
    bias_far_s = _group_rows(last_far[:, S_PAD:], G)
    d_new = s_real[:, None] - srow[None, :]
    bk_new = _masked_bucket(d_new, (d_new >= 0) & (srow[None, :] < S))
    nbias_nsa = _group_rows(_bias_tiles(table, bk_new, NSA_HEADS), G)
    sel_new8 = jnp.pad(sel_s.reshape(Bd, S, -1), ((0, 0), (0, S_PAD - S), (0, 0)))
    cache_sel_t = cache_nsa_sel.transpose(0, 2, 3, 4, 1).reshape(n_pool, feat, PAGE)
    flat = lambda a: a.reshape(-1, a.shape[-1])
    o_sel_ss = _sel_decode(page_table, q_aug_s, onehot_s, flat(bias_last_s), flat(bias_far_s), cache_sel_t, sel_new8,
                           flat(nbias_nsa), pps_m)
    o_sel_s = _unstack_heads_sample(o_sel_ss, R, S, S_PAD)

    def state_bias(wb, window, n_heads, n_groups):
        d = wb + s_real[:, None] - jnp.arange(wb, dtype=i32)[None, :]
        return _group_rows(_bias_tiles(table, _masked_bucket(d, d < window), n_heads), n_groups)

    wb0 = state_nsa_win.shape[1]
    win_new8 = jnp.pad(win_s.reshape(Bd, S, -1), ((0, 0), (0, S_PAD - S), (0, 0)))
    no_sink0_s = jnp.full((G, R * S_PAD, 1), NEG, F32)
    o_win_ss = _win_decode(q_stack_s, state_nsa_win.transpose(0, 2, 3, 4, 1).reshape(Bd, feat, wb0), win_new8,
                           state_bias(wb0, NSA_WINDOW, NSA_HEADS, G), nbias_nsa, no_sink0_s)
    o_win_s = _unstack_heads_sample(o_win_ss, R, S, S_PAD)

    eg = _gate_expand()
    w_out0 = w_out_0.astype(BF16)
    f = w_ffn_out_0.shape[0]
    ffn0 = (w_ffn_in_0[:, :f].astype(BF16), w_ffn_in_0[:, f:].astype(BF16), w_ffn_out_0.astype(BF16))
    y1_p = _out_ffn(x_prompt, [o_a_p, o_cmp_p, o_sel_p, o_win_p, gate_p], [eg, w_out0], mods0_p[2:], ffn0, None,
                    "ab_prompt", tm_p)
    y1_s = _out_ffn(xs, [o_full_s, o_cmp_s, o_sel_s, o_win_s, gate_s], [eg, w0["wuv_bd"], w_out0], mods0_s[2:], ffn0,
                    None, "ab_sample", tm_s)

    nq1 = SWA_HEADS * HEAD_DIM
    w1 = jnp.concatenate([w_in_1[:, :nq1] * ATTN_SCALE, w_in_1[:, nq1:]], axis=1).astype(BF16)
    q1_p, kv1_p, kvb1_p = _proj1(y1_p, mods1_p[0], mods1_p[1], w1, nq1, tm_p)
    q1_s, kv1_s, _ = _proj1(y1_s, mods1_s[0], mods1_s[1], w1, nq1, tm_s)
    bias_swa_p, nw1 = window_bias(SWA_WINDOW, SWA_HEADS, G1)
    sink_p = jnp.repeat(swa_sinks.astype(F32).reshape(G1, R1), tq, axis=1).reshape(G1, R1 * tq, 1)
    o_c_st = _banded(_stack_heads(q1_p, G1, R1, tq), kvb1_p, bias_swa_p, sink_p, tq, nw1)
    o_c_p = _unstack_heads(o_c_st, R1, tq)

    wb1 = state_swa.shape[1]
    kv_new8 = jnp.pad(kv1_s.reshape(Bd, S, -1), ((0, 0), (0, S_PAD - S), (0, 0)))
    d_new1 = s_real[:, None] - srow[None, :]
    nbias_swa = _group_rows(_bias_tiles(table, _masked_bucket(d_new1, (d_new1 >= 0) & (srow[None, :] < S)), SWA_HEADS),
                            G1)
    sink_s = jnp.repeat(swa_sinks.astype(F32).reshape(G1, R1), S_PAD, axis=1).reshape(G1, R1 * S_PAD, 1)
    state_swa_t = state_swa.transpose(0, 2, 3, 4, 1).reshape(Bd, G1 * 2 * HEAD_DIM, wb1)
    o_c_ss = _win_decode(_stack_heads_sample(q1_s[0], G1, R1, S, S_PAD), state_swa_t, kv_new8,
                         state_bias(wb1, SWA_WINDOW, SWA_HEADS, G1), nbias_swa, sink_s)
    o_c_s = _unstack_heads_sample(o_c_ss, R1, S, S_PAD)

    w_out1 = w_out_1.astype(BF16)
    f1 = w_ffn_out_1.shape[0]
    ffn1 = (w_ffn_in_1[:, :f1].astype(BF16), w_ffn_in_1[:, f1:].astype(BF16), w_ffn_out_1.astype(BF16))
    gain = final_norm.reshape(1, D).astype(F32)
    y_prompt = _out_ffn(y1_p, [o_c_p], [w_out1], mods1_p[2:], ffn1, gain, "c", tm_p)
    y_sample = _out_ffn(y1_s, [o_c_s], [w_out1], mods1_s[2:], ffn1, gain, "c", tm_s).reshape(Bd, S, D)

    row5 = lambda a, lead, g: a.reshape(lead + (g, 2, HEAD_DIM))
    win_p5 = row5(win_p, (B, T), G)
    kv1_p5 = row5(kv1_p, (B, T), G1)
    win_s5 = row5(win_s, (Bd, S), G)
    kv1_s5 = row5(kv1_s, (Bd, S), G1)
    return (y_prompt, y_sample, mla_p, mla_s.reshape(Bd, S, mla_w),
            row5(cmp_p, (B, T), G), row5(cmp_s, (Bd, S), G), row5(sel_p, (B, T), G), row5(sel_s, (Bd, S), G),
            win_p5[:, T - min(NSA_WINDOW, T):], jnp.concatenate([state_nsa_win, win_s5], axis=1)[:, S:],
            kv1_p5[:, T - min(SWA_WINDOW, T):], jnp.concatenate([state_swa, kv1_s5], axis=1)[:, S:])
```

```python
import functools
import math

import jax
import jax.numpy as jnp
import numpy as np
from jax import lax
from jax.experimental import pallas as pl
from jax.experimental.pallas import tpu as pltpu

F32 = jnp.float32
BF16 = jnp.bfloat16

MLA_HEADS, MLA_NOPE, MLA_ROPE, MLA_V = 8, 64, 32, 64
ROPE_THETA = 10000.0
NSA_HEADS, NSA_KV_HEADS, HEAD_DIM = 8, 2, 64
NSA_CMP_BLOCK, NSA_SEL_BLOCK, NSA_TOP_N, NSA_WINDOW = 32, 64, 16, 512
SWA_HEADS, SWA_KV_HEADS, SWA_WINDOW = 16, 4, 128
REL_BUCKETS, REL_MAX_DISTANCE = 32, 128
Q_BLOCK = 128
NORM_EPS = 1e-6
NEG = -1e30
SELECT_BIG = 1e9
MLA_SCALE = (MLA_NOPE + MLA_ROPE) ** -0.5
ATTN_SCALE = HEAD_DIM ** -0.5

LANES = 128
VMEM_LIMIT = 52 * 1024 * 1024


def _cp(*sem):
    return pltpu.CompilerParams(dimension_semantics=sem, vmem_limit_bytes=VMEM_LIMIT)


def _dot(a, b):
    return jnp.dot(a, b, preferred_element_type=F32)


def _dot_t(a, b):
    return lax.dot_general(a, b, (((1,), (1,)), ((), ())), preferred_element_type=F32)


def _rms(x):
    return x * lax.rsqrt(jnp.mean(x * x, axis=-1, keepdims=True) + NORM_EPS)


def _sigmoid(x):
    return 1.0 / (1.0 + jnp.exp(-x))


def _const_spec(shape):
    n = len(shape)
    return pl.BlockSpec(shape, lambda *_: (0,) * n)


def _softmax_update(scores, values, m_ref, l_ref, acc_ref, transposed_values=False):
    m_prev = m_ref[...]
    m_new = m_prev
    for s in scores:
        m_new = jnp.maximum(m_new, jnp.max(s, axis=-1, keepdims=True))
    alpha = jnp.exp(m_prev - m_new)
    l_new = alpha * l_ref[...]
    acc = alpha * acc_ref[...]
    for s, v in zip(scores, values):
        p = jnp.exp(s - m_new)
        l_new = l_new + jnp.sum(p, axis=-1, keepdims=True)
        pb = p.astype(BF16)
        acc = acc + (_dot_t(pb, v) if transposed_values else _dot(pb, v))
    m_ref[...] = m_new
    l_ref[...] = l_new
    acc_ref[...] = acc


def _lane_tile(x, width):
    reps = width // x.shape[-1]
    return x if reps == 1 else jnp.concatenate([x] * reps, axis=1)


def _flash_step(scores, values, m_ref, acc_ref):
    m_prev = m_ref[...]
    m_cur = None
    for s in scores:
        mx = jnp.max(s, axis=-1, keepdims=True)
        m_cur = mx if m_cur is None else jnp.maximum(m_cur, mx)
    m_new = jnp.maximum(m_prev, m_cur)
    acc = acc_ref[...] * _lane_tile(jnp.exp(m_prev - m_new), acc_ref.shape[-1])
    for s, v in zip(scores, values):
        p = jnp.exp(s - _lane_tile(m_new, s.shape[-1]))
        acc = acc + _dot(p.astype(BF16), v)
    m_ref[...] = m_new
    acc_ref[...] = acc


N_CHAINS = 1


def _merge_chains(m_ref, l_ref, acc_ref):
    m = m_ref[0]
    for k in range(1, m_ref.shape[0]):
        m = jnp.maximum(m, m_ref[k])
    l = acc = None
    for k in range(m_ref.shape[0]):
        w = jnp.exp(m_ref[k] - m)
        l = w * l_ref[k] if l is None else l + w * l_ref[k]
        acc = w * acc_ref[k] if acc is None else acc + w * acc_ref[k]
    return l, acc


def _softmax_init(m_ref, l_ref, acc_ref):
    m_ref[...] = jnp.full(m_ref.shape, NEG, F32)
    l_ref[...] = jnp.zeros(l_ref.shape, F32)
    acc_ref[...] = jnp.zeros(acc_ref.shape, F32)


def _ada_kernel(c_ref, w_ref, b_ref, o_ref):
    c = c_ref[...]
    a = (c * _sigmoid(c)).astype(BF16)
    o_ref[...] = _dot(a, w_ref[...]) + b_ref[...]


def _ada(c, w, b):
    M, D = c.shape
    N = w.shape[1]
    tn = 1024 if N % 1024 == 0 else N
    return pl.pallas_call(
        _ada_kernel, grid=(N // tn,),
        in_specs=[pl.BlockSpec((M, D), lambda j: (0, 0)), pl.BlockSpec((D, tn), lambda j: (0, j)),
                  pl.BlockSpec((1, tn), lambda j: (0, j))],
        out_specs=pl.BlockSpec((M, tn), lambda j: (0, j)),
        out_shape=jax.ShapeDtypeStruct((M, N), F32), compiler_params=_cp("parallel"), name="ada")(c, w, b)


def _bias_kernel(tab_ref, bkt_ref, o_ref):
    h = pl.program_id(0)
    bkt = bkt_ref[...]
    acc = jnp.full(bkt.shape, NEG, F32)
    for b in range(REL_BUCKETS):
        acc = jnp.where(bkt == b, tab_ref[b, h], acc)
    o_ref[0] = acc


def _bias_tiles(table, buckets, n_heads):
    R, C = buckets.shape
    tr = R
    for cand in (512, 256, 128):
        if R > cand and R % cand == 0:
            tr = cand
            break
    return pl.pallas_call(
        _bias_kernel,
        grid_spec=pltpu.PrefetchScalarGridSpec(
            num_scalar_prefetch=1, grid=(n_heads, R // tr),
            in_specs=[pl.BlockSpec((tr, C), lambda h, r, tab: (r, 0))],
            out_specs=pl.BlockSpec((1, tr, C), lambda h, r, tab: (h, r, 0))),
        out_shape=jax.ShapeDtypeStruct((n_heads, R, C), F32),
        compiler_params=_cp("parallel", "parallel"), name="rel_bias")(table, buckets)


def _bucket_thresholds():
    exact = REL_BUCKETS // 2
    n = np.arange(exact, REL_MAX_DISTANCE + 1)
    scaled = np.log(n / exact) / math.log(REL_MAX_DISTANCE / exact)
    large = np.minimum(exact + np.trunc(scaled * (REL_BUCKETS - exact)).astype(np.int64), REL_BUCKETS - 1)
    return [int(n[np.argmax(large >= b)]) for b in range(exact + 1, REL_BUCKETS)]


def _t5_bucket(dist):
    n = jnp.maximum(dist, 0)
    exact = REL_BUCKETS // 2
    large = jnp.full(n.shape, exact, jnp.int32)
    for thr in _bucket_thresholds():
        large = large + jnp.where(n >= thr, 1, 0)
    return jnp.where(n < exact, n, large)


def _masked_bucket(dist, valid):
    return jnp.where(valid, _t5_bucket(dist), -1).astype(jnp.int32)


def _modulate(x, shift, scale):
    return _rms(x) * (1.0 + scale) + shift


def _store_stacked_heads(q, ref):
    _, G, n_tiles, M, _ = ref.shape
    R = M * n_tiles // q.shape[0]
    tq = M // R
    lane = lax.broadcasted_iota(jnp.int32, (q.shape[0], LANES), 1)
    for h in range(G * R):
        pair = q[:, LANES * (h // 2):LANES * (h // 2 + 1)]
        if h % 2:
            pair = pltpu.roll(pair, HEAD_DIM, 1)
        head = jnp.where(lane < HEAD_DIM, pair, 0.0).astype(ref.dtype)
        g, r = divmod(h, R)
        for t in range(n_tiles):
            ref[0, g, t, r * tq:(r + 1) * tq, :] = head[t * tq:(t + 1) * tq]


def _proj0_kernel(x_ref, sh_ref, sc_ref, w0_ref, qn_ref, kvn_ref, wuq_ref, wuqs_ref, cq_ref, sq_ref, cm_ref,
                  sm_ref, wa_ref, wb_ref, vones_ref, *outs, sample):
    h = _modulate(x_ref[0], sh_ref[0], sc_ref[0]).astype(BF16)
    y = _dot(h, w0_ref[...])
    qn = (_rms(y[:, 0:256]) * qn_ref[...]).astype(BF16)
    ckv = _rms(y[:, 256:512]) * kvn_ref[...]
    misc = y[:, 512:640] * cm_ref[...] + y[:, 640:768] * sm_ref[...]
    cq = jnp.concatenate([cq_ref[...]] * MLA_HEADS, axis=1)
    sq = jnp.concatenate([sq_ref[...]] * MLA_HEADS, axis=1)
    q_rot = ((_dot(qn, wuq_ref[...]) * cq + _dot(qn, wuqs_ref[...]) * sq) * MLA_SCALE).astype(BF16)
    ckv_b = ckv.astype(BF16)
    if sample:
        rows_ref, qcat_ref, qnsa_ref, cmp_ref, sel_ref, win_ref, gate_ref = outs
        qcat_ref[0] = _dot(q_rot, wa_ref[...]).astype(BF16)
    else:
        rows_ref, qmla_ref, kcat_ref, vmla_ref, qnsa_ref, cmp_ref, sel_ref, win_ref, selb_ref, winb_ref, gate_ref = outs
        qmla_ref[0] = q_rot
        kcat_ref[0] = _dot(jnp.concatenate([ckv_b, misc.astype(BF16)], axis=1), wa_ref[...]).astype(BF16)
        vmla_ref[0] = (_dot(ckv_b, wb_ref[...]) + vones_ref[...]).astype(BF16)
        selb_ref[0] = y[:, 1536:1792].astype(BF16)
        winb_ref[0] = y[:, 1792:2048].astype(BF16)
    rows_ref[0, :, 0:256] = ckv
    rows_ref[0, :, 256:288] = misc[:, 0:MLA_ROPE]
    if sample:
        qnsa_ref[0] = y[:, 768:1280].astype(BF16)
    else:
        _store_stacked_heads(y[:, 768:1280], qnsa_ref)
    cmp_ref[0] = y[:, 1280:1536]
    sel_ref[0] = y[:, 1536:1792]
    win_ref[0] = y[:, 1792:2048]
    gate_ref[0] = _sigmoid(misc)


def _proj0(x, shift, scale, w, tabs, sample, tm):
    Bx, Tx, D = x.shape
    per_token = shift.shape[1] != 1
    mod_spec = (pl.BlockSpec((1, tm, D), lambda b, i: (b, i, 0)) if per_token
                else pl.BlockSpec((1, 1, D), lambda b, i: (b, 0, 0)))
    tok = lambda n: pl.BlockSpec((1, tm, n), lambda b, i: (b, i, 0))
    tab_spec = pl.BlockSpec((tm, LANES), lambda b, i: (i, 0))
    wa, wb = (w["a_abs"], w["wuv"]) if sample else (w["kcat"], w["wuv"])
    in_specs = [tok(D), mod_spec, mod_spec, _const_spec(w["w0"].shape), _const_spec((1, 256)), _const_spec((1, 256)),
                _const_spec(w["wuq"].shape), _const_spec(w["wuqs"].shape), tab_spec, tab_spec, tab_spec, tab_spec,
                _const_spec(wa.shape), _const_spec(wb.shape), _const_spec(w["vones"].shape)]
    sd = lambda n, dt: jax.ShapeDtypeStruct((Bx, Tx, n), dt)
    if sample:
        out_shape = [sd(288, F32), sd(MLA_HEADS * 288, BF16), sd(512, BF16), sd(256, F32), sd(256, F32), sd(256, F32),
                     sd(LANES, F32)]
    else:
        out_shape = [sd(288, F32), sd(1024, BF16), sd(1024, BF16), sd(1024, BF16), sd(512, BF16), sd(256, F32),
                     sd(256, F32), sd(256, F32), sd(256, BF16), sd(256, BF16), sd(LANES, F32)]
    out_specs = [tok(s.shape[-1]) for s in out_shape]
    if not sample:
        G, R, nt = NSA_KV_HEADS, NSA_HEADS // NSA_KV_HEADS, tm // Q_BLOCK
        out_shape[4] = jax.ShapeDtypeStruct((Bx, G, Tx // Q_BLOCK, R * Q_BLOCK, LANES), BF16)
        out_specs[4] = pl.BlockSpec((1, G, nt, R * Q_BLOCK, LANES), lambda b, i: (b, 0, i, 0, 0))
    return pl.pallas_call(
        functools.partial(_proj0_kernel, sample=sample), grid=(Bx, Tx // tm), in_specs=in_specs, out_specs=out_specs,
        out_shape=out_shape, compiler_params=_cp("parallel", "parallel"),
        name="proj0_sample" if sample else "proj0_prompt")(
            x, shift, scale, w["w0"], w["qn"], w["kvn"], w["wuq"], w["wuqs"], *tabs, wa, wb, w["vones"])


def _proj1_kernel(x_ref, sh_ref, sc_ref, w_ref, q_ref, kv_ref, kvb_ref, *, nq, stacked):
    h = _modulate(x_ref[0], sh_ref[0], sc_ref[0]).astype(BF16)
    y = _dot(h, w_ref[...])
    if stacked:
        _store_stacked_heads(y[:, :nq], q_ref)
    else:
        q_ref[0] = y[:, :nq].astype(BF16)
    kv_ref[0] = y[:, nq:]
    kvb_ref[0] = y[:, nq:].astype(BF16)


def _proj1(x, shift, scale, w1, nq, tm, stacked):
    Bx, Tx, D = x.shape
    nkv = w1.shape[1] - nq
    per_token = shift.shape[1] != 1
    mod_spec = (pl.BlockSpec((1, tm, D), lambda b, i: (b, i, 0)) if per_token
                else pl.BlockSpec((1, 1, D), lambda b, i: (b, 0, 0)))
    tok = lambda n: pl.BlockSpec((1, tm, n), lambda b, i: (b, i, 0))
    q_shape, q_spec = jax.ShapeDtypeStruct((Bx, Tx, nq), BF16), tok(nq)
    if stacked:
        G, R, nt = SWA_KV_HEADS, SWA_HEADS // SWA_KV_HEADS, tm // Q_BLOCK
        q_shape = jax.ShapeDtypeStruct((Bx, G, Tx // Q_BLOCK, R * Q_BLOCK, LANES), BF16)
        q_spec = pl.BlockSpec((1, G, nt, R * Q_BLOCK, LANES), lambda b, i: (b, 0, i, 0, 0))
    return pl.pallas_call(
        functools.partial(_proj1_kernel, nq=nq, stacked=stacked), grid=(Bx, Tx // tm),
        in_specs=[tok(D), mod_spec, mod_spec, _const_spec(w1.shape)], out_specs=[q_spec, tok(nkv), tok(nkv)],
        out_shape=[q_shape, jax.ShapeDtypeStruct((Bx, Tx, nkv), F32), jax.ShapeDtypeStruct((Bx, Tx, nkv), BF16)],
        compiler_params=_cp("parallel", "parallel"), name="proj1")(x, shift, scale, w1)


def _mla_prompt_kernel(q_ref, k_ref, v_ref, o_ref, m_sc, acc_sc, *, tq):
    i = pl.program_id(2)
    q = q_ref[0]
    causal = lax.broadcasted_iota(jnp.int32, (tq, tq), 1) <= lax.broadcasted_iota(jnp.int32, (tq, tq), 0)
    m_sc[...] = jnp.full(m_sc.shape, NEG, F32)
    acc_sc[...] = jnp.zeros(acc_sc.shape, F32)

    def step(j, masked):
        start = pl.multiple_of(j * tq, tq)
        kk = k_ref[0, pl.ds(start, tq), :]
        vv = v_ref[0, pl.ds(start, tq), :]
        for e in range(2):
            sl = slice(LANES * e, LANES * (e + 1))
            s = _dot_t(q[:, sl], kk[:, sl])
            if masked:
                s = jnp.where(causal, s, NEG)
            _flash_step([s], [vv[:, sl]], m_sc.at[e], acc_sc.at[e])

    def body(j, carry):
        step(j, False)
        return carry

    lax.fori_loop(0, i, body, 0)
    step(i, True)
    a0, a1 = acc_sc[0], acc_sc[1]
    o0 = a0 / pltpu.roll(a0, MLA_V, 1)
    o1 = a1 / pltpu.roll(a1, MLA_V, 1)
    lane = lax.broadcasted_iota(jnp.int32, (tq, LANES), 1)
    o_ref[0] = jnp.where(lane < MLA_V, o0, o1).astype(o_ref.dtype)


def _mla_prompt(q, k, v, tq):
    B, T, _ = q.shape
    return pl.pallas_call(
        functools.partial(_mla_prompt_kernel, tq=tq), grid=(B, MLA_HEADS // 2, T // tq),
        in_specs=[pl.BlockSpec((1, tq, 2 * LANES), lambda b, h, i: (b, i, h)),
                  pl.BlockSpec((1, T, 2 * LANES), lambda b, h, i: (b, 0, h)),
                  pl.BlockSpec((1, T, 2 * LANES), lambda b, h, i: (b, 0, h))],
        out_specs=pl.BlockSpec((1, tq, LANES), lambda b, h, i: (b, i, h)),
        out_shape=jax.ShapeDtypeStruct((B, T, MLA_HEADS * MLA_V), BF16),
        scratch_shapes=[pltpu.VMEM((2, tq, LANES), F32), pltpu.VMEM((2, tq, LANES), F32)],
        compiler_params=_cp("parallel", "parallel", "parallel"), name="mla_prompt")(q, k, v)


def _pe_bias_kernel(pe_ref, w_ref, o_ref):
    o_ref[...] = _dot(pe_ref[...].astype(BF16), w_ref[...])


def _pe_bias(pe_rows, wbig):
    return pl.pallas_call(_pe_bias_kernel, out_shape=jax.ShapeDtypeStruct((pe_rows.shape[0], wbig.shape[1]), F32),
                          compiler_params=_cp(), name="pe_bias")(pe_rows, wbig)


def _compress_prompt_kernel(x_ref, w_ref, peb_ref, o_ref):
    o_ref[0] = (_dot(x_ref[0].astype(BF16), w_ref[...]) + peb_ref[0:1, :]).astype(o_ref.dtype)


def _compress_prompt(x, wbig, pe_bias):
    B, NC, K = x.shape
    return pl.pallas_call(
        _compress_prompt_kernel, grid=(B,),
        in_specs=[pl.BlockSpec((1, NC, K), lambda b: (b, 0, 0)), _const_spec(wbig.shape), _const_spec(pe_bias.shape)],
        out_specs=pl.BlockSpec((1, NC, 256), lambda b: (b, 0, 0)),
        out_shape=jax.ShapeDtypeStruct((B, NC, 256), BF16), compiler_params=_cp("parallel"),
        name="compress_prompt")(x, wbig, pe_bias)


def _compress_paged_kernel(pt_ref, *refs, pps):
    page_refs = refs[:pps]
    perm_ref, w_ref, peb_ref, o_ref = refs[pps:]
    pairs = []
    for pr in range(pps // 2):
        both = jnp.concatenate([page_refs[2 * pr][0], page_refs[2 * pr + 1][0]], axis=1).astype(BF16)
        pairs.append(_dot_t(perm_ref[...], both))
    n_blk = o_ref.shape[1]
    acc = None
    for l in range(NSA_CMP_BLOCK):
        x = jnp.concatenate([p[8 * l:8 * (l + 1)] for p in pairs], axis=0)
        x = jnp.concatenate([x[:, :LANES], x[:, LANES:]], axis=0).astype(BF16)
        part = _dot(x, w_ref[l])
        acc = part if acc is None else acc + part
    y = jnp.concatenate([acc[:n_blk], acc[n_blk:]], axis=1)
    o_ref[0] = (y + peb_ref[0:1, :]).astype(o_ref.dtype)


def _compress_paged(page_table, cache_t, w_head, pe_bias, pps):
    Bd, n_pages = page_table.shape
    _, feat, page = cache_t.shape
    per_page = page // NSA_CMP_BLOCK
    assert per_page * 2 == 8 and feat == 2 * LANES, "a pair of pages must hold one sublane tile of blocks"
    n_blk = pps * per_page
    r = jnp.arange(2 * page)
    l, p2, n = r // 8, (r // per_page) % 2, r % per_page
    perm = (jnp.arange(2 * page)[None, :] == (p2 * page + n * NSA_CMP_BLOCK + l)[:, None]).astype(BF16)
    page_specs = [pl.BlockSpec((1, feat, page), lambda b, c, pt, k=k: (pt[b, c * pps + k], 0, 0)) for k in range(pps)]
    cst = lambda a: pl.BlockSpec(a.shape, lambda b, c, pt: (0,) * a.ndim)
    return pl.pallas_call(
        functools.partial(_compress_paged_kernel, pps=pps),
        grid_spec=pltpu.PrefetchScalarGridSpec(
            num_scalar_prefetch=1, grid=(Bd, n_pages // pps),
            in_specs=page_specs + [cst(perm), cst(w_head), cst(pe_bias)],
            out_specs=pl.BlockSpec((1, n_blk, feat), lambda b, c, pt: (b, c, 0))),
        out_shape=jax.ShapeDtypeStruct((Bd, n_pages * per_page, feat), BF16),
        compiler_params=_cp("parallel", "parallel"), name="compress_paged")(
            page_table, *([cache_t] * pps), perm, w_head, pe_bias)


def _cmp_select_kernel(q_ref, kv_ref, bias_ref, code_ref, o_ref, sel_ref, *, tq, n_rank, first_pos):
    i = pl.program_id(1)
    G, R = NSA_KV_HEADS, NSA_HEADS // NSA_KV_HEADS
    M = R * tq
    NC = kv_ref.shape[1]
    n_lanes = code_ref.shape[-1]
    code = code_ref[0]
    row_pos = first_pos + i * tq + lax.broadcasted_iota(jnp.int32, (M, 1), 0) % tq
    any_valid = (row_pos >= NSA_CMP_BLOCK - 1).astype(F32)
    lane = lax.broadcasted_iota(jnp.int32, (tq, n_lanes), 1)
    for g in range(G):
        kv = kv_ref[0, :, LANES * g:LANES * (g + 1)]
        s = _dot_t(q_ref[0, g, 0], kv) + bias_ref[0, g]
        m = jnp.max(s, axis=-1, keepdims=True)
        p = jnp.exp(s - m)
        p = p * (any_valid / jnp.sum(p, axis=-1, keepdims=True))
        o_ref[0, g, 0] = _dot(p.astype(BF16), kv).astype(o_ref.dtype)
        imp = p[0:tq]
        for r in range(1, R):
            imp = imp + p[r * tq:(r + 1) * tq]
        imp = imp[:, :NC // 2] + imp[:, NC // 2:]
        if n_lanes > NC // 2:
            imp = jnp.concatenate([imp, jnp.zeros((tq, n_lanes - NC // 2), F32)], axis=1)
        score = jnp.where(code == 0.0, imp, code)
        rank = jnp.zeros((tq, n_lanes), F32)
        for jp in range(n_rank):
            col = score[:, jp:jp + 1]
            ahead = jnp.where(col > score, 1.0, jnp.where(col == score, jnp.where(lane > jp, 1.0, 0.0), 0.0))
            rank = rank + ahead
        sel_neg = jnp.where(rank < float(NSA_TOP_N), jnp.where(code > -0.5 * SELECT_BIG, 0.0, NEG), NEG)
        sel_ref[0, g, 0] = sel_neg.astype(sel_ref.dtype)


def _cmp_select(q_stack, kvc, bias, code, tq, n_rank, first_pos):
    B, G, nq, M, _ = q_stack.shape
    NC = kvc.shape[1]
    n_lanes = code.shape[-1]
    return pl.pallas_call(
        functools.partial(_cmp_select_kernel, tq=tq, n_rank=n_rank, first_pos=first_pos), grid=(B, nq),
        in_specs=[pl.BlockSpec((1, G, 1, M, LANES), lambda b, i: (b, 0, i, 0, 0)),
                  pl.BlockSpec((1, NC, G * LANES), lambda b, i: (b, 0, 0)),
                  pl.BlockSpec((1, G, M, NC), lambda b, i: (i, 0, 0, 0)),
                  pl.BlockSpec((1, tq, n_lanes), lambda b, i: (i, 0, 0))],
        out_specs=[pl.BlockSpec((1, G, 1, M, LANES), lambda b, i: (b, 0, i, 0, 0)),
                   pl.BlockSpec((1, G, 1, tq, n_lanes), lambda b, i: (b, 0, i, 0, 0))],
        out_shape=[jax.ShapeDtypeStruct((B, G, nq, M, LANES), BF16),
                   jax.ShapeDtypeStruct((B, G, nq, tq, n_lanes), BF16)],
        compiler_params=_cp("parallel", "parallel"), name="cmp_select")(q_stack, kvc, bias, code)


def _sel_prompt_kernel(q_ref, sel_ref, kv_ref, oh_ref, bnear_ref, bfar_ref, o_ref, m_sc, acc_sc, *, tq, tk):
    i = pl.program_id(1)
    G = NSA_KV_HEADS
    M = q_ref.shape[3]
    R = M // tq
    far_end = jnp.maximum(i - 1, 0) * tq
    n_full = far_end // tk
    rem = far_end - n_full * tk
    ones = jnp.ones((tk, LANES), BF16)
    m_sc[...] = jnp.full(m_sc.shape, NEG, F32)
    acc_sc[...] = jnp.zeros(acc_sc.shape, F32)

    def scores(start, size):
        kv_all = kv_ref[0, pl.ds(start, size), :]
        onehot = oh_ref[pl.ds(start, size), :]
        out = []
        for g in range(G):
            q = jnp.concatenate([q_ref[0, g, 0], jnp.concatenate([sel_ref[0, g, 0]] * R, axis=0)], axis=1)
            keys = kv_all[:, LANES * g:LANES * (g + 1)]
            out.append((_dot_t(q, jnp.concatenate([keys, onehot], axis=1)), jnp.concatenate([keys, ones[:size]], axis=1)))
        return out

    def body(j, carry):
        for g, (s, v) in enumerate(scores(pl.multiple_of(j * tk, tk), tk)):
            _flash_step([s], [v], m_sc.at[g], acc_sc.at[g])
        return carry

    lax.fori_loop(0, n_full, body, 0)

    @pl.when(rem > 0)
    def _():
        keep = lax.broadcasted_iota(jnp.int32, (M, tk), 1) < rem
        for g, (s, v) in enumerate(scores(pl.multiple_of(n_full * tk, tk), tk)):
            _flash_step([jnp.where(keep, s, NEG)], [v], m_sc.at[g], acc_sc.at[g])

    prev = scores(pl.multiple_of(jnp.maximum(i - 1, 0) * tq, tq), tq)
    diag = scores(pl.multiple_of(i * tq, tq), tq)
    edge = jnp.where(i >= 1, 0.0, NEG)
    for g in range(G):
        far = bfar_ref[g]
        s0 = prev[g][0] + (bnear_ref[g, :, 0:tq] - far) + edge
        s1 = diag[g][0] + (bnear_ref[g, :, tq:2 * tq] - far)
        _flash_step([s0, s1], [prev[g][1], diag[g][1]], m_sc.at[g], acc_sc.at[g])
        acc = acc_sc[g]
        o_ref[0, g, 0] = (acc[:, :LANES] / acc[:, LANES:]).astype(o_ref.dtype)


def _sel_prompt(q_stack, sel_rows, kv, onehot, bias_near, bias_far, tq, tk):
    B, G, nq, M, _ = q_stack.shape
    T = kv.shape[1]
    return pl.pallas_call(
        functools.partial(_sel_prompt_kernel, tq=tq, tk=tk), grid=(B, nq),
        in_specs=[pl.BlockSpec((1, G, 1, M, LANES), lambda b, i: (b, 0, i, 0, 0)),
                  pl.BlockSpec((1, G, 1, tq, LANES), lambda b, i: (b, 0, i, 0, 0)),
                  pl.BlockSpec((1, T, G * LANES), lambda b, i: (b, 0, 0)),
                  _const_spec(onehot.shape), _const_spec(bias_near.shape), _const_spec(bias_far.shape)],
        out_specs=pl.BlockSpec((1, G, 1, M, LANES), lambda b, i: (b, 0, i, 0, 0)),
        out_shape=jax.ShapeDtypeStruct((B, G, nq, M, LANES), BF16),
        scratch_shapes=[pltpu.VMEM((G, M, LANES), F32), pltpu.VMEM((G, M, 2 * LANES), F32)],
        compiler_params=_cp("parallel", "parallel"), name="sel_prompt")(q_stack, sel_rows, kv, onehot, bias_near, bias_far)


def _cmp_select_t_kernel(q_ref, kv_ref, kvt_ref, bias_ref, code_ref, eye_ref, o_ref, sel_ref, *, tq):
    i = pl.program_id(1)
    G, R = NSA_KV_HEADS, NSA_HEADS // NSA_KV_HEADS
    M = R * tq
    NC = kv_ref.shape[1]
    n_sel = code_ref.shape[1]
    code = code_ref[0]
    col_pos = i * tq + lax.broadcasted_iota(jnp.int32, (1, M), 1) % tq
    any_valid = (col_pos >= NSA_CMP_BLOCK - 1).astype(F32)
    blk = lax.broadcasted_iota(jnp.int32, (n_sel, tq), 0)
    for g in range(G):
        kv = kv_ref[0, :, LANES * g:LANES * (g + 1)]
        s = _dot_t(kv, q_ref[0, g, 0]) + bias_ref[0, g]
        p = jnp.exp(s - jnp.max(s, axis=0, keepdims=True))
        p = p * (any_valid / jnp.sum(p, axis=0, keepdims=True))
        o_ref[0, g, 0] = _dot(kvt_ref[0, LANES * g:LANES * (g + 1), :], p.astype(BF16)).astype(o_ref.dtype)
        imp = p[:, 0:tq]
        for r in range(1, R):
            imp = imp + p[:, r * tq:(r + 1) * tq]
        imp = imp[:NC // 2] + imp[NC // 2:]
        score = jnp.where(code == 0.0, imp, code)
        rank = jnp.zeros((n_sel, tq), F32)
        for jp in range(n_sel):
            row = score[jp:jp + 1, :]
            rank = rank + jnp.where(row > score, 1.0, jnp.where(row == score, jnp.where(blk > jp, 1.0, 0.0), 0.0))
        sel_neg = jnp.where(rank < float(NSA_TOP_N), jnp.where(code > -0.5 * SELECT_BIG, 0.0, NEG), NEG)
        padded = jnp.concatenate([sel_neg.astype(BF16), jnp.zeros((LANES - n_sel, tq), BF16)], axis=0)
        sel_ref[0, g, 0] = _dot_t(eye_ref[...], padded).astype(sel_ref.dtype)


def _cmp_select_t(q_stack, kvc, kvc_t, bias_t, code_t, tq):
    B, G, nq, M, _ = q_stack.shape
    NC = kvc.shape[1]
    n_sel = code_t.shape[1]
    eye = jnp.eye(tq, dtype=BF16)
    return pl.pallas_call(
        functools.partial(_cmp_select_t_kernel, tq=tq), grid=(B, nq),
        in_specs=[pl.BlockSpec((1, G, 1, M, LANES), lambda b, i: (b, 0, i, 0, 0)),
                  pl.BlockSpec((1, NC, G * LANES), lambda b, i: (b, 0, 0)),
                  pl.BlockSpec((1, G * LANES, NC), lambda b, i: (b, 0, 0)),
                  pl.BlockSpec((1, G, NC, M), lambda b, i: (i, 0, 0, 0)),
                  pl.BlockSpec((1, n_sel, tq), lambda b, i: (i, 0, 0)), _const_spec(eye.shape)],
        out_specs=[pl.BlockSpec((1, G, 1, LANES, M), lambda b, i: (b, 0, i, 0, 0)),
                   pl.BlockSpec((1, G, 1, tq, LANES), lambda b, i: (b, 0, i, 0, 0))],
        out_shape=[jax.ShapeDtypeStruct((B, G, nq, LANES, M), BF16),
                   jax.ShapeDtypeStruct((B, G, nq, tq, LANES), BF16)],
        compiler_params=_cp("parallel", "parallel"), name="cmp_select_t")(q_stack, kvc, kvc_t, bias_t, code_t, eye)


def _banded_kernel(q_ref, kv_ref, bias_ref, sink_ref, o_ref, *, tq, nw, G):
    i = pl.program_id(1)
    ones = jnp.ones((tq, LANES), BF16)
    for g in range(G):
        q = q_ref[0, g, 0]
        scores, vals = [], []
        for jj in range(nw + 1):
            kb = i - nw + jj
            start = pl.multiple_of(jnp.maximum(kb, 0) * tq, tq)
            kv = kv_ref[0, pl.ds(start, tq), LANES * g:LANES * (g + 1)]
            edge = jnp.where(kb >= 0, 0.0, NEG)
            scores.append(_dot_t(q, kv) + bias_ref[g, :, tq * jj:tq * (jj + 1)] + edge)
            vals.append(kv)
        sink = sink_ref[g]
        m = sink
        for s in scores:
            m = jnp.maximum(m, jnp.max(s, axis=-1, keepdims=True))
        acc = None
        for s, v in zip(scores, vals):
            pv = _dot(jnp.exp(s - m).astype(BF16), jnp.concatenate([v, ones], axis=1))
            acc = pv if acc is None else acc + pv
        o_ref[0, g, 0] = (acc[:, :LANES] / (acc[:, LANES:] + jnp.exp(sink - m))).astype(o_ref.dtype)


def _banded(q_stack, kv, bias, sinks, tq, nw):
    B, G, nq, M, _ = q_stack.shape
    T = kv.shape[1]
    return pl.pallas_call(
        functools.partial(_banded_kernel, tq=tq, nw=nw, G=G), grid=(B, nq),
        in_specs=[pl.BlockSpec((1, G, 1, M, LANES), lambda b, i: (b, 0, i, 0, 0)),
                  pl.BlockSpec((1, T, G * LANES), lambda b, i: (b, 0, 0)),
                  _const_spec(bias.shape), _const_spec(sinks.shape)],
        out_specs=pl.BlockSpec((1, G, 1, M, LANES), lambda b, i: (b, 0, i, 0, 0)),
        out_shape=jax.ShapeDtypeStruct((B, G, nq, M, LANES), BF16),
        compiler_params=_cp("parallel", "parallel"), name="banded")(q_stack, kv, bias, sinks)


def _mla_decode_kernel(pt_ref, q_ref, *refs, pps):
    page_refs = refs[:pps]
    new_ref, nmask_ref, o_ref, m_sc, l_sc, acc_sc = refs[pps:]
    c = pl.program_id(1)

    @pl.when(c == 0)
    def _():
        _softmax_init(m_sc, l_sc, acc_sc)

    q = q_ref[0]
    nv = acc_sc.shape[-1]
    for half in range(N_CHAINS):
        rows = [r[0].astype(BF16) for r in page_refs[half::N_CHAINS]]
        _softmax_update([_dot(q, r) for r in rows], [r[:nv] for r in rows], m_sc.at[half], l_sc.at[half],
                        acc_sc.at[half], transposed_values=True)

    @pl.when(c == pl.num_programs(1) - 1)
    def _():
        nr = new_ref[0].astype(BF16)
        _softmax_update([_dot_t(q, nr) + nmask_ref[...]], [nr[:, :nv]], m_sc.at[0], l_sc.at[0], acc_sc.at[0])
        l, acc = _merge_chains(m_sc, l_sc, acc_sc)
        o_ref[0] = (acc / l).astype(o_ref.dtype)


def _mla_decode(page_table, q_cat, cache_t, new_rows, new_mask, pps):
    Bd, n_pages = page_table.shape
    _, width, page = cache_t.shape
    M = q_cat.shape[1]
    nv = width - MLA_ROPE
    page_specs = [pl.BlockSpec((1, width, page), lambda b, c, pt, k=k: (pt[b, c * pps + k], 0, 0)) for k in range(pps)]
    return pl.pallas_call(
        functools.partial(_mla_decode_kernel, pps=pps),
        grid_spec=pltpu.PrefetchScalarGridSpec(
            num_scalar_prefetch=1, grid=(Bd, n_pages // pps),
            in_specs=[pl.BlockSpec((1, M, width), lambda b, c, pt: (b, 0, 0))] + page_specs + [
                pl.BlockSpec((1,) + new_rows.shape[1:], lambda b, c, pt: (b, 0, 0)),
                pl.BlockSpec(new_mask.shape, lambda b, c, pt: (0, 0))],
            out_specs=pl.BlockSpec((1, M, nv), lambda b, c, pt: (b, 0, 0)),
            scratch_shapes=[pltpu.VMEM((N_CHAINS, M, 1), F32), pltpu.VMEM((N_CHAINS, M, 1), F32),
                            pltpu.VMEM((N_CHAINS, M, nv), F32)]),
        out_shape=jax.ShapeDtypeStruct((Bd, M, nv), BF16),
        compiler_params=_cp("parallel", "arbitrary"), name="mla_decode")(
            page_table, q_cat, *([cache_t] * pps), new_rows, new_mask)


def _sel_decode_kernel(pt_ref, q_ref, oh_ref, blast_ref, bfar_ref, *refs, pps):
    page_refs = refs[:pps]
    new_ref, nbias_ref, o_ref, m_sc, l_sc, acc_sc = refs[pps:]
    c = pl.program_id(1)
    last = c == pl.num_programs(1) - 1
    G = NSA_KV_HEADS

    @pl.when(c == 0)
    def _():
        _softmax_init(m_sc, l_sc, acc_sc)

    M = q_ref.shape[2]
    rows = lambda g: slice(M * g, M * (g + 1))
    for half in range(N_CHAINS):
        scores, vals = [], []
        for k in range(half, pps, N_CHAINS):
            kv_t = [page_refs[k][0, LANES * g:LANES * (g + 1), :].astype(BF16) for g in range(G)]
            onehot = oh_ref[c * pps + k]
            s = jnp.concatenate([_dot(q_ref[0, g], jnp.concatenate([kv_t[g], onehot], axis=0)) for g in range(G)],
                                axis=0)
            if k == pps - 1:
                s = s + jnp.where(last, blast_ref[...] - bfar_ref[...], 0.0)
            scores.append(s)
            vals.append(kv_t)
        m_prev = m_sc[half]
        m_new = m_prev
        for s in scores:
            m_new = jnp.maximum(m_new, jnp.max(s, axis=-1, keepdims=True))
        alpha = jnp.exp(m_prev - m_new)
        l_new = alpha * l_sc[half]
        acc = alpha * acc_sc[half]
        for s, kv_t in zip(scores, vals):
            p = jnp.exp(s - m_new)
            l_new = l_new + jnp.sum(p, axis=-1, keepdims=True)
            pb = p.astype(BF16)
            acc = acc + jnp.concatenate([_dot_t(pb[rows(g)], kv_t[g]) for g in range(G)], axis=0)
        m_sc[half] = m_new
        l_sc[half] = l_new
        acc_sc[half] = acc

    @pl.when(last)
    def _():
        kv = [new_ref[0, :, LANES * g:LANES * (g + 1)].astype(BF16) for g in range(G)]
        n_new = kv[0].shape[0]
        s = jnp.concatenate([_dot_t(q_ref[0, g][:, :LANES], kv[g]) for g in range(G)], axis=0)
        s = s + (nbias_ref[...] - bfar_ref[:, :n_new])
        m_fin = jnp.maximum(m_sc[0], jnp.max(s, axis=-1, keepdims=True))
        a_fin = jnp.exp(m_sc[0] - m_fin)
        p = jnp.exp(s - m_fin)
        pb = p.astype(BF16)
        l_sc[0] = a_fin * l_sc[0] + jnp.sum(p, axis=-1, keepdims=True)
        acc_sc[0] = a_fin * acc_sc[0] + jnp.concatenate([_dot(pb[rows(g)], kv[g]) for g in range(G)], axis=0)
        m_sc[0] = m_fin
        l, acc = _merge_chains(m_sc, l_sc, acc_sc)
        out = acc / l
        for g in range(G):
            o_ref[0, g] = out[rows(g)].astype(o_ref.dtype)


def _sel_decode(page_table, q_aug, onehot_t, bias_last, bias_far, cache_t, new_rows, new_bias, pps):
    Bd, n_pages = page_table.shape
    _, feat, page = cache_t.shape
    _, G, M, _ = q_aug.shape
    page_specs = [pl.BlockSpec((1, feat, page), lambda b, c, pt, k=k: (pt[b, c * pps + k], 0, 0)) for k in range(pps)]
    cst = lambda a: pl.BlockSpec(a.shape, lambda b, c, pt: (0,) * a.ndim)
    per_b = lambda a: pl.BlockSpec((1,) + a.shape[1:], lambda b, c, pt: (b,) + (0,) * (a.ndim - 1))
    return pl.pallas_call(
        functools.partial(_sel_decode_kernel, pps=pps),
        grid_spec=pltpu.PrefetchScalarGridSpec(
            num_scalar_prefetch=1, grid=(Bd, n_pages // pps),
            in_specs=[per_b(q_aug), cst(onehot_t), cst(bias_last), cst(bias_far)] + page_specs
            + [per_b(new_rows), cst(new_bias)],
            out_specs=pl.BlockSpec((1, G, M, LANES), lambda b, c, pt: (b, 0, 0, 0)),
            scratch_shapes=[pltpu.VMEM((N_CHAINS, G * M, 1), F32), pltpu.VMEM((N_CHAINS, G * M, 1), F32),
                            pltpu.VMEM((N_CHAINS, G * M, LANES), F32)]),
        out_shape=jax.ShapeDtypeStruct((Bd, G, M, LANES), BF16),
        compiler_params=_cp("parallel", "arbitrary"), name="sel_decode")(
            page_table, q_aug, onehot_t, bias_last, bias_far, *([cache_t] * pps), new_rows, new_bias)


def _win_decode_kernel(q_ref, buf_ref, new_ref, bias_ref, nbias_ref, sink_ref, o_ref, *, G, bb):
    def one_sequence(j, carry):
        for g in range(G):
            q = q_ref[j, g]
            kb = buf_ref[j, LANES * g:LANES * (g + 1), :].astype(BF16)
            kn = new_ref[j, :, LANES * g:LANES * (g + 1)].astype(BF16)
            sb = _dot(q, kb) + bias_ref[g]
            sn = _dot_t(q, kn) + nbias_ref[g]
            sink = sink_ref[g]
            m = jnp.maximum(jnp.maximum(jnp.max(sb, axis=-1, keepdims=True), jnp.max(sn, axis=-1, keepdims=True)), sink)
            pb = jnp.exp(sb - m)
            pn = jnp.exp(sn - m)
            l = jnp.exp(sink - m) + jnp.sum(pb, axis=-1, keepdims=True) + jnp.sum(pn, axis=-1, keepdims=True)
            acc = _dot_t(pb.astype(BF16), kb) + _dot(pn.astype(BF16), kn)
            o_ref[j, g] = (acc / l).astype(o_ref.dtype)
        return carry

    lax.fori_loop(0, bb, one_sequence, 0)


def _win_decode(q_stack, buf, new_rows, bias, new_bias, sinks):
    Bd, G, M, _ = q_stack.shape
    bb = 4 if Bd % 4 == 0 else 1
    per_b = lambda a: pl.BlockSpec((bb,) + a.shape[1:], lambda b: (b,) + (0,) * (a.ndim - 1))
    return pl.pallas_call(
        functools.partial(_win_decode_kernel, G=G, bb=bb), grid=(Bd // bb,),
        in_specs=[per_b(q_stack), per_b(buf), per_b(new_rows), _const_spec(bias.shape), _const_spec(new_bias.shape),
                  _const_spec(sinks.shape)],
        out_specs=pl.BlockSpec((bb, G, M, LANES), lambda b: (b, 0, 0, 0)),
        out_shape=jax.ShapeDtypeStruct((Bd, G, M, LANES), BF16),
        compiler_params=_cp("parallel"), name="win_decode")(q_stack, buf, new_rows, bias, new_bias, sinks)


def _ffn_chunk(f):
    for cand in range(min(f, 1536) // LANES, 0, -1):
        if f % (cand * LANES) == 0:
            return cand * LANES
    return f


def _out_ffn_kernel(*refs, mode, final, n_mods):
    it = iter(refs)
    y_ref = next(it)
    if mode == "c":
        attn_in = next(it)[0]
    else:
        oa_ref, ocmp_ref, osel_ref, owin_ref, gate_ref, eg_ref = (next(it) for _ in range(6))
        if mode == "ab_sample":
            wuv_ref = next(it)
    wout_ref = next(it)
    gt_ref, fsh_ref, fsc_ref, fgt_ref = (next(it) for _ in range(4))
    wg_ref, wu_ref, wo_ref = (next(it) for _ in range(3))
    gain_ref = next(it) if final else None
    o_ref = next(it)

    if mode == "c":
        attn = _dot(attn_in, wout_ref[...])
    else:
        gates = gate_ref[0]
        g_hi = gates.astype(BF16)
        g_lo = (gates - g_hi.astype(F32)).astype(BF16)
        o_b = None
        for br, ref in enumerate((ocmp_ref, osel_ref, owin_ref)):
            ge = _dot(g_hi, eg_ref[br]) + _dot(g_lo, eg_ref[br])
            term = ge * ref[0].astype(F32)
            o_b = term if o_b is None else o_b + term
        o_a = oa_ref[0]
        if mode == "ab_sample":
            o_a = _dot(o_a, wuv_ref[...]).astype(BF16)
        na = o_a.shape[-1]
        attn = _dot(o_a, wout_ref[0:na, :]) + _dot(o_b.astype(BF16), wout_ref[na:, :])
    y1 = y_ref[0] + gt_ref[0] * attn
    h = _modulate(y1, fsh_ref[0], fsc_ref[0]).astype(BF16)
    f = wg_ref.shape[1]
    fc = _ffn_chunk(f)
    acc = None
    for k in range(f // fc):
        g = _dot(h, wg_ref[:, fc * k:fc * (k + 1)])
        u = _dot(h, wu_ref[:, fc * k:fc * (k + 1)])
        a = (g * _sigmoid(g) * u).astype(BF16)
        part = _dot(a, wo_ref[fc * k:fc * (k + 1), :])
        acc = part if acc is None else acc + part
    y2 = y1 + fgt_ref[0] * acc
    if final:
        y2 = _rms(y2) * gain_ref[...]
    o_ref[0] = y2


def _out_ffn(y, attn_parts, weights, mods, ffn_w, gain, mode, tm):
    Bx, Tx, D = y.shape
    per_token = mods[0].shape[1] != 1
    mod_spec = (pl.BlockSpec((1, tm, D), lambda b, i: (b, i, 0)) if per_token
                else pl.BlockSpec((1, 1, D), lambda b, i: (b, 0, 0)))
    tok = lambda a: pl.BlockSpec((1, tm, a.shape[-1]), lambda b, i: (b, i, 0))
    single = lambda a: pl.BlockSpec(a.shape, lambda b, i: (0,) * a.ndim, pipeline_mode=pl.Buffered(1))
    args = [y] + list(attn_parts) + list(weights) + list(mods) + list(ffn_w)
    in_specs = ([tok(y)] + [tok(a) for a in attn_parts] + [single(a) for a in weights] + [mod_spec] * 4
                + [single(a) for a in ffn_w])
    final = gain is not None
    if final:
        args.append(gain)
        in_specs.append(single(gain))
    return pl.pallas_call(
        functools.partial(_out_ffn_kernel, mode=mode, final=final, n_mods=4), grid=(Bx, Tx // tm),
        in_specs=in_specs, out_specs=tok(y), out_shape=jax.ShapeDtypeStruct(y.shape, F32),
        compiler_params=_cp("parallel", "parallel"), name="out_ffn_" + mode)(*args)


def _stack_heads(q, G, R, tq):
    B, T, _ = q.shape
    q = q.reshape(B, T // tq, tq, G, R, HEAD_DIM).transpose(0, 3, 1, 4, 2, 5).reshape(B, G, T // tq, R * tq, HEAD_DIM)
    return jnp.pad(q, ((0, 0),) * 4 + ((0, LANES - HEAD_DIM),))


def _unstack_heads(o, R, tq):
    B, G, nq, _, _ = o.shape
    o = o[..., HEAD_DIM:].reshape(B, G, nq, R, tq, HEAD_DIM).transpose(0, 2, 4, 1, 3, 5)
    return o.reshape(B, nq * tq, G * R * HEAD_DIM)


def _stack_heads_sample(q, G, R, S, s_pad):
    Bd = q.shape[0] // S
    q = q.reshape(Bd, S, G, R, HEAD_DIM).transpose(0, 2, 3, 1, 4)
    q = jnp.pad(q, ((0, 0), (0, 0), (0, 0), (0, s_pad - S), (0, LANES - HEAD_DIM)))
    return q.reshape(Bd, G, R * s_pad, LANES)


def _unstack_heads_sample(o, R, S, s_pad):
    Bd, G, _, _ = o.shape
    o = o[..., HEAD_DIM:].reshape(Bd, G, R, s_pad, HEAD_DIM)[:, :, :, :S].transpose(0, 3, 1, 2, 4)
    return o.reshape(1, Bd * S, G * R * HEAD_DIM)


def _group_rows(t, G):
    H, rows, C = t.shape
    return t.reshape(G, (H // G) * rows, C)


def _even_odd(n):
    return jnp.concatenate([jnp.arange(0, n, 2), jnp.arange(1, n, 2)]).astype(jnp.int32)


def _rope_tables(pos):
    half = MLA_ROPE // 2
    freq = ROPE_THETA ** (-jnp.arange(half, dtype=F32) / half)
    ang = pos.astype(F32)[:, None] * freq[None, :]
    cos, sin = jnp.cos(ang), jnp.sin(ang)
    n = pos.shape[0]
    one, zero = jnp.ones, jnp.zeros
    cq = jnp.concatenate([one((n, MLA_NOPE), F32), cos, cos, zero((n, LANES - MLA_NOPE - MLA_ROPE), F32)], axis=1)
    sq = jnp.concatenate([zero((n, MLA_NOPE), F32), sin, sin, zero((n, LANES - MLA_NOPE - MLA_ROPE), F32)], axis=1)
    n_gate = 3 * NSA_HEADS
    cm = jnp.concatenate([cos, cos, one((n, n_gate), F32), zero((n, LANES - MLA_ROPE - n_gate), F32)], axis=1)
    sm = jnp.concatenate([sin, sin, zero((n, LANES - MLA_ROPE), F32)], axis=1)
    return cq, sq, cm, sm


def _layer0_weights(w_in_0, mla_q_norm, mla_w_uq, mla_kv_norm, mla_w_uk, mla_w_uv):
    D = w_in_0.shape[0]
    qr, kvr = mla_q_norm.shape[0], mla_kv_norm.shape[0]
    half = MLA_ROPE // 2
    o_kr = qr + kvr
    o_q = o_kr + MLA_ROPE
    o_cmp = o_q + NSA_HEADS * HEAD_DIM
    kvw = NSA_KV_HEADS * 2 * HEAD_DIM
    o_g = o_cmp + 3 * kvw
    w_kr = w_in_0[:, o_kr:o_q]
    w_g = w_in_0[:, o_g:]
    z = lambda n: jnp.zeros((D, n), F32)
    misc_a = jnp.concatenate([w_kr, w_g, z(LANES - MLA_ROPE - w_g.shape[1])], axis=1)
    misc_b = jnp.concatenate([-w_kr[:, half:], w_kr[:, :half], z(LANES - MLA_ROPE)], axis=1)
    w0 = jnp.concatenate([w_in_0[:, :o_kr], misc_a, misc_b, w_in_0[:, o_q:o_cmp] * ATTN_SCALE, w_in_0[:, o_cmp:o_g]],
                         axis=1).astype(BF16)
    H = MLA_HEADS
    wq = mla_w_uq.reshape(qr, H, MLA_NOPE + MLA_ROPE)
    nope, x1, x2 = wq[..., :MLA_NOPE], wq[..., MLA_NOPE:MLA_NOPE + half], wq[..., MLA_NOPE + half:]
    zq = lambda n: jnp.zeros((qr, H, n), F32)
    pad = LANES - MLA_NOPE - MLA_ROPE
    wuq = jnp.concatenate([nope, x1, x2, zq(pad)], axis=-1).reshape(qr, H * LANES).astype(BF16)
    wuqs = jnp.concatenate([zq(MLA_NOPE), -x2, x1, zq(pad)], axis=-1).reshape(qr, H * LANES).astype(BF16)
    k_top = jnp.pad(mla_w_uk, ((0, 0), (0, 0), (0, LANES - MLA_NOPE))).reshape(kvr, H * LANES)
    place = jnp.zeros((LANES, H, LANES), F32).at[jnp.arange(MLA_ROPE), :, MLA_NOPE + jnp.arange(MLA_ROPE)].set(1.0)
    kcat = jnp.concatenate([k_top, place.reshape(LANES, H * LANES)], axis=0).astype(BF16)
    even = (jnp.arange(H) % 2 == 0)[None, :, None]
    zv = jnp.zeros((kvr, H, LANES - MLA_V), F32)
    wuv = jnp.where(even, jnp.concatenate([mla_w_uv, zv], axis=-1), jnp.concatenate([zv, mla_w_uv], axis=-1))
    wuv = wuv.reshape(kvr, H * LANES).astype(BF16)
    ones_v, zero_v = jnp.ones((1, H, MLA_V), F32), jnp.zeros((1, H, MLA_V), F32)
    vones = jnp.where(even, jnp.concatenate([zero_v, ones_v], axis=-1), jnp.concatenate([ones_v, zero_v], axis=-1))
    vones = vones.reshape(1, H * LANES)
    width = kvr + MLA_ROPE
    blk = jnp.zeros((H, LANES, width), F32)
    blk = blk.at[:, :MLA_NOPE, :kvr].set(jnp.transpose(mla_w_uk, (1, 2, 0)))
    blk = blk.at[:, MLA_NOPE + jnp.arange(MLA_ROPE), kvr + jnp.arange(MLA_ROPE)].set(1.0)
    eye = jnp.eye(H, dtype=F32)
    a_abs = jnp.einsum("hij,hk->hikj", blk, eye).reshape(H * LANES, H * width).astype(BF16)
    wuv_bd = jnp.einsum("chd,hk->hckd", mla_w_uv, eye).reshape(H * kvr, H * MLA_V).astype(BF16)
    return dict(w0=w0, qn=mla_q_norm.reshape(1, qr), kvn=mla_kv_norm.reshape(1, kvr), wuq=wuq, wuqs=wuqs, kcat=kcat,
                wuv=wuv, vones=vones, a_abs=a_abs, wuv_bd=wuv_bd)


def _gate_expand():
    h = jnp.arange(NSA_HEADS)
    mats = []
    for br in range(3):
        m = jnp.zeros((LANES, NSA_HEADS, HEAD_DIM), F32).at[MLA_ROPE + 3 * h + br, h, :].set(1.0)
        mats.append(m.reshape(LANES, NSA_HEADS * HEAD_DIM))
    return jnp.stack(mats).astype(BF16)


def kernel(x_prompt, x_sample, cache_mla, cache_nsa_cmp, cache_nsa_sel, state_nsa_win, state_swa, page_table, c_prompt, c_sample, rel_bias_table, w_ada_0, b_ada_0, w_in_0, mla_q_norm, mla_w_uq, mla_kv_norm, mla_w_uk, mla_w_uv, nsa_w_cmp, nsa_pe_cmp, w_out_0, w_ffn_in_0, w_ffn_out_0, w_ada_1, b_ada_1, w_in_1, swa_sinks, w_out_1, w_ffn_in_1, w_ffn_out_1, final_norm):
    B, T, D = x_prompt.shape
    Bd, S, _ = x_sample.shape
    n_pool, PAGE, mla_w = cache_mla.shape
    n_pages = page_table.shape[1]
    PAST = n_pages * PAGE
    G, R = NSA_KV_HEADS, NSA_HEADS // NSA_KV_HEADS
    G1, R1 = SWA_KV_HEADS, SWA_HEADS // SWA_KV_HEADS
    tq = Q_BLOCK
    nq = T // tq
    NS = Bd * S
    S_PAD = 8
    tm_p = 512 if T % 512 == 0 else 256
    tm_s = 256 if NS % 256 == 0 else NS
    i32 = jnp.int32
    table = rel_bias_table.astype(F32)

    n_c = B + Bd
    c_all = jnp.pad(jnp.concatenate([c_prompt, c_sample], axis=0), ((0, (-n_c) % 8), (0, 0)))

    def mods_for(w_ada, b_ada):
        m = _ada(c_all, w_ada.astype(BF16), b_ada.reshape(1, -1))
        mp = [m[:B, k * D:(k + 1) * D][:, None, :] for k in range(6)]
        ms = [jnp.repeat(m[B:B + Bd, k * D:(k + 1) * D], S, axis=0)[None] for k in range(6)]
        return mp, ms

    mods0_p, mods0_s = mods_for(w_ada_0, b_ada_0)
    mods1_p, mods1_s = mods_for(w_ada_1, b_ada_1)
    xs = x_sample.reshape(1, NS, D)

    w0 = _layer0_weights(w_in_0, mla_q_norm, mla_w_uq, mla_kv_norm, mla_w_uk, mla_w_uv)
    tabs_p = _rope_tables(jnp.arange(T, dtype=i32))
    tabs_s = tuple(jnp.tile(t, (Bd, 1)) for t in _rope_tables(PAST + jnp.arange(S, dtype=i32)))
    (mla_p, q_mla, k_mla, v_mla, q_stack_p, cmp_p, sel_p, win_p, selb_p, winb_p, gate_p) = _proj0(
        x_prompt, mods0_p[0], mods0_p[1], w0, tabs_p, False, tm_p)
    (mla_s, qcat_s, qn_s, cmp_s, sel_s, win_s, gate_s) = _proj0(xs, mods0_s[0], mods0_s[1], w0, tabs_s, True, tm_s)

    o_a_p = _mla_prompt(q_mla, k_mla, v_mla, 512 if T % 512 == 0 else 256)

    eye2 = jnp.eye(2, dtype=F32)
    wbig = jnp.einsum("lcde,gh,ck->lgcdhke", nsa_w_cmp, eye2, eye2).reshape(NSA_CMP_BLOCK * 4 * HEAD_DIM, 4 * HEAD_DIM)
    wbig = wbig.astype(BF16)
    pe_rows = jnp.broadcast_to(nsa_pe_cmp[:, None], (NSA_CMP_BLOCK, G, 2, HEAD_DIM)).reshape(1, -1)
    pe_rows = jnp.broadcast_to(pe_rows, (8, pe_rows.shape[1]))
    pe_bias = _pe_bias(pe_rows, wbig)
    NC = T // NSA_CMP_BLOCK
    kvc_p = _compress_prompt(cmp_p.reshape(B, NC, -1), wbig, pe_bias)
    order_p = _even_odd(NC)
    kvc_p = kvc_p[:, order_p]

    qpos = jnp.arange(T, dtype=i32)
    dist = qpos[:, None] - (order_p * NSA_CMP_BLOCK + NSA_CMP_BLOCK - 1)[None, :]
    bias_cmp_p = _bias_tiles(table, _masked_bucket(dist, dist >= 0), NSA_HEADS)
    bias_cmp_p = bias_cmp_p.reshape(G, R, nq, tq, NC).transpose(2, 0, 1, 3, 4).reshape(nq, G, R * tq, NC)
    n_sel = T // NSA_SEL_BLOCK
    blk = jnp.arange(n_sel, dtype=i32)[None, :]
    cur = (qpos // NSA_SEL_BLOCK)[:, None]
    forced = (blk == 0) | (blk == cur) | (blk == cur - 1)
    causal = blk * NSA_SEL_BLOCK <= qpos[:, None]
    code_p = jnp.where(causal, jnp.where(forced, SELECT_BIG, 0.0), -SELECT_BIG).astype(F32).reshape(nq, tq, n_sel)
    assert n_sel <= LANES and n_sel == NC // 2, "selection blocks must fit one lane tile"
    o_cmp_t, sel_rows_p = _cmp_select_t(q_stack_p, kvc_p, kvc_p.transpose(0, 2, 1), bias_cmp_p.transpose(0, 1, 3, 2),
                                       code_p.transpose(0, 2, 1), tq)
    o_cmp_p = o_cmp_t[:, :, :, HEAD_DIM:, :].reshape(B, G, nq, HEAD_DIM, R, tq).transpose(0, 2, 5, 1, 4, 3)
    o_cmp_p = o_cmp_p.reshape(B, T, G * R * HEAD_DIM)

    onehot_p = (jnp.arange(LANES, dtype=i32)[None, :] == (qpos // NSA_SEL_BLOCK)[:, None]).astype(BF16)
    ql = jnp.arange(tq, dtype=i32)[:, None]
    d_near = ql + tq - jnp.arange(2 * tq, dtype=i32)[None, :]
    bk_near = jnp.concatenate([_masked_bucket(d_near, d_near >= 0), jnp.full((tq, 2 * tq), REL_BUCKETS - 1, i32)], axis=0)
    near_far = _bias_tiles(table, bk_near, NSA_HEADS)
    bias_near_p = _group_rows(near_far[:, :tq], G)
    bias_far_p = _group_rows(near_far[:, tq:, :LANES], G)
    tk_sel = 512 if T % 512 == 0 else tq
    o_sel_st = _sel_prompt(q_stack_p, sel_rows_p, selb_p, onehot_p, bias_near_p, bias_far_p, tq, tk_sel)

    def window_bias(window, n_heads, n_groups):
        nw = -(-window // tq)
        dw = ql + nw * tq - jnp.arange((nw + 1) * tq, dtype=i32)[None, :]
        t = _bias_tiles(table, _masked_bucket(dw, (dw >= 0) & (dw < window)), n_heads)
        return _group_rows(t, n_groups), nw

    bias_win_p, nw0 = window_bias(NSA_WINDOW, NSA_HEADS, G)
    no_sink0 = jnp.full((G, R * tq, 1), NEG, F32)
    o_win_st = _banded(q_stack_p, winb_p, bias_win_p, no_sink0, tq, nw0)

    o_sel_p = _unstack_heads(o_sel_st, R, tq)
    o_win_p = _unstack_heads(o_win_st, R, tq)

    pps = 8 if n_pages % 8 == 0 else n_pages
    srow = jnp.arange(S_PAD, dtype=i32)
    s_real = jnp.minimum(srow, S - 1)
    q_cat = qcat_s.reshape(Bd, S * MLA_HEADS, mla_w)
    mla_new = jnp.pad(mla_s.reshape(Bd, S, mla_w), ((0, 0), (0, S_PAD - S), (0, 0)))
    s_of_row = jnp.repeat(jnp.arange(S, dtype=i32), MLA_HEADS)[:, None]
    new_mask = jnp.where((srow[None, :] <= s_of_row) & (srow[None, :] < S), 0.0, NEG).astype(F32)
    pps_m = 16 if n_pages % 16 == 0 else pps
    o_full_s = _mla_decode(page_table, q_cat, cache_mla.transpose(0, 2, 1), mla_new, new_mask, pps_m)
    o_full_s = o_full_s.reshape(1, NS, MLA_HEADS * (mla_w - MLA_ROPE))

    pps_c = 32 if n_pages % 32 == 0 else pps_m
    feat = G * 2 * HEAD_DIM
    cache_cmp_t = cache_nsa_cmp.transpose(0, 2, 3, 4, 1).reshape(n_pool, feat, PAGE)
    w_head = wbig.reshape(NSA_CMP_BLOCK, feat, feat)[:, :LANES, :LANES]
    kvc_s = _compress_paged(page_table, cache_cmp_t, w_head, pe_bias, pps_c)
    NCs = PAST // NSA_CMP_BLOCK
    order_s = _even_odd(NCs)
    kvc_s = kvc_s[:, order_s]
    q_stack_s = _stack_heads_sample(qn_s[0], G, R, S, S_PAD)
    pos_s = PAST + s_real
    dist_s = pos_s[:, None] - (order_s * NSA_CMP_BLOCK + NSA_CMP_BLOCK - 1)[None, :]
    bias_cmp_s = _group_rows(_bias_tiles(table, _masked_bucket(dist_s, dist_s >= 0), NSA_HEADS), G)[None]
    n_past_blk = PAST // NSA_SEL_BLOCK
    n_sel_s = n_past_blk + -(-S // NSA_SEL_BLOCK)
    sel_lanes = -(-n_sel_s // LANES) * LANES
    blk_s = jnp.arange(sel_lanes, dtype=i32)[None, :]
    cur_s = (pos_s // NSA_SEL_BLOCK)[:, None]
    forced_s = (blk_s == 0) | (blk_s == cur_s) | (blk_s == cur_s - 1)
    causal_s = (blk_s * NSA_SEL_BLOCK <= pos_s[:, None]) & (blk_s < n_sel_s)
    code_s = jnp.where(causal_s, jnp.where(forced_s, SELECT_BIG, 0.0), -SELECT_BIG).astype(F32)[None]
    o_cmp_ss, sel_mask_s = _cmp_select(q_stack_s[:, :, None], kvc_s, bias_cmp_s, code_s, S_PAD, n_sel_s, PAST)
    o_cmp_s = _unstack_heads_sample(o_cmp_ss[:, :, 0], R, S, S_PAD)

    assert n_past_blk <= LANES and PAGE >= REL_MAX_DISTANCE, "past selection blocks must fit one lane tile"
    sel_past = jnp.pad(sel_mask_s[:, :, 0, :, :n_past_blk], ((0, 0),) * 3 + ((0, LANES - n_past_blk),))
    q_aug_s = jnp.concatenate([q_stack_s, jnp.tile(sel_past, (1, 1, R, 1))], axis=-1)
    key_blk_s = (jnp.arange(PAST, dtype=i32) // NSA_SEL_BLOCK).reshape(n_pages, 1, PAGE)
    onehot_s = (jnp.arange(LANES, dtype=i32)[None, :, None] == key_blk_s).astype(BF16)
    d_last = pos_s[:, None] - (PAST - PAGE + jnp.arange(PAGE, dtype=i32))[None, :]
    bk_last = jnp.concatenate([_masked_bucket(d_last, d_last >= 0), jnp.full((S_PAD, PAGE), REL_BUCKETS - 1, i32)], axis=0)
    last_far = _bias_tiles(table, bk_last, NSA_HEADS)
    bias_last_s = _group_rows(last_far[:, :S_PAD], G)
    bias_far_s = _group_rows(last_far[:, S_PAD:], G)
    d_new = s_real[:, None] - srow[None, :]
    bk_new = _masked_bucket(d_new, (d_new >= 0) & (srow[None, :] < S))
    nbias_nsa = _group_rows(_bias_tiles(table, bk_new, NSA_HEADS), G)
    sel_new8 = jnp.pad(sel_s.reshape(Bd, S, -1), ((0, 0), (0, S_PAD - S), (0, 0)))
    cache_sel_t = cache_nsa_sel.transpose(0, 2, 3, 4, 1).reshape(n_pool, feat, PAGE)
    flat = lambda a: a.reshape(-1, a.shape[-1])
    o_sel_ss = _sel_decode(page_table, q_aug_s, onehot_s, flat(bias_last_s), flat(bias_far_s), cache_sel_t, sel_new8,
                           flat(nbias_nsa), pps_m)
    o_sel_s = _unstack_heads_sample(o_sel_ss, R, S, S_PAD)

    def state_bias(wb, window, n_heads, n_groups):
        d = wb + s_real[:, None] - jnp.arange(wb, dtype=i32)[None, :]
        return _group_rows(_bias_tiles(table, _masked_bucket(d, d < window), n_heads), n_groups)

    wb0 = state_nsa_win.shape[1]
    win_new8 = jnp.pad(win_s.reshape(Bd, S, -1), ((0, 0), (0, S_PAD - S), (0, 0)))
    no_sink0_s = jnp.full((G, R * S_PAD, 1), NEG, F32)
    o_win_ss = _win_decode(q_stack_s, state_nsa_win.transpose(0, 2, 3, 4, 1).reshape(Bd, feat, wb0), win_new8,
                           state_bias(wb0, NSA_WINDOW, NSA_HEADS, G), nbias_nsa, no_sink0_s)
    o_win_s = _unstack_heads_sample(o_win_ss, R, S, S_PAD)

    eg = _gate_expand()
    w_out0 = w_out_0.astype(BF16)
    f = w_ffn_out_0.shape[0]
    ffn0 = (w_ffn_in_0[:, :f].astype(BF16), w_ffn_in_0[:, f:].astype(BF16), w_ffn_out_0.astype(BF16))
    y1_p = _out_ffn(x_prompt, [o_a_p, o_cmp_p, o_sel_p, o_win_p, gate_p], [eg, w_out0], mods0_p[2:], ffn0, None,
                    "ab_prompt", tm_p)
    y1_s = _out_ffn(xs, [o_full_s, o_cmp_s, o_sel_s, o_win_s, gate_s], [eg, w0["wuv_bd"], w_out0], mods0_s[2:], ffn0,
                    None, "ab_sample", tm_s)

    nq1 = SWA_HEADS * HEAD_DIM
    w1 = jnp.concatenate([w_in_1[:, :nq1] * ATTN_SCALE, w_in_1[:, nq1:]], axis=1).astype(BF16)
    q1_p, kv1_p, kvb1_p = _proj1(y1_p, mods1_p[0], mods1_p[1], w1, nq1, tm_p, True)
    q1_s, kv1_s, _ = _proj1(y1_s, mods1_s[0], mods1_s[1], w1, nq1, tm_s, False)
    bias_swa_p, nw1 = window_bias(SWA_WINDOW, SWA_HEADS, G1)
    sink_p = jnp.repeat(swa_sinks.astype(F32).reshape(G1, R1), tq, axis=1).reshape(G1, R1 * tq, 1)
    o_c_st = _banded(q1_p, kvb1_p, bias_swa_p, sink_p, tq, nw1)
    o_c_p = _unstack_heads(o_c_st, R1, tq)

    wb1 = state_swa.shape[1]
    kv_new8 = jnp.pad(kv1_s.reshape(Bd, S, -1), ((0, 0), (0, S_PAD - S), (0, 0)))
    d_new1 = s_real[:, None] - srow[None, :]
    nbias_swa = _group_rows(_bias_tiles(table, _masked_bucket(d_new1, (d_new1 >= 0) & (srow[None, :] < S)), SWA_HEADS),
                            G1)
    sink_s = jnp.repeat(swa_sinks.astype(F32).reshape(G1, R1), S_PAD, axis=1).reshape(G1, R1 * S_PAD, 1)
    state_swa_t = state_swa.transpose(0, 2, 3, 4, 1).reshape(Bd, G1 * 2 * HEAD_DIM, wb1)
    o_c_ss = _win_decode(_stack_heads_sample(q1_s[0], G1, R1, S, S_PAD), state_swa_t, kv_new8,
                         state_bias(wb1, SWA_WINDOW, SWA_HEADS, G1), nbias_swa, sink_s)
    o_c_s = _unstack_heads_sample(o_c_ss, R1, S, S_PAD)

    w_out1 = w_out_1.astype(BF16)
    f1 = w_ffn_out_1.shape[0]
    ffn1 = (w_ffn_in_1[:, :f1].astype(BF16), w_ffn_in_1[:, f1:].astype(BF16), w_ffn_out_1.astype(BF16))
    gain = final_norm.reshape(1, D).astype(F32)
    y_prompt = _out_ffn(y1_p, [o_c_p], [w_out1], mods1_p[2:], ffn1, gain, "c", tm_p)
    y_sample = _out_ffn(y1_s, [o_c_s], [w_out1], mods1_s[2:], ffn1, gain, "c", tm_s).reshape(Bd, S, D)

    row5 = lambda a, lead, g: a.reshape(lead + (g, 2, HEAD_DIM))
    win_p5 = row5(win_p, (B, T), G)
    kv1_p5 = row5(kv1_p, (B, T), G1)
    win_s5 = row5(win_s, (Bd, S), G)
    kv1_s5 = row5(kv1_s, (Bd, S), G1)
    return (y_prompt, y_sample, mla_p, mla_s.reshape(Bd, S, mla_w),
            row5(cmp_p, (B, T), G), row5(cmp_s, (Bd, S), G), row5(sel_p, (B, T), G), row5(sel_s, (Bd, S), G),
            win_p5[:, T - min(NSA_WINDOW, T):], jnp.concatenate([state_nsa_win, win_s5], axis=1)[:, S:],
            kv1_p5[:, T - min(SWA_WINDOW, T):], jnp.concatenate([state_swa, kv1_s5], axis=1)[:, S:])
```

```python
import functools
import math

import jax
import jax.numpy as jnp
import numpy as np
from jax import lax
from jax.experimental import pallas as pl
from jax.experimental.pallas import tpu as pltpu

F32 = jnp.float32
BF16 = jnp.bfloat16

MLA_HEADS, MLA_NOPE, MLA_ROPE, MLA_V = 8, 64, 32, 64
ROPE_THETA = 10000.0
NSA_HEADS, NSA_KV_HEADS, HEAD_DIM = 8, 2, 64
NSA_CMP_BLOCK, NSA_SEL_BLOCK, NSA_TOP_N, NSA_WINDOW = 32, 64, 16, 512
SWA_HEADS, SWA_KV_HEADS, SWA_WINDOW = 16, 4, 128
REL_BUCKETS, REL_MAX_DISTANCE = 32, 128
Q_BLOCK = 128
NORM_EPS = 1e-6
NEG = -1e30
SELECT_BIG = 1e9
MLA_SCALE = (MLA_NOPE + MLA_ROPE) ** -0.5
ATTN_SCALE = HEAD_DIM ** -0.5

LANES = 128
VMEM_LIMIT = 52 * 1024 * 1024


def _cp(*sem):
    return pltpu.CompilerParams(dimension_semantics=sem, vmem_limit_bytes=VMEM_LIMIT)


def _dot(a, b):
    return jnp.dot(a, b, preferred_element_type=F32)


def _dot_t(a, b):
    return lax.dot_general(a, b, (((1,), (1,)), ((), ())), preferred_element_type=F32)


def _rms(x):
    return x * lax.rsqrt(jnp.mean(x * x, axis=-1, keepdims=True) + NORM_EPS)


def _sigmoid(x):
    return 1.0 / (1.0 + jnp.exp(-x))


def _const_spec(shape):
    n = len(shape)
    return pl.BlockSpec(shape, lambda *_: (0,) * n)


def _softmax_update(scores, values, m_ref, l_ref, acc_ref, transposed_values=False):
    m_prev = m_ref[...]
    m_new = m_prev
    for s in scores:
        m_new = jnp.maximum(m_new, jnp.max(s, axis=-1, keepdims=True))
    alpha = jnp.exp(m_prev - m_new)
    l_new = alpha * l_ref[...]
    acc = alpha * acc_ref[...]
    for s, v in zip(scores, values):
        p = jnp.exp(s - m_new)
        l_new = l_new + jnp.sum(p, axis=-1, keepdims=True)
        pb = p.astype(BF16)
        acc = acc + (_dot_t(pb, v) if transposed_values else _dot(pb, v))
    m_ref[...] = m_new
    l_ref[...] = l_new
    acc_ref[...] = acc


def _lane_tile(x, width):
    reps = width // x.shape[-1]
    return x if reps == 1 else jnp.concatenate([x] * reps, axis=1)


def _flash_step(scores, values, m_ref, acc_ref):
    m_prev = m_ref[...]
    m_cur = None
    for s in scores:
        mx = jnp.max(s, axis=-1, keepdims=True)
        m_cur = mx if m_cur is None else jnp.maximum(m_cur, mx)
    m_new = jnp.maximum(m_prev, m_cur)
    acc = acc_ref[...] * _lane_tile(jnp.exp(m_prev - m_new), acc_ref.shape[-1])
    for s, v in zip(scores, values):
        p = jnp.exp(s - _lane_tile(m_new, s.shape[-1]))
        acc = acc + _dot(p.astype(BF16), v)
    m_ref[...] = m_new
    acc_ref[...] = acc


N_CHAINS = 1


def _merge_chains(m_ref, l_ref, acc_ref):
    m = m_ref[0]
    for k in range(1, m_ref.shape[0]):
        m = jnp.maximum(m, m_ref[k])
    l = acc = None
    for k in range(m_ref.shape[0]):
        w = jnp.exp(m_ref[k] - m)
        l = w * l_ref[k] if l is None else l + w * l_ref[k]
        acc = w * acc_ref[k] if acc is None else acc + w * acc_ref[k]
    return l, acc


def _softmax_init(m_ref, l_ref, acc_ref):
    m_ref[...] = jnp.full(m_ref.shape, NEG, F32)
    l_ref[...] = jnp.zeros(l_ref.shape, F32)
    acc_ref[...] = jnp.zeros(acc_ref.shape, F32)


def _ada_kernel(c_ref, w_ref, b_ref, o_ref):
    c = c_ref[...]
    a = (c * _sigmoid(c)).astype(BF16)
    o_ref[...] = _dot(a, w_ref[...]) + b_ref[...]


def _ada(c, w, b):
    M, D = c.shape
    N = w.shape[1]
    tn = 1024 if N % 1024 == 0 else N
    return pl.pallas_call(
        _ada_kernel, grid=(N // tn,),
        in_specs=[pl.BlockSpec((M, D), lambda j: (0, 0)), pl.BlockSpec((D, tn), lambda j: (0, j)),
                  pl.BlockSpec((1, tn), lambda j: (0, j))],
        out_specs=pl.BlockSpec((M, tn), lambda j: (0, j)),
        out_shape=jax.ShapeDtypeStruct((M, N), F32), compiler_params=_cp("parallel"), name="ada")(c, w, b)


def _bias_kernel(tab_ref, bkt_ref, o_ref):
    h = pl.program_id(0)
    bkt = bkt_ref[...]
    acc = jnp.full(bkt.shape, NEG, F32)
    for b in range(REL_BUCKETS):
        acc = jnp.where(bkt == b, tab_ref[b, h], acc)
    o_ref[0] = acc


def _bias_tiles(table, buckets, n_heads):
    R, C = buckets.shape
    tr = R
    for cand in (512, 256, 128):
        if R > cand and R % cand == 0:
            tr = cand
            break
    return pl.pallas_call(
        _bias_kernel,
        grid_spec=pltpu.PrefetchScalarGridSpec(
            num_scalar_prefetch=1, grid=(n_heads, R // tr),
            in_specs=[pl.BlockSpec((tr, C), lambda h, r, tab: (r, 0))],
            out_specs=pl.BlockSpec((1, tr, C), lambda h, r, tab: (h, r, 0))),
        out_shape=jax.ShapeDtypeStruct((n_heads, R, C), F32),
        compiler_params=_cp("parallel", "parallel"), name="rel_bias")(table, buckets)


def _bucket_thresholds():
    exact = REL_BUCKETS // 2
    n = np.arange(exact, REL_MAX_DISTANCE + 1)
    scaled = np.log(n / exact) / math.log(REL_MAX_DISTANCE / exact)
    large = np.minimum(exact + np.trunc(scaled * (REL_BUCKETS - exact)).astype(np.int64), REL_BUCKETS - 1)
    return [int(n[np.argmax(large >= b)]) for b in range(exact + 1, REL_BUCKETS)]


def _t5_bucket(dist):
    n = jnp.maximum(dist, 0)
    exact = REL_BUCKETS // 2
    large = jnp.full(n.shape, exact, jnp.int32)
    for thr in _bucket_thresholds():
        large = large + jnp.where(n >= thr, 1, 0)
    return jnp.where(n < exact, n, large)


def _masked_bucket(dist, valid):
    return jnp.where(valid, _t5_bucket(dist), -1).astype(jnp.int32)


def _modulate(x, shift, scale):
    return _rms(x) * (1.0 + scale) + shift


def _store_stacked_heads(q, ref):
    _, G, n_tiles, M, _ = ref.shape
    R = M * n_tiles // q.shape[0]
    tq = M // R
    lane = lax.broadcasted_iota(jnp.int32, (q.shape[0], LANES), 1)
    for h in range(G * R):
        pair = q[:, LANES * (h // 2):LANES * (h // 2 + 1)]
        if h % 2:
            pair = pltpu.roll(pair, HEAD_DIM, 1)
        head = jnp.where(lane < HEAD_DIM, pair, 0.0).astype(ref.dtype)
        g, r = divmod(h, R)
        for t in range(n_tiles):
            ref[0, g, t, r * tq:(r + 1) * tq, :] = head[t * tq:(t + 1) * tq]


def _store_heads(o, o_ref, g, tq):
    R = o.shape[0] // tq
    lane = lax.broadcasted_iota(jnp.int32, (tq, LANES), 1)
    for p in range(R // 2):
        even = o[(2 * p) * tq:(2 * p + 1) * tq]
        odd = o[(2 * p + 1) * tq:(2 * p + 2) * tq]
        tile = jnp.where(lane < HEAD_DIM, pltpu.roll(even, HEAD_DIM, 1), odd)
        col = (g * R + 2 * p) * HEAD_DIM
        o_ref[0, :, col:col + LANES] = tile.astype(o_ref.dtype)


def _proj0_kernel(x_ref, sh_ref, sc_ref, w0_ref, qn_ref, kvn_ref, wuq_ref, wuqs_ref, cq_ref, sq_ref, cm_ref,
                  sm_ref, wa_ref, wb_ref, vones_ref, *outs, sample):
    h = _modulate(x_ref[0], sh_ref[0], sc_ref[0]).astype(BF16)
    y = _dot(h, w0_ref[...])
    qn = (_rms(y[:, 0:256]) * qn_ref[...]).astype(BF16)
    ckv = _rms(y[:, 256:512]) * kvn_ref[...]
    misc = y[:, 512:640] * cm_ref[...] + y[:, 640:768] * sm_ref[...]
    cq = jnp.concatenate([cq_ref[...]] * MLA_HEADS, axis=1)
    sq = jnp.concatenate([sq_ref[...]] * MLA_HEADS, axis=1)
    q_rot = ((_dot(qn, wuq_ref[...]) * cq + _dot(qn, wuqs_ref[...]) * sq) * MLA_SCALE).astype(BF16)
    ckv_b = ckv.astype(BF16)
    if sample:
        rows_ref, qcat_ref, qnsa_ref, cmp_ref, sel_ref, win_ref, gate_ref = outs
        qcat_ref[0] = _dot(q_rot, wa_ref[...]).astype(BF16)
    else:
        rows_ref, qmla_ref, kcat_ref, vmla_ref, qnsa_ref, cmp_ref, sel_ref, win_ref, selb_ref, winb_ref, gate_ref = outs
        qmla_ref[0] = q_rot
        kcat_ref[0] = _dot(jnp.concatenate([ckv_b, misc.astype(BF16)], axis=1), wa_ref[...]).astype(BF16)
        vmla_ref[0] = (_dot(ckv_b, wb_ref[...]) + vones_ref[...]).astype(BF16)
        selb_ref[0] = y[:, 1536:1792].astype(BF16)
        winb_ref[0] = y[:, 1792:2048].astype(BF16)
    rows_ref[0, :, 0:256] = ckv
    rows_ref[0, :, 256:288] = misc[:, 0:MLA_ROPE]
    if sample:
        qnsa_ref[0] = y[:, 768:1280].astype(BF16)
    else:
        _store_stacked_heads(y[:, 768:1280], qnsa_ref)
    cmp_ref[0] = y[:, 1280:1536]
    sel_ref[0] = y[:, 1536:1792]
    win_ref[0] = y[:, 1792:2048]
    gate_ref[0] = _sigmoid(misc)


def _proj0(x, shift, scale, w, tabs, sample, tm):
    Bx, Tx, D = x.shape
    per_token = shift.shape[1] != 1
    mod_spec = (pl.BlockSpec((1, tm, D), lambda b, i: (b, i, 0)) if per_token
                else pl.BlockSpec((1, 1, D), lambda b, i: (b, 0, 0)))
    tok = lambda n: pl.BlockSpec((1, tm, n), lambda b, i: (b, i, 0))
    tab_spec = pl.BlockSpec((tm, LANES), lambda b, i: (i, 0))
    wa, wb = (w["a_abs"], w["wuv"]) if sample else (w["kcat"], w["wuv"])
    in_specs = [tok(D), mod_spec, mod_spec, _const_spec(w["w0"].shape), _const_spec((1, 256)), _const_spec((1, 256)),
                _const_spec(w["wuq"].shape), _const_spec(w["wuqs"].shape), tab_spec, tab_spec, tab_spec, tab_spec,
                _const_spec(wa.shape), _const_spec(wb.shape), _const_spec(w["vones"].shape)]
    sd = lambda n, dt: jax.ShapeDtypeStruct((Bx, Tx, n), dt)
    if sample:
        out_shape = [sd(288, F32), sd(MLA_HEADS * 288, BF16), sd(512, BF16), sd(256, F32), sd(256, F32), sd(256, F32),
                     sd(LANES, F32)]
    else:
        out_shape = [sd(288, F32), sd(1024, BF16), sd(1024, BF16), sd(1024, BF16), sd(512, BF16), sd(256, F32),
                     sd(256, F32), sd(256, F32), sd(256, BF16), sd(256, BF16), sd(LANES, F32)]
    out_specs = [tok(s.shape[-1]) for s in out_shape]
    if not sample:
        G, R, nt = NSA_KV_HEADS, NSA_HEADS // NSA_KV_HEADS, tm // Q_BLOCK
        out_shape[4] = jax.ShapeDtypeStruct((Bx, G, Tx // Q_BLOCK, R * Q_BLOCK, LANES), BF16)
        out_specs[4] = pl.BlockSpec((1, G, nt, R * Q_BLOCK, LANES), lambda b, i: (b, 0, i, 0, 0))
    return pl.pallas_call(
        functools.partial(_proj0_kernel, sample=sample), grid=(Bx, Tx // tm), in_specs=in_specs, out_specs=out_specs,
        out_shape=out_shape, compiler_params=_cp("parallel", "parallel"),
        name="proj0_sample" if sample else "proj0_prompt")(
            x, shift, scale, w["w0"], w["qn"], w["kvn"], w["wuq"], w["wuqs"], *tabs, wa, wb, w["vones"])


def _proj1_kernel(x_ref, sh_ref, sc_ref, w_ref, q_ref, kv_ref, kvb_ref, *, nq, stacked):
    h = _modulate(x_ref[0], sh_ref[0], sc_ref[0]).astype(BF16)
    y = _dot(h, w_ref[...])
    if stacked:
        _store_stacked_heads(y[:, :nq], q_ref)
    else:
        q_ref[0] = y[:, :nq].astype(BF16)
    kv_ref[0] = y[:, nq:]
    kvb_ref[0] = y[:, nq:].astype(BF16)


def _proj1(x, shift, scale, w1, nq, tm, stacked):
    Bx, Tx, D = x.shape
    nkv = w1.shape[1] - nq
    per_token = shift.shape[1] != 1
    mod_spec = (pl.BlockSpec((1, tm, D), lambda b, i: (b, i, 0)) if per_token
                else pl.BlockSpec((1, 1, D), lambda b, i: (b, 0, 0)))
    tok = lambda n: pl.BlockSpec((1, tm, n), lambda b, i: (b, i, 0))
    q_shape, q_spec = jax.ShapeDtypeStruct((Bx, Tx, nq), BF16), tok(nq)
    if stacked:
        G, R, nt = SWA_KV_HEADS, SWA_HEADS // SWA_KV_HEADS, tm // Q_BLOCK
        q_shape = jax.ShapeDtypeStruct((Bx, G, Tx // Q_BLOCK, R * Q_BLOCK, LANES), BF16)
        q_spec = pl.BlockSpec((1, G, nt, R * Q_BLOCK, LANES), lambda b, i: (b, 0, i, 0, 0))
    return pl.pallas_call(
        functools.partial(_proj1_kernel, nq=nq, stacked=stacked), grid=(Bx, Tx // tm),
        in_specs=[tok(D), mod_spec, mod_spec, _const_spec(w1.shape)], out_specs=[q_spec, tok(nkv), tok(nkv)],
        out_shape=[q_shape, jax.ShapeDtypeStruct((Bx, Tx, nkv), F32), jax.ShapeDtypeStruct((Bx, Tx, nkv), BF16)],
        compiler_params=_cp("parallel", "parallel"), name="proj1")(x, shift, scale, w1)


def _mla_prompt_kernel(q_ref, k_ref, v_ref, o_ref, m_sc, acc_sc, *, tq):
    i = pl.program_id(2)
    q = q_ref[0]
    causal = lax.broadcasted_iota(jnp.int32, (tq, tq), 1) <= lax.broadcasted_iota(jnp.int32, (tq, tq), 0)
    m_sc[...] = jnp.full(m_sc.shape, NEG, F32)
    acc_sc[...] = jnp.zeros(acc_sc.shape, F32)

    def step(j, masked):
        start = pl.multiple_of(j * tq, tq)
        kk = k_ref[0, pl.ds(start, tq), :]
        vv = v_ref[0, pl.ds(start, tq), :]
        for e in range(2):
            sl = slice(LANES * e, LANES * (e + 1))
            s = _dot_t(q[:, sl], kk[:, sl])
            if masked:
                s = jnp.where(causal, s, NEG)
            _flash_step([s], [vv[:, sl]], m_sc.at[e], acc_sc.at[e])

    def body(j, carry):
        step(j, False)
        return carry

    lax.fori_loop(0, i, body, 0)
    step(i, True)
    a0, a1 = acc_sc[0], acc_sc[1]
    o0 = a0 / pltpu.roll(a0, MLA_V, 1)
    o1 = a1 / pltpu.roll(a1, MLA_V, 1)
    lane = lax.broadcasted_iota(jnp.int32, (tq, LANES), 1)
    o_ref[0] = jnp.where(lane < MLA_V, o0, o1).astype(o_ref.dtype)


def _mla_prompt(q, k, v, tq):
    B, T, _ = q.shape
    return pl.pallas_call(
        functools.partial(_mla_prompt_kernel, tq=tq), grid=(B, MLA_HEADS // 2, T // tq),
        in_specs=[pl.BlockSpec((1, tq, 2 * LANES), lambda b, h, i: (b, i, h)),
                  pl.BlockSpec((1, T, 2 * LANES), lambda b, h, i: (b, 0, h)),
                  pl.BlockSpec((1, T, 2 * LANES), lambda b, h, i: (b, 0, h))],
        out_specs=pl.BlockSpec((1, tq, LANES), lambda b, h, i: (b, i, h)),
        out_shape=jax.ShapeDtypeStruct((B, T, MLA_HEADS * MLA_V), BF16),
        scratch_shapes=[pltpu.VMEM((2, tq, LANES), F32), pltpu.VMEM((2, tq, LANES), F32)],
        compiler_params=_cp("parallel", "parallel", "parallel"), name="mla_prompt")(q, k, v)


def _pe_bias_kernel(pe_ref, w_ref, o_ref):
    o_ref[...] = _dot(pe_ref[...].astype(BF16), w_ref[...])


def _pe_bias(pe_rows, wbig):
    return pl.pallas_call(_pe_bias_kernel, out_shape=jax.ShapeDtypeStruct((pe_rows.shape[0], wbig.shape[1]), F32),
                          compiler_params=_cp(), name="pe_bias")(pe_rows, wbig)


def _compress_prompt_kernel(x_ref, w_ref, peb_ref, o_ref):
    o_ref[0] = (_dot(x_ref[0].astype(BF16), w_ref[...]) + peb_ref[0:1, :]).astype(o_ref.dtype)


def _compress_prompt(x, wbig, pe_bias):
    B, NC, K = x.shape
    return pl.pallas_call(
        _compress_prompt_kernel, grid=(B,),
        in_specs=[pl.BlockSpec((1, NC, K), lambda b: (b, 0, 0)), _const_spec(wbig.shape), _const_spec(pe_bias.shape)],
        out_specs=pl.BlockSpec((1, NC, 256), lambda b: (b, 0, 0)),
        out_shape=jax.ShapeDtypeStruct((B, NC, 256), BF16), compiler_params=_cp("parallel"),
        name="compress_prompt")(x, wbig, pe_bias)


def _compress_paged_kernel(pt_ref, *refs, pps):
    page_refs = refs[:pps]
    perm_ref, w_ref, peb_ref, o_ref = refs[pps:]
    pairs = []
    for pr in range(pps // 2):
        both = jnp.concatenate([page_refs[2 * pr][0], page_refs[2 * pr + 1][0]], axis=1).astype(BF16)
        pairs.append(_dot_t(perm_ref[...], both))
    n_blk = o_ref.shape[1]
    acc = None
    for l in range(NSA_CMP_BLOCK):
        x = jnp.concatenate([p[8 * l:8 * (l + 1)] for p in pairs], axis=0)
        x = jnp.concatenate([x[:, :LANES], x[:, LANES:]], axis=0).astype(BF16)
        part = _dot(x, w_ref[l])
        acc = part if acc is None else acc + part
    y = jnp.concatenate([acc[:n_blk], acc[n_blk:]], axis=1)
    o_ref[0] = (y + peb_ref[0:1, :]).astype(o_ref.dtype)


def _compress_paged(page_table, cache_t, w_head, pe_bias, pps):
    Bd, n_pages = page_table.shape
    _, feat, page = cache_t.shape
    per_page = page // NSA_CMP_BLOCK
    assert per_page * 2 == 8 and feat == 2 * LANES, "a pair of pages must hold one sublane tile of blocks"
    n_blk = pps * per_page
    r = jnp.arange(2 * page)
    l, p2, n = r // 8, (r // per_page) % 2, r % per_page
    perm = (jnp.arange(2 * page)[None, :] == (p2 * page + n * NSA_CMP_BLOCK + l)[:, None]).astype(BF16)
    page_specs = [pl.BlockSpec((1, feat, page), lambda b, c, pt, k=k: (pt[b, c * pps + k], 0, 0)) for k in range(pps)]
    cst = lambda a: pl.BlockSpec(a.shape, lambda b, c, pt: (0,) * a.ndim)
    return pl.pallas_call(
        functools.partial(_compress_paged_kernel, pps=pps),
        grid_spec=pltpu.PrefetchScalarGridSpec(
            num_scalar_prefetch=1, grid=(Bd, n_pages // pps),
            in_specs=page_specs + [cst(perm), cst(w_head), cst(pe_bias)],
            out_specs=pl.BlockSpec((1, n_blk, feat), lambda b, c, pt: (b, c, 0))),
        out_shape=jax.ShapeDtypeStruct((Bd, n_pages * per_page, feat), BF16),
        compiler_params=_cp("parallel", "parallel"), name="compress_paged")(
            page_table, *([cache_t] * pps), perm, w_head, pe_bias)


def _cmp_select_kernel(q_ref, kv_ref, bias_ref, code_ref, o_ref, sel_ref, *, tq, n_rank, first_pos):
    i = pl.program_id(1)
    G, R = NSA_KV_HEADS, NSA_HEADS // NSA_KV_HEADS
    M = R * tq
    NC = kv_ref.shape[1]
    n_lanes = code_ref.shape[-1]
    code = code_ref[0]
    row_pos = first_pos + i * tq + lax.broadcasted_iota(jnp.int32, (M, 1), 0) % tq
    any_valid = (row_pos >= NSA_CMP_BLOCK - 1).astype(F32)
    lane = lax.broadcasted_iota(jnp.int32, (tq, n_lanes), 1)
    for g in range(G):
        kv = kv_ref[0, :, LANES * g:LANES * (g + 1)]
        s = _dot_t(q_ref[0, g, 0], kv) + bias_ref[0, g]
        m = jnp.max(s, axis=-1, keepdims=True)
        p = jnp.exp(s - m)
        p = p * (any_valid / jnp.sum(p, axis=-1, keepdims=True))
        o_ref[0, g, 0] = _dot(p.astype(BF16), kv).astype(o_ref.dtype)
        imp = p[0:tq]
        for r in range(1, R):
            imp = imp + p[r * tq:(r + 1) * tq]
        imp = imp[:, :NC // 2] + imp[:, NC // 2:]
        if n_lanes > NC // 2:
            imp = jnp.concatenate([imp, jnp.zeros((tq, n_lanes - NC // 2), F32)], axis=1)
        score = jnp.where(code == 0.0, imp, code)
        rank = jnp.zeros((tq, n_lanes), F32)
        for jp in range(n_rank):
            col = score[:, jp:jp + 1]
            ahead = jnp.where(col > score, 1.0, jnp.where(col == score, jnp.where(lane > jp, 1.0, 0.0), 0.0))
            rank = rank + ahead
        sel_neg = jnp.where(rank < float(NSA_TOP_N), jnp.where(code > -0.5 * SELECT_BIG, 0.0, NEG), NEG)
        sel_ref[0, g, 0] = sel_neg.astype(sel_ref.dtype)


def _cmp_select(q_stack, kvc, bias, code, tq, n_rank, first_pos):
    B, G, nq, M, _ = q_stack.shape
    NC = kvc.shape[1]
    n_lanes = code.shape[-1]
    return pl.pallas_call(
        functools.partial(_cmp_select_kernel, tq=tq, n_rank=n_rank, first_pos=first_pos), grid=(B, nq),
        in_specs=[pl.BlockSpec((1, G, 1, M, LANES), lambda b, i: (b, 0, i, 0, 0)),
                  pl.BlockSpec((1, NC, G * LANES), lambda b, i: (b, 0, 0)),
                  pl.BlockSpec((1, G, M, NC), lambda b, i: (i, 0, 0, 0)),
                  pl.BlockSpec((1, tq, n_lanes), lambda b, i: (i, 0, 0))],
        out_specs=[pl.BlockSpec((1, G, 1, M, LANES), lambda b, i: (b, 0, i, 0, 0)),
                   pl.BlockSpec((1, G, 1, tq, n_lanes), lambda b, i: (b, 0, i, 0, 0))],
        out_shape=[jax.ShapeDtypeStruct((B, G, nq, M, LANES), BF16),
                   jax.ShapeDtypeStruct((B, G, nq, tq, n_lanes), BF16)],
        compiler_params=_cp("parallel", "parallel"), name="cmp_select")(q_stack, kvc, bias, code)


def _sel_prompt_kernel(q_ref, sel_ref, kv_ref, oh_ref, bnear_ref, bfar_ref, o_ref, m_sc, acc_sc, *, tq, tk):
    i = pl.program_id(1)
    G = NSA_KV_HEADS
    M = q_ref.shape[3]
    R = M // tq
    far_end = jnp.maximum(i - 1, 0) * tq
    n_full = far_end // tk
    rem = far_end - n_full * tk
    ones = jnp.ones((tk, LANES), BF16)
    m_sc[...] = jnp.full(m_sc.shape, NEG, F32)
    acc_sc[...] = jnp.zeros(acc_sc.shape, F32)

    def scores(start, size):
        kv_all = kv_ref[0, pl.ds(start, size), :]
        onehot = oh_ref[pl.ds(start, size), :]
        out = []
        for g in range(G):
            q = jnp.concatenate([q_ref[0, g, 0], jnp.concatenate([sel_ref[0, g, 0]] * R, axis=0)], axis=1)
            keys = kv_all[:, LANES * g:LANES * (g + 1)]
            out.append((_dot_t(q, jnp.concatenate([keys, onehot], axis=1)), jnp.concatenate([keys, ones[:size]], axis=1)))
        return out

    def body(j, carry):
        for g, (s, v) in enumerate(scores(pl.multiple_of(j * tk, tk), tk)):
            _flash_step([s], [v], m_sc.at[g], acc_sc.at[g])
        return carry

    lax.fori_loop(0, n_full, body, 0)

    @pl.when(rem > 0)
    def _():
        keep = lax.broadcasted_iota(jnp.int32, (M, tk), 1) < rem
        for g, (s, v) in enumerate(scores(pl.multiple_of(n_full * tk, tk), tk)):
            _flash_step([jnp.where(keep, s, NEG)], [v], m_sc.at[g], acc_sc.at[g])

    prev = scores(pl.multiple_of(jnp.maximum(i - 1, 0) * tq, tq), tq)
    diag = scores(pl.multiple_of(i * tq, tq), tq)
    edge = jnp.where(i >= 1, 0.0, NEG)
    for g in range(G):
        far = bfar_ref[g]
        s0 = prev[g][0] + (bnear_ref[g, :, 0:tq] - far) + edge
        s1 = diag[g][0] + (bnear_ref[g, :, tq:2 * tq] - far)
        _flash_step([s0, s1], [prev[g][1], diag[g][1]], m_sc.at[g], acc_sc.at[g])
        acc = acc_sc[g]
        _store_heads(acc[:, :LANES] / acc[:, LANES:], o_ref, g, tq)


def _sel_prompt(q_stack, sel_rows, kv, onehot, bias_near, bias_far, tq, tk):
    B, G, nq, M, _ = q_stack.shape
    T = kv.shape[1]
    return pl.pallas_call(
        functools.partial(_sel_prompt_kernel, tq=tq, tk=tk), grid=(B, nq),
        in_specs=[pl.BlockSpec((1, G, 1, M, LANES), lambda b, i: (b, 0, i, 0, 0)),
                  pl.BlockSpec((1, G, 1, tq, LANES), lambda b, i: (b, 0, i, 0, 0)),
                  pl.BlockSpec((1, T, G * LANES), lambda b, i: (b, 0, 0)),
                  _const_spec(onehot.shape), _const_spec(bias_near.shape), _const_spec(bias_far.shape)],
        out_specs=pl.BlockSpec((1, tq, G * (M // tq) * HEAD_DIM), lambda b, i: (b, i, 0)),
        out_shape=jax.ShapeDtypeStruct((B, nq * tq, G * (M // tq) * HEAD_DIM), BF16),
        scratch_shapes=[pltpu.VMEM((G, M, LANES), F32), pltpu.VMEM((G, M, 2 * LANES), F32)],
        compiler_params=_cp("parallel", "parallel"), name="sel_prompt")(q_stack, sel_rows, kv, onehot, bias_near, bias_far)


def _cmp_select_t_kernel(q_ref, kv_ref, kvt_ref, bias_ref, code_ref, eye_ref, o_ref, sel_ref, *, tq):
    i = pl.program_id(1)
    G, R = NSA_KV_HEADS, NSA_HEADS // NSA_KV_HEADS
    M = R * tq
    NC = kv_ref.shape[1]
    n_sel = code_ref.shape[1]
    code = code_ref[0]
    col_pos = i * tq + lax.broadcasted_iota(jnp.int32, (1, M), 1) % tq
    any_valid = (col_pos >= NSA_CMP_BLOCK - 1).astype(F32)
    blk = lax.broadcasted_iota(jnp.int32, (n_sel, tq), 0)
    for g in range(G):
        kv = kv_ref[0, :, LANES * g:LANES * (g + 1)]
        s = _dot_t(kv, q_ref[0, g, 0]) + bias_ref[0, g]
        p = jnp.exp(s - jnp.max(s, axis=0, keepdims=True))
        p = p * (any_valid / jnp.sum(p, axis=0, keepdims=True))
        o_t = _dot(kvt_ref[0, LANES * g:LANES * (g + 1), :], p.astype(BF16))
        _store_heads(_dot_t(eye_ref[...], o_t.astype(BF16)), o_ref, g, tq)
        imp = p[:, 0:tq]
        for r in range(1, R):
            imp = imp + p[:, r * tq:(r + 1) * tq]
        imp = imp[:NC // 2] + imp[NC // 2:]
        score = jnp.where(code == 0.0, imp, code)
        rank = jnp.zeros((n_sel, tq), F32)
        for jp in range(n_sel):
            row = score[jp:jp + 1, :]
            rank = rank + jnp.where(row > score, 1.0, jnp.where(row == score, jnp.where(blk > jp, 1.0, 0.0), 0.0))
        sel_neg = jnp.where(rank < float(NSA_TOP_N), jnp.where(code > -0.5 * SELECT_BIG, 0.0, NEG), NEG)
        padded = jnp.concatenate([sel_neg.astype(BF16), jnp.zeros((LANES - n_sel, tq), BF16)], axis=0)
        sel_ref[0, g, 0] = _dot_t(eye_ref[0:tq, 0:tq], padded).astype(sel_ref.dtype)


def _cmp_select_t(q_stack, kvc, kvc_t, bias_t, code_t, tq):
    B, G, nq, M, _ = q_stack.shape
    NC = kvc.shape[1]
    n_sel = code_t.shape[1]
    eye = jnp.eye(M, dtype=BF16)
    return pl.pallas_call(
        functools.partial(_cmp_select_t_kernel, tq=tq), grid=(B, nq),
        in_specs=[pl.BlockSpec((1, G, 1, M, LANES), lambda b, i: (b, 0, i, 0, 0)),
                  pl.BlockSpec((1, NC, G * LANES), lambda b, i: (b, 0, 0)),
                  pl.BlockSpec((1, G * LANES, NC), lambda b, i: (b, 0, 0)),
                  pl.BlockSpec((1, G, NC, M), lambda b, i: (i, 0, 0, 0)),
                  pl.BlockSpec((1, n_sel, tq), lambda b, i: (i, 0, 0)), _const_spec(eye.shape)],
        out_specs=[pl.BlockSpec((1, tq, G * (M // tq) * HEAD_DIM), lambda b, i: (b, i, 0)),
                   pl.BlockSpec((1, G, 1, tq, LANES), lambda b, i: (b, 0, i, 0, 0))],
        out_shape=[jax.ShapeDtypeStruct((B, nq * tq, G * (M // tq) * HEAD_DIM), BF16),
                   jax.ShapeDtypeStruct((B, G, nq, tq, LANES), BF16)],
        compiler_params=_cp("parallel", "parallel"), name="cmp_select_t")(q_stack, kvc, kvc_t, bias_t, code_t, eye)


def _banded_kernel(q_ref, kv_ref, bias_ref, sink_ref, o_ref, *, tq, nw, G):
    i = pl.program_id(1)
    ones = jnp.ones((tq, LANES), BF16)
    for g in range(G):
        q = q_ref[0, g, 0]
        scores, vals = [], []
        for jj in range(nw + 1):
            kb = i - nw + jj
            start = pl.multiple_of(jnp.maximum(kb, 0) * tq, tq)
            kv = kv_ref[0, pl.ds(start, tq), LANES * g:LANES * (g + 1)]
            edge = jnp.where(kb >= 0, 0.0, NEG)
            scores.append(_dot_t(q, kv) + bias_ref[g, :, tq * jj:tq * (jj + 1)] + edge)
            vals.append(kv)
        sink = sink_ref[g]
        m = sink
        for s in scores:
            m = jnp.maximum(m, jnp.max(s, axis=-1, keepdims=True))
        acc = None
        for s, v in zip(scores, vals):
            pv = _dot(jnp.exp(s - m).astype(BF16), jnp.concatenate([v, ones], axis=1))
            acc = pv if acc is None else acc + pv
        _store_heads(acc[:, :LANES] / (acc[:, LANES:] + jnp.exp(sink - m)), o_ref, g, tq)


def _banded(q_stack, kv, bias, sinks, tq, nw):
    B, G, nq, M, _ = q_stack.shape
    T = kv.shape[1]
    return pl.pallas_call(
        functools.partial(_banded_kernel, tq=tq, nw=nw, G=G), grid=(B, nq),
        in_specs=[pl.BlockSpec((1, G, 1, M, LANES), lambda b, i: (b, 0, i, 0, 0)),
                  pl.BlockSpec((1, T, G * LANES), lambda b, i: (b, 0, 0)),
                  _const_spec(bias.shape), _const_spec(sinks.shape)],
        out_specs=pl.BlockSpec((1, tq, G * (M // tq) * HEAD_DIM), lambda b, i: (b, i, 0)),
        out_shape=jax.ShapeDtypeStruct((B, nq * tq, G * (M // tq) * HEAD_DIM), BF16),
        compiler_params=_cp("parallel", "parallel"), name="banded")(q_stack, kv, bias, sinks)


def _mla_decode_kernel(pt_ref, q_ref, *refs, pps):
    page_refs = refs[:pps]
    new_ref, nmask_ref, o_ref, m_sc, l_sc, acc_sc = refs[pps:]
    c = pl.program_id(1)

    @pl.when(c == 0)
    def _():
        _softmax_init(m_sc, l_sc, acc_sc)

    q = q_ref[0]
    nv = acc_sc.shape[-1]
    for half in range(N_CHAINS):
        rows = [r[0].astype(BF16) for r in page_refs[half::N_CHAINS]]
        _softmax_update([_dot(q, r) for r in rows], [r[:nv] for r in rows], m_sc.at[half], l_sc.at[half],
                        acc_sc.at[half], transposed_values=True)

    @pl.when(c == pl.num_programs(1) - 1)
    def _():
        nr = new_ref[0].astype(BF16)
        _softmax_update([_dot_t(q, nr) + nmask_ref[...]], [nr[:, :nv]], m_sc.at[0], l_sc.at[0], acc_sc.at[0])
        l, acc = _merge_chains(m_sc, l_sc, acc_sc)
        o_ref[0] = (acc / l).astype(o_ref.dtype)


def _mla_decode(page_table, q_cat, cache_t, new_rows, new_mask, pps):
    Bd, n_pages = page_table.shape
    _, width, page = cache_t.shape
    M = q_cat.shape[1]
    nv = width - MLA_ROPE
    page_specs = [pl.BlockSpec((1, width, page), lambda b, c, pt, k=k: (pt[b, c * pps + k], 0, 0)) for k in range(pps)]
    return pl.pallas_call(
        functools.partial(_mla_decode_kernel, pps=pps),
        grid_spec=pltpu.PrefetchScalarGridSpec(
            num_scalar_prefetch=1, grid=(Bd, n_pages // pps),
            in_specs=[pl.BlockSpec((1, M, width), lambda b, c, pt: (b, 0, 0))] + page_specs + [
                pl.BlockSpec((1,) + new_rows.shape[1:], lambda b, c, pt: (b, 0, 0)),
                pl.BlockSpec(new_mask.shape, lambda b, c, pt: (0, 0))],
            out_specs=pl.BlockSpec((1, M, nv), lambda b, c, pt: (b, 0, 0)),
            scratch_shapes=[pltpu.VMEM((N_CHAINS, M, 1), F32), pltpu.VMEM((N_CHAINS, M, 1), F32),
                            pltpu.VMEM((N_CHAINS, M, nv), F32)]),
        out_shape=jax.ShapeDtypeStruct((Bd, M, nv), BF16),
        compiler_params=_cp("parallel", "arbitrary"), name="mla_decode")(
            page_table, q_cat, *([cache_t] * pps), new_rows, new_mask)


def _sel_decode_kernel(pt_ref, q_ref, oh_ref, blast_ref, bfar_ref, *refs, pps):
    page_refs = refs[:pps]
    new_ref, nbias_ref, o_ref, m_sc, l_sc, acc_sc = refs[pps:]
    c = pl.program_id(1)
    last = c == pl.num_programs(1) - 1
    G = NSA_KV_HEADS

    @pl.when(c == 0)
    def _():
        _softmax_init(m_sc, l_sc, acc_sc)

    M = q_ref.shape[2]
    rows = lambda g: slice(M * g, M * (g + 1))
    for half in range(N_CHAINS):
        scores, vals = [], []
        for k in range(half, pps, N_CHAINS):
            kv_t = [page_refs[k][0, LANES * g:LANES * (g + 1), :].astype(BF16) for g in range(G)]
            onehot = oh_ref[c * pps + k]
            s = jnp.concatenate([_dot(q_ref[0, g], jnp.concatenate([kv_t[g], onehot], axis=0)) for g in range(G)],
                                axis=0)
            if k == pps - 1:
                s = s + jnp.where(last, blast_ref[...] - bfar_ref[...], 0.0)
            scores.append(s)
            vals.append(kv_t)
        m_prev = m_sc[half]
        m_new = m_prev
        for s in scores:
            m_new = jnp.maximum(m_new, jnp.max(s, axis=-1, keepdims=True))
        alpha = jnp.exp(m_prev - m_new)
        l_new = alpha * l_sc[half]
        acc = alpha * acc_sc[half]
        for s, kv_t in zip(scores, vals):
            p = jnp.exp(s - m_new)
            l_new = l_new + jnp.sum(p, axis=-1, keepdims=True)
            pb = p.astype(BF16)
            acc = acc + jnp.concatenate([_dot_t(pb[rows(g)], kv_t[g]) for g in range(G)], axis=0)
        m_sc[half] = m_new
        l_sc[half] = l_new
        acc_sc[half] = acc

    @pl.when(last)
    def _():
        kv = [new_ref[0, :, LANES * g:LANES * (g + 1)].astype(BF16) for g in range(G)]
        n_new = kv[0].shape[0]
        s = jnp.concatenate([_dot_t(q_ref[0, g][:, :LANES], kv[g]) for g in range(G)], axis=0)
        s = s + (nbias_ref[...] - bfar_ref[:, :n_new])
        m_fin = jnp.maximum(m_sc[0], jnp.max(s, axis=-1, keepdims=True))
        a_fin = jnp.exp(m_sc[0] - m_fin)
        p = jnp.exp(s - m_fin)
        pb = p.astype(BF16)
        l_sc[0] = a_fin * l_sc[0] + jnp.sum(p, axis=-1, keepdims=True)
        acc_sc[0] = a_fin * acc_sc[0] + jnp.concatenate([_dot(pb[rows(g)], kv[g]) for g in range(G)], axis=0)
        m_sc[0] = m_fin
        l, acc = _merge_chains(m_sc, l_sc, acc_sc)
        out = acc / l
        for g in range(G):
            o_ref[0, g] = out[rows(g)].astype(o_ref.dtype)


def _sel_decode(page_table, q_aug, onehot_t, bias_last, bias_far, cache_t, new_rows, new_bias, pps):
    Bd, n_pages = page_table.shape
    _, feat, page = cache_t.shape
    _, G, M, _ = q_aug.shape
    page_specs = [pl.BlockSpec((1, feat, page), lambda b, c, pt, k=k: (pt[b, c * pps + k], 0, 0)) for k in range(pps)]
    cst = lambda a: pl.BlockSpec(a.shape, lambda b, c, pt: (0,) * a.ndim)
    per_b = lambda a: pl.BlockSpec((1,) + a.shape[1:], lambda b, c, pt: (b,) + (0,) * (a.ndim - 1))
    return pl.pallas_call(
        functools.partial(_sel_decode_kernel, pps=pps),
        grid_spec=pltpu.PrefetchScalarGridSpec(
            num_scalar_prefetch=1, grid=(Bd, n_pages // pps),
            in_specs=[per_b(q_aug), cst(onehot_t), cst(bias_last), cst(bias_far)] + page_specs
            + [per_b(new_rows), cst(new_bias)],
            out_specs=pl.BlockSpec((1, G, M, LANES), lambda b, c, pt: (b, 0, 0, 0)),
            scratch_shapes=[pltpu.VMEM((N_CHAINS, G * M, 1), F32), pltpu.VMEM((N_CHAINS, G * M, 1), F32),
                            pltpu.VMEM((N_CHAINS, G * M, LANES), F32)]),
        out_shape=jax.ShapeDtypeStruct((Bd, G, M, LANES), BF16),
        compiler_params=_cp("parallel", "arbitrary"), name="sel_decode")(
            page_table, q_aug, onehot_t, bias_last, bias_far, *([cache_t] * pps), new_rows, new_bias)


def _win_decode_kernel(q_ref, buf_ref, new_ref, bias_ref, nbias_ref, sink_ref, o_ref, *, G, bb):
    def one_sequence(j, carry):
        for g in range(G):
            q = q_ref[j, g]
            kb = buf_ref[j, LANES * g:LANES * (g + 1), :].astype(BF16)
            kn = new_ref[j, :, LANES * g:LANES * (g + 1)].astype(BF16)
            sb = _dot(q, kb) + bias_ref[g]
            sn = _dot_t(q, kn) + nbias_ref[g]
            sink = sink_ref[g]
            m = jnp.maximum(jnp.maximum(jnp.max(sb, axis=-1, keepdims=True), jnp.max(sn, axis=-1, keepdims=True)), sink)
            pb = jnp.exp(sb - m)
            pn = jnp.exp(sn - m)
            l = jnp.exp(sink - m) + jnp.sum(pb, axis=-1, keepdims=True) + jnp.sum(pn, axis=-1, keepdims=True)
            acc = _dot_t(pb.astype(BF16), kb) + _dot(pn.astype(BF16), kn)
            o_ref[j, g] = (acc / l).astype(o_ref.dtype)
        return carry

    lax.fori_loop(0, bb, one_sequence, 0)


def _win_decode(q_stack, buf, new_rows, bias, new_bias, sinks):
    Bd, G, M, _ = q_stack.shape
    bb = 4 if Bd % 4 == 0 else 1
    per_b = lambda a: pl.BlockSpec((bb,) + a.shape[1:], lambda b: (b,) + (0,) * (a.ndim - 1))
    return pl.pallas_call(
        functools.partial(_win_decode_kernel, G=G, bb=bb), grid=(Bd // bb,),
        in_specs=[per_b(q_stack), per_b(buf), per_b(new_rows), _const_spec(bias.shape), _const_spec(new_bias.shape),
                  _const_spec(sinks.shape)],
        out_specs=pl.BlockSpec((bb, G, M, LANES), lambda b: (b, 0, 0, 0)),
        out_shape=jax.ShapeDtypeStruct((Bd, G, M, LANES), BF16),
        compiler_params=_cp("parallel"), name="win_decode")(q_stack, buf, new_rows, bias, new_bias, sinks)


def _ffn_chunk(f):
    for cand in range(min(f, 1536) // LANES, 0, -1):
        if f % (cand * LANES) == 0:
            return cand * LANES
    return f


def _out_ffn_kernel(*refs, mode, final, n_mods):
    it = iter(refs)
    y_ref = next(it)
    if mode == "c":
        attn_in = next(it)[0]
    else:
        oa_ref, ocmp_ref, osel_ref, owin_ref, gate_ref, eg_ref = (next(it) for _ in range(6))
        if mode == "ab_sample":
            wuv_ref = next(it)
    wout_ref = next(it)
    gt_ref, fsh_ref, fsc_ref, fgt_ref = (next(it) for _ in range(4))
    wg_ref, wu_ref, wo_ref = (next(it) for _ in range(3))
    gain_ref = next(it) if final else None
    o_ref = next(it)

    if mode == "c":
        attn = _dot(attn_in, wout_ref[...])
    else:
        gates = gate_ref[0]
        g_hi = gates.astype(BF16)
        g_lo = (gates - g_hi.astype(F32)).astype(BF16)
        o_b = None
        for br, ref in enumerate((ocmp_ref, osel_ref, owin_ref)):
            ge = _dot(g_hi, eg_ref[br]) + _dot(g_lo, eg_ref[br])
            term = ge * ref[0].astype(F32)
            o_b = term if o_b is None else o_b + term
        o_a = oa_ref[0]
        if mode == "ab_sample":
            o_a = _dot(o_a, wuv_ref[...]).astype(BF16)
        na = o_a.shape[-1]
        attn = _dot(o_a, wout_ref[0:na, :]) + _dot(o_b.astype(BF16), wout_ref[na:, :])
    y1 = y_ref[0] + gt_ref[0] * attn
    h = _modulate(y1, fsh_ref[0], fsc_ref[0]).astype(BF16)
    f = wg_ref.shape[1]
    fc = _ffn_chunk(f)
    acc = None
    for k in range(f // fc):
        g = _dot(h, wg_ref[:, fc * k:fc * (k + 1)])
        u = _dot(h, wu_ref[:, fc * k:fc * (k + 1)])
        a = (g * _sigmoid(g) * u).astype(BF16)
        part = _dot(a, wo_ref[fc * k:fc * (k + 1), :])
        acc = part if acc is None else acc + part
    y2 = y1 + fgt_ref[0] * acc
    if final:
        y2 = _rms(y2) * gain_ref[...]
    o_ref[0] = y2


def _out_ffn(y, attn_parts, weights, mods, ffn_w, gain, mode, tm):
    Bx, Tx, D = y.shape
    per_token = mods[0].shape[1] != 1
    mod_spec = (pl.BlockSpec((1, tm, D), lambda b, i: (b, i, 0)) if per_token
                else pl.BlockSpec((1, 1, D), lambda b, i: (b, 0, 0)))
    tok = lambda a: pl.BlockSpec((1, tm, a.shape[-1]), lambda b, i: (b, i, 0))
    single = lambda a: pl.BlockSpec(a.shape, lambda b, i: (0,) * a.ndim, pipeline_mode=pl.Buffered(1))
    args = [y] + list(attn_parts) + list(weights) + list(mods) + list(ffn_w)
    in_specs = ([tok(y)] + [tok(a) for a in attn_parts] + [single(a) for a in weights] + [mod_spec] * 4
                + [single(a) for a in ffn_w])
    final = gain is not None
    if final:
        args.append(gain)
        in_specs.append(single(gain))
    return pl.pallas_call(
        functools.partial(_out_ffn_kernel, mode=mode, final=final, n_mods=4), grid=(Bx, Tx // tm),
        in_specs=in_specs, out_specs=tok(y), out_shape=jax.ShapeDtypeStruct(y.shape, F32),
        compiler_params=_cp("parallel", "parallel"), name="out_ffn_" + mode)(*args)


def _stack_heads(q, G, R, tq):
    B, T, _ = q.shape
    q = q.reshape(B, T // tq, tq, G, R, HEAD_DIM).transpose(0, 3, 1, 4, 2, 5).reshape(B, G, T // tq, R * tq, HEAD_DIM)
    return jnp.pad(q, ((0, 0),) * 4 + ((0, LANES - HEAD_DIM),))


def _unstack_heads(o, R, tq):
    B, G, nq, _, _ = o.shape
    o = o[..., HEAD_DIM:].reshape(B, G, nq, R, tq, HEAD_DIM).transpose(0, 2, 4, 1, 3, 5)
    return o.reshape(B, nq * tq, G * R * HEAD_DIM)


def _stack_heads_sample(q, G, R, S, s_pad):
    Bd = q.shape[0] // S
    q = q.reshape(Bd, S, G, R, HEAD_DIM).transpose(0, 2, 3, 1, 4)
    q = jnp.pad(q, ((0, 0), (0, 0), (0, 0), (0, s_pad - S), (0, LANES - HEAD_DIM)))
    return q.reshape(Bd, G, R * s_pad, LANES)


def _unstack_heads_sample(o, R, S, s_pad):
    Bd, G, _, _ = o.shape
    o = o[..., HEAD_DIM:].reshape(Bd, G, R, s_pad, HEAD_DIM)[:, :, :, :S].transpose(0, 3, 1, 2, 4)
    return o.reshape(1, Bd * S, G * R * HEAD_DIM)


def _group_rows(t, G):
    H, rows, C = t.shape
    return t.reshape(G, (H // G) * rows, C)


def _even_odd(n):
    return jnp.concatenate([jnp.arange(0, n, 2), jnp.arange(1, n, 2)]).astype(jnp.int32)


def _rope_tables(pos):
    half = MLA_ROPE // 2
    freq = ROPE_THETA ** (-jnp.arange(half, dtype=F32) / half)
    ang = pos.astype(F32)[:, None] * freq[None, :]
    cos, sin = jnp.cos(ang), jnp.sin(ang)
    n = pos.shape[0]
    one, zero = jnp.ones, jnp.zeros
    cq = jnp.concatenate([one((n, MLA_NOPE), F32), cos, cos, zero((n, LANES - MLA_NOPE - MLA_ROPE), F32)], axis=1)
    sq = jnp.concatenate([zero((n, MLA_NOPE), F32), sin, sin, zero((n, LANES - MLA_NOPE - MLA_ROPE), F32)], axis=1)
    n_gate = 3 * NSA_HEADS
    cm = jnp.concatenate([cos, cos, one((n, n_gate), F32), zero((n, LANES - MLA_ROPE - n_gate), F32)], axis=1)
    sm = jnp.concatenate([sin, sin, zero((n, LANES - MLA_ROPE), F32)], axis=1)
    return cq, sq, cm, sm


def _layer0_weights(w_in_0, mla_q_norm, mla_w_uq, mla_kv_norm, mla_w_uk, mla_w_uv):
    D = w_in_0.shape[0]
    qr, kvr = mla_q_norm.shape[0], mla_kv_norm.shape[0]
    half = MLA_ROPE // 2
    o_kr = qr + kvr
    o_q = o_kr + MLA_ROPE
    o_cmp = o_q + NSA_HEADS * HEAD_DIM
    kvw = NSA_KV_HEADS * 2 * HEAD_DIM
    o_g = o_cmp + 3 * kvw
    w_kr = w_in_0[:, o_kr:o_q]
    w_g = w_in_0[:, o_g:]
    z = lambda n: jnp.zeros((D, n), F32)
    misc_a = jnp.concatenate([w_kr, w_g, z(LANES - MLA_ROPE - w_g.shape[1])], axis=1)
    misc_b = jnp.concatenate([-w_kr[:, half:], w_kr[:, :half], z(LANES - MLA_ROPE)], axis=1)
    w0 = jnp.concatenate([w_in_0[:, :o_kr], misc_a, misc_b, w_in_0[:, o_q:o_cmp] * ATTN_SCALE, w_in_0[:, o_cmp:o_g]],
                         axis=1).astype(BF16)
    H = MLA_HEADS
    wq = mla_w_uq.reshape(qr, H, MLA_NOPE + MLA_ROPE)
    nope, x1, x2 = wq[..., :MLA_NOPE], wq[..., MLA_NOPE:MLA_NOPE + half], wq[..., MLA_NOPE + half:]
    zq = lambda n: jnp.zeros((qr, H, n), F32)
    pad = LANES - MLA_NOPE - MLA_ROPE
    wuq = jnp.concatenate([nope, x1, x2, zq(pad)], axis=-1).reshape(qr, H * LANES).astype(BF16)
    wuqs = jnp.concatenate([zq(MLA_NOPE), -x2, x1, zq(pad)], axis=-1).reshape(qr, H * LANES).astype(BF16)
    k_top = jnp.pad(mla_w_uk, ((0, 0), (0, 0), (0, LANES - MLA_NOPE))).reshape(kvr, H * LANES)
    place = jnp.zeros((LANES, H, LANES), F32).at[jnp.arange(MLA_ROPE), :, MLA_NOPE + jnp.arange(MLA_ROPE)].set(1.0)
    kcat = jnp.concatenate([k_top, place.reshape(LANES, H * LANES)], axis=0).astype(BF16)
    even = (jnp.arange(H) % 2 == 0)[None, :, None]
    zv = jnp.zeros((kvr, H, LANES - MLA_V), F32)
    wuv = jnp.where(even, jnp.concatenate([mla_w_uv, zv], axis=-1), jnp.concatenate([zv, mla_w_uv], axis=-1))
    wuv = wuv.reshape(kvr, H * LANES).astype(BF16)
    ones_v, zero_v = jnp.ones((1, H, MLA_V), F32), jnp.zeros((1, H, MLA_V), F32)
    vones = jnp.where(even, jnp.concatenate([zero_v, ones_v], axis=-1), jnp.concatenate([ones_v, zero_v], axis=-1))
    vones = vones.reshape(1, H * LANES)
    width = kvr + MLA_ROPE
    blk = jnp.zeros((H, LANES, width), F32)
    blk = blk.at[:, :MLA_NOPE, :kvr].set(jnp.transpose(mla_w_uk, (1, 2, 0)))
    blk = blk.at[:, MLA_NOPE + jnp.arange(MLA_ROPE), kvr + jnp.arange(MLA_ROPE)].set(1.0)
    eye = jnp.eye(H, dtype=F32)
    a_abs = jnp.einsum("hij,hk->hikj", blk, eye).reshape(H * LANES, H * width).astype(BF16)
    wuv_bd = jnp.einsum("chd,hk->hckd", mla_w_uv, eye).reshape(H * kvr, H * MLA_V).astype(BF16)
    return dict(w0=w0, qn=mla_q_norm.reshape(1, qr), kvn=mla_kv_norm.reshape(1, kvr), wuq=wuq, wuqs=wuqs, kcat=kcat,
                wuv=wuv, vones=vones, a_abs=a_abs, wuv_bd=wuv_bd)


def _gate_expand():
    h = jnp.arange(NSA_HEADS)
    mats = []
    for br in range(3):
        m = jnp.zeros((LANES, NSA_HEADS, HEAD_DIM), F32).at[MLA_ROPE + 3 * h + br, h, :].set(1.0)
        mats.append(m.reshape(LANES, NSA_HEADS * HEAD_DIM))
    return jnp.stack(mats).astype(BF16)


def kernel(x_prompt, x_sample, cache_mla, cache_nsa_cmp, cache_nsa_sel, state_nsa_win, state_swa, page_table, c_prompt, c_sample, rel_bias_table, w_ada_0, b_ada_0, w_in_0, mla_q_norm, mla_w_uq, mla_kv_norm, mla_w_uk, mla_w_uv, nsa_w_cmp, nsa_pe_cmp, w_out_0, w_ffn_in_0, w_ffn_out_0, w_ada_1, b_ada_1, w_in_1, swa_sinks, w_out_1, w_ffn_in_1, w_ffn_out_1, final_norm):
    B, T, D = x_prompt.shape
    Bd, S, _ = x_sample.shape
    n_pool, PAGE, mla_w = cache_mla.shape
    n_pages = page_table.shape[1]
    PAST = n_pages * PAGE
    G, R = NSA_KV_HEADS, NSA_HEADS // NSA_KV_HEADS
    G1, R1 = SWA_KV_HEADS, SWA_HEADS // SWA_KV_HEADS
    tq = Q_BLOCK
    nq = T // tq
    NS = Bd * S
    S_PAD = 8
    tm_p = 512 if T % 512 == 0 else 256
    tm_s = 256 if NS % 256 == 0 else NS
    i32 = jnp.int32
    table = rel_bias_table.astype(F32)

    n_c = B + Bd
    c_all = jnp.pad(jnp.concatenate([c_prompt, c_sample], axis=0), ((0, (-n_c) % 8), (0, 0)))

    def mods_for(w_ada, b_ada):
        m = _ada(c_all, w_ada.astype(BF16), b_ada.reshape(1, -1))
        mp = [m[:B, k * D:(k + 1) * D][:, None, :] for k in range(6)]
        ms = [jnp.repeat(m[B:B + Bd, k * D:(k + 1) * D], S, axis=0)[None] for k in range(6)]
        return mp, ms

    mods0_p, mods0_s = mods_for(w_ada_0, b_ada_0)
    mods1_p, mods1_s = mods_for(w_ada_1, b_ada_1)
    xs = x_sample.reshape(1, NS, D)

    w0 = _layer0_weights(w_in_0, mla_q_norm, mla_w_uq, mla_kv_norm, mla_w_uk, mla_w_uv)
    tabs_p = _rope_tables(jnp.arange(T, dtype=i32))
    tabs_s = tuple(jnp.tile(t, (Bd, 1)) for t in _rope_tables(PAST + jnp.arange(S, dtype=i32)))
    (mla_p, q_mla, k_mla, v_mla, q_stack_p, cmp_p, sel_p, win_p, selb_p, winb_p, gate_p) = _proj0(
        x_prompt, mods0_p[0], mods0_p[1], w0, tabs_p, False, tm_p)
    (mla_s, qcat_s, qn_s, cmp_s, sel_s, win_s, gate_s) = _proj0(xs, mods0_s[0], mods0_s[1], w0, tabs_s, True, tm_s)

    o_a_p = _mla_prompt(q_mla, k_mla, v_mla, 512 if T % 512 == 0 else 256)

    eye2 = jnp.eye(2, dtype=F32)
    wbig = jnp.einsum("lcde,gh,ck->lgcdhke", nsa_w_cmp, eye2, eye2).reshape(NSA_CMP_BLOCK * 4 * HEAD_DIM, 4 * HEAD_DIM)
    wbig = wbig.astype(BF16)
    pe_rows = jnp.broadcast_to(nsa_pe_cmp[:, None], (NSA_CMP_BLOCK, G, 2, HEAD_DIM)).reshape(1, -1)
    pe_rows = jnp.broadcast_to(pe_rows, (8, pe_rows.shape[1]))
    pe_bias = _pe_bias(pe_rows, wbig)
    NC = T // NSA_CMP_BLOCK
    kvc_p = _compress_prompt(cmp_p.reshape(B, NC, -1), wbig, pe_bias)
    order_p = _even_odd(NC)
    kvc_p = kvc_p[:, order_p]

    qpos = jnp.arange(T, dtype=i32)
    dist = qpos[:, None] - (order_p * NSA_CMP_BLOCK + NSA_CMP_BLOCK - 1)[None, :]
    bias_cmp_p = _bias_tiles(table, _masked_bucket(dist, dist >= 0), NSA_HEADS)
    bias_cmp_p = bias_cmp_p.reshape(G, R, nq, tq, NC).transpose(2, 0, 1, 3, 4).reshape(nq, G, R * tq, NC)
    n_sel = T // NSA_SEL_BLOCK
    blk = jnp.arange(n_sel, dtype=i32)[None, :]
    cur = (qpos // NSA_SEL_BLOCK)[:, None]
    forced = (blk == 0) | (blk == cur) | (blk == cur - 1)
    causal = blk * NSA_SEL_BLOCK <= qpos[:, None]
    code_p = jnp.where(causal, jnp.where(forced, SELECT_BIG, 0.0), -SELECT_BIG).astype(F32).reshape(nq, tq, n_sel)
    assert n_sel <= LANES and n_sel == NC // 2, "selection blocks must fit one lane tile"
    o_cmp_p, sel_rows_p = _cmp_select_t(q_stack_p, kvc_p, kvc_p.transpose(0, 2, 1), bias_cmp_p.transpose(0, 1, 3, 2),
                                        code_p.transpose(0, 2, 1), tq)

    onehot_p = (jnp.arange(LANES, dtype=i32)[None, :] == (qpos // NSA_SEL_BLOCK)[:, None]).astype(BF16)
    ql = jnp.arange(tq, dtype=i32)[:, None]
    d_near = ql + tq - jnp.arange(2 * tq, dtype=i32)[None, :]
    bk_near = jnp.concatenate([_masked_bucket(d_near, d_near >= 0), jnp.full((tq, 2 * tq), REL_BUCKETS - 1, i32)], axis=0)
    near_far = _bias_tiles(table, bk_near, NSA_HEADS)
    bias_near_p = _group_rows(near_far[:, :tq], G)
    bias_far_p = _group_rows(near_far[:, tq:, :LANES], G)
    tk_sel = 512 if T % 512 == 0 else tq
    o_sel_st = _sel_prompt(q_stack_p, sel_rows_p, selb_p, onehot_p, bias_near_p, bias_far_p, tq, tk_sel)

    def window_bias(window, n_heads, n_groups):
        nw = -(-window // tq)
        dw = ql + nw * tq - jnp.arange((nw + 1) * tq, dtype=i32)[None, :]
        t = _bias_tiles(table, _masked_bucket(dw, (dw >= 0) & (dw < window)), n_heads)
        return _group_rows(t, n_groups), nw

    bias_win_p, nw0 = window_bias(NSA_WINDOW, NSA_HEADS, G)
    no_sink0 = jnp.full((G, R * tq, 1), NEG, F32)
    o_win_st = _banded(q_stack_p, winb_p, bias_win_p, no_sink0, tq, nw0)

    o_sel_p, o_win_p = o_sel_st, o_win_st

    pps = 8 if n_pages % 8 == 0 else n_pages
    srow = jnp.arange(S_PAD, dtype=i32)
    s_real = jnp.minimum(srow, S - 1)
    q_cat = qcat_s.reshape(Bd, S * MLA_HEADS, mla_w)
    mla_new = jnp.pad(mla_s.reshape(Bd, S, mla_w), ((0, 0), (0, S_PAD - S), (0, 0)))
    s_of_row = jnp.repeat(jnp.arange(S, dtype=i32), MLA_HEADS)[:, None]
    new_mask = jnp.where((srow[None, :] <= s_of_row) & (srow[None, :] < S), 0.0, NEG).astype(F32)
    pps_m = 16 if n_pages % 16 == 0 else pps
    o_full_s = _mla_decode(page_table, q_cat, cache_mla.transpose(0, 2, 1), mla_new, new_mask, pps_m)
    o_full_s = o_full_s.reshape(1, NS, MLA_HEADS * (mla_w - MLA_ROPE))

    pps_c = 32 if n_pages % 32 == 0 else pps_m
    feat = G * 2 * HEAD_DIM
    cache_cmp_t = cache_nsa_cmp.transpose(0, 2, 3, 4, 1).reshape(n_pool, feat, PAGE)
    w_head = wbig.reshape(NSA_CMP_BLOCK, feat, feat)[:, :LANES, :LANES]
    kvc_s = _compress_paged(page_table, cache_cmp_t, w_head, pe_bias, pps_c)
    NCs = PAST // NSA_CMP_BLOCK
    order_s = _even_odd(NCs)
    kvc_s = kvc_s[:, order_s]
    q_stack_s = _stack_heads_sample(qn_s[0], G, R, S, S_PAD)
    pos_s = PAST + s_real
    dist_s = pos_s[:, None] - (order_s * NSA_CMP_BLOCK + NSA_CMP_BLOCK - 1)[None, :]
    bias_cmp_s = _group_rows(_bias_tiles(table, _masked_bucket(dist_s, dist_s >= 0), NSA_HEADS), G)[None]
    n_past_blk = PAST // NSA_SEL_BLOCK
    n_sel_s = n_past_blk + -(-S // NSA_SEL_BLOCK)
    sel_lanes = -(-n_sel_s // LANES) * LANES
    blk_s = jnp.arange(sel_lanes, dtype=i32)[None, :]
    cur_s = (pos_s // NSA_SEL_BLOCK)[:, None]
    forced_s = (blk_s == 0) | (blk_s == cur_s) | (blk_s == cur_s - 1)
    causal_s = (blk_s * NSA_SEL_BLOCK <= pos_s[:, None]) & (blk_s < n_sel_s)
    code_s = jnp.where(causal_s, jnp.where(forced_s, SELECT_BIG, 0.0), -SELECT_BIG).astype(F32)[None]
    o_cmp_ss, sel_mask_s = _cmp_select(q_stack_s[:, :, None], kvc_s, bias_cmp_s, code_s, S_PAD, n_sel_s, PAST)
    o_cmp_s = _unstack_heads_sample(o_cmp_ss[:, :, 0], R, S, S_PAD)

    assert n_past_blk <= LANES and PAGE >= REL_MAX_DISTANCE, "past selection blocks must fit one lane tile"
    sel_past = jnp.pad(sel_mask_s[:, :, 0, :, :n_past_blk], ((0, 0),) * 3 + ((0, LANES - n_past_blk),))
    q_aug_s = jnp.concatenate([q_stack_s, jnp.tile(sel_past, (1, 1, R, 1))], axis=-1)
    key_blk_s = (jnp.arange(PAST, dtype=i32) // NSA_SEL_BLOCK).reshape(n_pages, 1, PAGE)
    onehot_s = (jnp.arange(LANES, dtype=i32)[None, :, None] == key_blk_s).astype(BF16)
    d_last = pos_s[:, None] - (PAST - PAGE + jnp.arange(PAGE, dtype=i32))[None, :]
    bk_last = jnp.concatenate([_masked_bucket(d_last, d_last >= 0), jnp.full((S_PAD, PAGE), REL_BUCKETS - 1, i32)], axis=0)
    last_far = _bias_tiles(table, bk_last, NSA_HEADS)
    bias_last_s = _group_rows(last_far[:, :S_PAD], G)
    bias_far_s = _group_rows(last_far[:, S_PAD:], G)
    d_new = s_real[:, None] - srow[None, :]
    bk_new = _masked_bucket(d_new, (d_new >= 0) & (srow[None, :] < S))
    nbias_nsa = _group_rows(_bias_tiles(table, bk_new, NSA_HEADS), G)
    sel_new8 = jnp.pad(sel_s.reshape(Bd, S, -1), ((0, 0), (0, S_PAD - S), (0, 0)))
    cache_sel_t = cache_nsa_sel.transpose(0, 2, 3, 4, 1).reshape(n_pool, feat, PAGE)
    flat = lambda a: a.reshape(-1, a.shape[-1])
    o_sel_ss = _sel_decode(page_table, q_aug_s, onehot_s, flat(bias_last_s), flat(bias_far_s), cache_sel_t, sel_new8,
                           flat(nbias_nsa), pps_m)
    o_sel_s = _unstack_heads_sample(o_sel_ss, R, S, S_PAD)

    def state_bias(wb, window, n_heads, n_groups):
        d = wb + s_real[:, None] - jnp.arange(wb, dtype=i32)[None, :]
        return _group_rows(_bias_tiles(table, _masked_bucket(d, d < window), n_heads), n_groups)

    wb0 = state_nsa_win.shape[1]
    win_new8 = jnp.pad(win_s.reshape(Bd, S, -1), ((0, 0), (0, S_PAD - S), (0, 0)))
    no_sink0_s = jnp.full((G, R * S_PAD, 1), NEG, F32)
    o_win_ss = _win_decode(q_stack_s, state_nsa_win.transpose(0, 2, 3, 4, 1).reshape(Bd, feat, wb0), win_new8,
                           state_bias(wb0, NSA_WINDOW, NSA_HEADS, G), nbias_nsa, no_sink0_s)
    o_win_s = _unstack_heads_sample(o_win_ss, R, S, S_PAD)

    eg = _gate_expand()
    w_out0 = w_out_0.astype(BF16)
    f = w_ffn_out_0.shape[0]
    ffn0 = (w_ffn_in_0[:, :f].astype(BF16), w_ffn_in_0[:, f:].astype(BF16), w_ffn_out_0.astype(BF16))
    y1_p = _out_ffn(x_prompt, [o_a_p, o_cmp_p, o_sel_p, o_win_p, gate_p], [eg, w_out0], mods0_p[2:], ffn0, None,
                    "ab_prompt", tm_p)
    y1_s = _out_ffn(xs, [o_full_s, o_cmp_s, o_sel_s, o_win_s, gate_s], [eg, w0["wuv_bd"], w_out0], mods0_s[2:], ffn0,
                    None, "ab_sample", tm_s)

    nq1 = SWA_HEADS * HEAD_DIM
    w1 = jnp.concatenate([w_in_1[:, :nq1] * ATTN_SCALE, w_in_1[:, nq1:]], axis=1).astype(BF16)
    q1_p, kv1_p, kvb1_p = _proj1(y1_p, mods1_p[0], mods1_p[1], w1, nq1, tm_p, True)
    q1_s, kv1_s, _ = _proj1(y1_s, mods1_s[0], mods1_s[1], w1, nq1, tm_s, False)
    bias_swa_p, nw1 = window_bias(SWA_WINDOW, SWA_HEADS, G1)
    sink_p = jnp.repeat(swa_sinks.astype(F32).reshape(G1, R1), tq, axis=1).reshape(G1, R1 * tq, 1)
    o_c_p = _banded(q1_p, kvb1_p, bias_swa_p, sink_p, tq, nw1)

    wb1 = state_swa.shape[1]
    kv_new8 = jnp.pad(kv1_s.reshape(Bd, S, -1), ((0, 0), (0, S_PAD - S), (0, 0)))
    d_new1 = s_real[:, None] - srow[None, :]
    nbias_swa = _group_rows(_bias_tiles(table, _masked_bucket(d_new1, (d_new1 >= 0) & (srow[None, :] < S)), SWA_HEADS),
                            G1)
    sink_s = jnp.repeat(swa_sinks.astype(F32).reshape(G1, R1), S_PAD, axis=1).reshape(G1, R1 * S_PAD, 1)
    state_swa_t = state_swa.transpose(0, 2, 3, 4, 1).reshape(Bd, G1 * 2 * HEAD_DIM, wb1)
    o_c_ss = _win_decode(_stack_heads_sample(q1_s[0], G1, R1, S, S_PAD), state_swa_t, kv_new8,
                         state_bias(wb1, SWA_WINDOW, SWA_HEADS, G1), nbias_swa, sink_s)
    o_c_s = _unstack_heads_sample(o_c_ss, R1, S, S_PAD)

    w_out1 = w_out_1.astype(BF16)
    f1 = w_ffn_out_1.shape[0]
    ffn1 = (w_ffn_in_1[:, :f1].astype(BF16), w_ffn_in_1[:, f1:].astype(BF16), w_ffn_out_1.astype(BF16))
    gain = final_norm.reshape(1, D).astype(F32)
    y_prompt = _out_ffn(y1_p, [o_c_p], [w_out1], mods1_p[2:], ffn1, gain, "c", tm_p)
    y_sample = _out_ffn(y1_s, [o_c_s], [w_out1], mods1_s[2:], ffn1, gain, "c", tm_s).reshape(Bd, S, D)

    row5 = lambda a, lead, g: a.reshape(lead + (g, 2, HEAD_DIM))
    win_p5 = row5(win_p, (B, T), G)
    kv1_p5 = row5(kv1_p, (B, T), G1)
    win_s5 = row5(win_s, (Bd, S), G)
    kv1_s5 = row5(kv1_s, (Bd, S), G1)
    return (y_prompt, y_sample, mla_p, mla_s.reshape(Bd, S, mla_w),
            row5(cmp_p, (B, T), G), row5(cmp_s, (Bd, S), G), row5(sel_p, (B, T), G), row5(sel_s, (Bd, S), G),
            win_p5[:, T - min(NSA_WINDOW, T):], jnp.concatenate([state_nsa_win, win_s5], axis=1)[:, S:],
            kv1_p5[:, T - min(SWA_WINDOW, T):], jnp.concatenate([state_swa, kv1_s5], axis=1)[:, S:])
```

```python
import functools
import math

import jax
import jax.numpy as jnp
import numpy as np
from jax import lax
from jax.experimental import pallas as pl
from jax.experimental.pallas import tpu as pltpu

F32 = jnp.float32
BF16 = jnp.bfloat16

MLA_HEADS, MLA_NOPE, MLA_ROPE, MLA_V = 8, 64, 32, 64
ROPE_THETA = 10000.0
NSA_HEADS, NSA_KV_HEADS, HEAD_DIM = 8, 2, 64
NSA_CMP_BLOCK, NSA_SEL_BLOCK, NSA_TOP_N, NSA_WINDOW = 32, 64, 16, 512
SWA_HEADS, SWA_KV_HEADS, SWA_WINDOW = 16, 4, 128
REL_BUCKETS, REL_MAX_DISTANCE = 32, 128
Q_BLOCK = 128
NORM_EPS = 1e-6
NEG = -1e30
SELECT_BIG = 1e9
MLA_SCALE = (MLA_NOPE + MLA_ROPE) ** -0.5
ATTN_SCALE = HEAD_DIM ** -0.5

LANES = 128
VMEM_LIMIT = 52 * 1024 * 1024


def _cp(*sem):
    return pltpu.CompilerParams(dimension_semantics=sem, vmem_limit_bytes=VMEM_LIMIT)


def _dot(a, b):
    return jnp.dot(a, b, preferred_element_type=F32)


def _dot_t(a, b):
    return lax.dot_general(a, b, (((1,), (1,)), ((), ())), preferred_element_type=F32)


def _rms(x):
    return x * lax.rsqrt(jnp.mean(x * x, axis=-1, keepdims=True) + NORM_EPS)


def _sigmoid(x):
    return 1.0 / (1.0 + jnp.exp(-x))


def _const_spec(shape):
    n = len(shape)
    return pl.BlockSpec(shape, lambda *_: (0,) * n)


def _softmax_update(scores, values, m_ref, l_ref, acc_ref, transposed_values=False):
    m_prev = m_ref[...]
    m_new = m_prev
    for s in scores:
        m_new = jnp.maximum(m_new, jnp.max(s, axis=-1, keepdims=True))
    alpha = jnp.exp(m_prev - m_new)
    l_new = alpha * l_ref[...]
    acc = alpha * acc_ref[...]
    for s, v in zip(scores, values):
        p = jnp.exp(s - m_new)
        l_new = l_new + jnp.sum(p, axis=-1, keepdims=True)
        pb = p.astype(BF16)
        acc = acc + (_dot_t(pb, v) if transposed_values else _dot(pb, v))
    m_ref[...] = m_new
    l_ref[...] = l_new
    acc_ref[...] = acc


def _lane_tile(x, width):
    reps = width // x.shape[-1]
    return x if reps == 1 else jnp.concatenate([x] * reps, axis=1)


def _flash_step(scores, values, m_ref, acc_ref):
    m_prev = m_ref[...]
    m_cur = None
    for s in scores:
        mx = jnp.max(s, axis=-1, keepdims=True)
        m_cur = mx if m_cur is None else jnp.maximum(m_cur, mx)
    m_new = jnp.maximum(m_prev, m_cur)
    acc = acc_ref[...] * _lane_tile(jnp.exp(m_prev - m_new), acc_ref.shape[-1])
    for s, v in zip(scores, values):
        p = jnp.exp(s - _lane_tile(m_new, s.shape[-1]))
        acc = acc + _dot(p.astype(BF16), v)
    m_ref[...] = m_new
    acc_ref[...] = acc


N_CHAINS = 1


def _merge_chains(m_ref, l_ref, acc_ref):
    m = m_ref[0]
    for k in range(1, m_ref.shape[0]):
        m = jnp.maximum(m, m_ref[k])
    l = acc = None
    for k in range(m_ref.shape[0]):
        w = jnp.exp(m_ref[k] - m)
        l = w * l_ref[k] if l is None else l + w * l_ref[k]
        acc = w * acc_ref[k] if acc is None else acc + w * acc_ref[k]
    return l, acc


def _softmax_init(m_ref, l_ref, acc_ref):
    m_ref[...] = jnp.full(m_ref.shape, NEG, F32)
    l_ref[...] = jnp.zeros(l_ref.shape, F32)
    acc_ref[...] = jnp.zeros(acc_ref.shape, F32)


def _ada_kernel(c_ref, w_ref, b_ref, o_ref):
    c = c_ref[...]
    a = (c * _sigmoid(c)).astype(BF16)
    o_ref[...] = _dot(a, w_ref[...]) + b_ref[...]


def _ada(c, w, b):
    M, D = c.shape
    N = w.shape[1]
    tn = 1024 if N % 1024 == 0 else N
    return pl.pallas_call(
        _ada_kernel, grid=(N // tn,),
        in_specs=[pl.BlockSpec((M, D), lambda j: (0, 0)), pl.BlockSpec((D, tn), lambda j: (0, j)),
                  pl.BlockSpec((1, tn), lambda j: (0, j))],
        out_specs=pl.BlockSpec((M, tn), lambda j: (0, j)),
        out_shape=jax.ShapeDtypeStruct((M, N), F32), compiler_params=_cp("parallel"), name="ada")(c, w, b)


def _bias_kernel(tab_ref, bkt_ref, o_ref):
    h = pl.program_id(0)
    bkt = bkt_ref[...]
    acc = jnp.full(bkt.shape, NEG, F32)
    for b in range(REL_BUCKETS):
        acc = jnp.where(bkt == b, tab_ref[b, h], acc)
    o_ref[0] = acc


def _bias_tiles(table, buckets, n_heads):
    R, C = buckets.shape
    tr = R
    for cand in (512, 256, 128):
        if R > cand and R % cand == 0:
            tr = cand
            break
    return pl.pallas_call(
        _bias_kernel,
        grid_spec=pltpu.PrefetchScalarGridSpec(
            num_scalar_prefetch=1, grid=(n_heads, R // tr),
            in_specs=[pl.BlockSpec((tr, C), lambda h, r, tab: (r, 0))],
            out_specs=pl.BlockSpec((1, tr, C), lambda h, r, tab: (h, r, 0))),
        out_shape=jax.ShapeDtypeStruct((n_heads, R, C), F32),
        compiler_params=_cp("parallel", "parallel"), name="rel_bias")(table, buckets)


def _bucket_thresholds():
    exact = REL_BUCKETS // 2
    n = np.arange(exact, REL_MAX_DISTANCE + 1)
    scaled = np.log(n / exact) / math.log(REL_MAX_DISTANCE / exact)
    large = np.minimum(exact + np.trunc(scaled * (REL_BUCKETS - exact)).astype(np.int64), REL_BUCKETS - 1)
    return [int(n[np.argmax(large >= b)]) for b in range(exact + 1, REL_BUCKETS)]


def _t5_bucket(dist):
    n = jnp.maximum(dist, 0)
    exact = REL_BUCKETS // 2
    large = jnp.full(n.shape, exact, jnp.int32)
    for thr in _bucket_thresholds():
        large = large + jnp.where(n >= thr, 1, 0)
    return jnp.where(n < exact, n, large)


def _masked_bucket(dist, valid):
    return jnp.where(valid, _t5_bucket(dist), -1).astype(jnp.int32)


def _modulate(x, shift, scale):
    return _rms(x) * (1.0 + scale) + shift


def _store_stacked_heads(q, ref):
    _, G, n_tiles, M, _ = ref.shape
    R = M * n_tiles // q.shape[0]
    tq = M // R
    lane = lax.broadcasted_iota(jnp.int32, (q.shape[0], LANES), 1)
    for h in range(G * R):
        pair = q[:, LANES * (h // 2):LANES * (h // 2 + 1)]
        if h % 2:
            pair = pltpu.roll(pair, HEAD_DIM, 1)
        head = jnp.where(lane < HEAD_DIM, pair, 0.0).astype(ref.dtype)
        g, r = divmod(h, R)
        for t in range(n_tiles):
            ref[0, g, t, r * tq:(r + 1) * tq, :] = head[t * tq:(t + 1) * tq]


def _store_heads(o, o_ref, g, tq):
    R = o.shape[0] // tq
    lane = lax.broadcasted_iota(jnp.int32, (tq, LANES), 1)
    for p in range(R // 2):
        even = o[(2 * p) * tq:(2 * p + 1) * tq]
        odd = o[(2 * p + 1) * tq:(2 * p + 2) * tq]
        tile = jnp.where(lane < HEAD_DIM, pltpu.roll(even, HEAD_DIM, 1), odd)
        col = (g * R + 2 * p) * HEAD_DIM
        o_ref[0, :, col:col + LANES] = tile.astype(o_ref.dtype)


def _proj0_kernel(x_ref, sh_ref, sc_ref, w0_ref, qn_ref, kvn_ref, wuq_ref, wuqs_ref, cq_ref, sq_ref, cm_ref,
                  sm_ref, wa_ref, wb_ref, vones_ref, *outs, sample):
    h = _modulate(x_ref[0], sh_ref[0], sc_ref[0]).astype(BF16)
    y = _dot(h, w0_ref[...])
    qn = (_rms(y[:, 0:256]) * qn_ref[...]).astype(BF16)
    ckv = _rms(y[:, 256:512]) * kvn_ref[...]
    misc = y[:, 512:640] * cm_ref[...] + y[:, 640:768] * sm_ref[...]
    cq = jnp.concatenate([cq_ref[...]] * MLA_HEADS, axis=1)
    sq = jnp.concatenate([sq_ref[...]] * MLA_HEADS, axis=1)
    q_rot = ((_dot(qn, wuq_ref[...]) * cq + _dot(qn, wuqs_ref[...]) * sq) * MLA_SCALE).astype(BF16)
    ckv_b = ckv.astype(BF16)
    if sample:
        rows_ref, qcat_ref, qnsa_ref, cmp_ref, sel_ref, win_ref, gate_ref = outs
        qcat_ref[0] = _dot(q_rot, wa_ref[...]).astype(BF16)
    else:
        rows_ref, qmla_ref, kcat_ref, vmla_ref, qnsa_ref, cmp_ref, sel_ref, win_ref, selb_ref, winb_ref, gate_ref = outs
        qmla_ref[0] = q_rot
        kcat_ref[0] = _dot(jnp.concatenate([ckv_b, misc.astype(BF16)], axis=1), wa_ref[...]).astype(BF16)
        vmla_ref[0] = (_dot(ckv_b, wb_ref[...]) + vones_ref[...]).astype(BF16)
        selb_ref[0] = y[:, 1536:1792].astype(BF16)
        winb_ref[0] = y[:, 1792:2048].astype(BF16)
    rows_ref[0, :, 0:256] = ckv
    rows_ref[0, :, 256:288] = misc[:, 0:MLA_ROPE]
    if sample:
        qnsa_ref[0] = y[:, 768:1280].astype(BF16)
    else:
        _store_stacked_heads(y[:, 768:1280], qnsa_ref)
    cmp_ref[0] = y[:, 1280:1536]
    sel_ref[0] = y[:, 1536:1792]
    win_ref[0] = y[:, 1792:2048]
    gate_ref[0] = _sigmoid(misc)


def _proj0(x, shift, scale, w, tabs, sample, tm):
    Bx, Tx, D = x.shape
    per_token = shift.shape[1] != 1
    mod_spec = (pl.BlockSpec((1, tm, D), lambda b, i: (b, i, 0)) if per_token
                else pl.BlockSpec((1, 1, D), lambda b, i: (b, 0, 0)))
    tok = lambda n: pl.BlockSpec((1, tm, n), lambda b, i: (b, i, 0))
    tab_spec = pl.BlockSpec((tm, LANES), lambda b, i: (i, 0))
    wa, wb = (w["a_abs"], w["wuv"]) if sample else (w["kcat"], w["wuv"])
    in_specs = [tok(D), mod_spec, mod_spec, _const_spec(w["w0"].shape), _const_spec((1, 256)), _const_spec((1, 256)),
                _const_spec(w["wuq"].shape), _const_spec(w["wuqs"].shape), tab_spec, tab_spec, tab_spec, tab_spec,
                _const_spec(wa.shape), _const_spec(wb.shape), _const_spec(w["vones"].shape)]
    sd = lambda n, dt: jax.ShapeDtypeStruct((Bx, Tx, n), dt)
    if sample:
        out_shape = [sd(288, F32), sd(MLA_HEADS * 288, BF16), sd(512, BF16), sd(256, F32), sd(256, F32), sd(256, F32),
                     sd(LANES, F32)]
    else:
        out_shape = [sd(288, F32), sd(1024, BF16), sd(1024, BF16), sd(1024, BF16), sd(512, BF16), sd(256, F32),
                     sd(256, F32), sd(256, F32), sd(256, BF16), sd(256, BF16), sd(LANES, F32)]
    out_specs = [tok(s.shape[-1]) for s in out_shape]
    if not sample:
        G, R, nt = NSA_KV_HEADS, NSA_HEADS // NSA_KV_HEADS, tm // Q_BLOCK
        out_shape[4] = jax.ShapeDtypeStruct((Bx, G, Tx // Q_BLOCK, R * Q_BLOCK, LANES), BF16)
        out_specs[4] = pl.BlockSpec((1, G, nt, R * Q_BLOCK, LANES), lambda b, i: (b, 0, i, 0, 0))
    return pl.pallas_call(
        functools.partial(_proj0_kernel, sample=sample), grid=(Bx, Tx // tm), in_specs=in_specs, out_specs=out_specs,
        out_shape=out_shape, compiler_params=_cp("parallel", "parallel"),
        name="proj0_sample" if sample else "proj0_prompt")(
            x, shift, scale, w["w0"], w["qn"], w["kvn"], w["wuq"], w["wuqs"], *tabs, wa, wb, w["vones"])


def _proj1_kernel(x_ref, sh_ref, sc_ref, w_ref, q_ref, kv_ref, kvb_ref, *, nq, stacked):
    h = _modulate(x_ref[0], sh_ref[0], sc_ref[0]).astype(BF16)
    y = _dot(h, w_ref[...])
    if stacked:
        _store_stacked_heads(y[:, :nq], q_ref)
    else:
        q_ref[0] = y[:, :nq].astype(BF16)
    kv_ref[0] = y[:, nq:]
    kvb_ref[0] = y[:, nq:].astype(BF16)


def _proj1(x, shift, scale, w1, nq, tm, stacked):
    Bx, Tx, D = x.shape
    nkv = w1.shape[1] - nq
    per_token = shift.shape[1] != 1
    mod_spec = (pl.BlockSpec((1, tm, D), lambda b, i: (b, i, 0)) if per_token
                else pl.BlockSpec((1, 1, D), lambda b, i: (b, 0, 0)))
    tok = lambda n: pl.BlockSpec((1, tm, n), lambda b, i: (b, i, 0))
    q_shape, q_spec = jax.ShapeDtypeStruct((Bx, Tx, nq), BF16), tok(nq)
    if stacked:
        G, R, nt = SWA_KV_HEADS, SWA_HEADS // SWA_KV_HEADS, tm // Q_BLOCK
        q_shape = jax.ShapeDtypeStruct((Bx, G, Tx // Q_BLOCK, R * Q_BLOCK, LANES), BF16)
        q_spec = pl.BlockSpec((1, G, nt, R * Q_BLOCK, LANES), lambda b, i: (b, 0, i, 0, 0))
    return pl.pallas_call(
        functools.partial(_proj1_kernel, nq=nq, stacked=stacked), grid=(Bx, Tx // tm),
        in_specs=[tok(D), mod_spec, mod_spec, _const_spec(w1.shape)], out_specs=[q_spec, tok(nkv), tok(nkv)],
        out_shape=[q_shape, jax.ShapeDtypeStruct((Bx, Tx, nkv), F32), jax.ShapeDtypeStruct((Bx, Tx, nkv), BF16)],
        compiler_params=_cp("parallel", "parallel"), name="proj1")(x, shift, scale, w1)


def _mla_prompt_kernel(q_ref, k_ref, v_ref, o_ref, m_sc, acc_sc, *, tq):
    i = pl.program_id(2)
    q = q_ref[0]
    causal = lax.broadcasted_iota(jnp.int32, (tq, tq), 1) <= lax.broadcasted_iota(jnp.int32, (tq, tq), 0)
    m_sc[...] = jnp.full(m_sc.shape, NEG, F32)
    acc_sc[...] = jnp.zeros(acc_sc.shape, F32)

    def step(j, masked):
        start = pl.multiple_of(j * tq, tq)
        kk = k_ref[0, pl.ds(start, tq), :]
        vv = v_ref[0, pl.ds(start, tq), :]
        for e in range(2):
            sl = slice(LANES * e, LANES * (e + 1))
            s = _dot_t(q[:, sl], kk[:, sl])
            if masked:
                s = jnp.where(causal, s, NEG)
            _flash_step([s], [vv[:, sl]], m_sc.at[e], acc_sc.at[e])

    def body(j, carry):
        step(j, False)
        return carry

    lax.fori_loop(0, i, body, 0)
    step(i, True)
    a0, a1 = acc_sc[0], acc_sc[1]
    o0 = a0 / pltpu.roll(a0, MLA_V, 1)
    o1 = a1 / pltpu.roll(a1, MLA_V, 1)
    lane = lax.broadcasted_iota(jnp.int32, (tq, LANES), 1)
    o_ref[0] = jnp.where(lane < MLA_V, o0, o1).astype(o_ref.dtype)


def _mla_prompt(q, k, v, tq):
    B, T, _ = q.shape
    return pl.pallas_call(
        functools.partial(_mla_prompt_kernel, tq=tq), grid=(B, MLA_HEADS // 2, T // tq),
        in_specs=[pl.BlockSpec((1, tq, 2 * LANES), lambda b, h, i: (b, i, h)),
                  pl.BlockSpec((1, T, 2 * LANES), lambda b, h, i: (b, 0, h)),
                  pl.BlockSpec((1, T, 2 * LANES), lambda b, h, i: (b, 0, h))],
        out_specs=pl.BlockSpec((1, tq, LANES), lambda b, h, i: (b, i, h)),
        out_shape=jax.ShapeDtypeStruct((B, T, MLA_HEADS * MLA_V), BF16),
        scratch_shapes=[pltpu.VMEM((2, tq, LANES), F32), pltpu.VMEM((2, tq, LANES), F32)],
        compiler_params=_cp("parallel", "parallel", "parallel"), name="mla_prompt")(q, k, v)


def _pe_bias_kernel(pe_ref, w_ref, o_ref):
    o_ref[...] = _dot(pe_ref[...].astype(BF16), w_ref[...])


def _pe_bias(pe_rows, wbig):
    return pl.pallas_call(_pe_bias_kernel, out_shape=jax.ShapeDtypeStruct((pe_rows.shape[0], wbig.shape[1]), F32),
                          compiler_params=_cp(), name="pe_bias")(pe_rows, wbig)


def _compress_prompt_kernel(x_ref, w_ref, peb_ref, o_ref):
    o_ref[0] = (_dot(x_ref[0].astype(BF16), w_ref[...]) + peb_ref[0:1, :]).astype(o_ref.dtype)


def _compress_prompt(x, wbig, pe_bias):
    B, NC, K = x.shape
    return pl.pallas_call(
        _compress_prompt_kernel, grid=(B,),
        in_specs=[pl.BlockSpec((1, NC, K), lambda b: (b, 0, 0)), _const_spec(wbig.shape), _const_spec(pe_bias.shape)],
        out_specs=pl.BlockSpec((1, NC, 256), lambda b: (b, 0, 0)),
        out_shape=jax.ShapeDtypeStruct((B, NC, 256), BF16), compiler_params=_cp("parallel"),
        name="compress_prompt")(x, wbig, pe_bias)


def _compress_paged_kernel(pt_ref, *refs, pps):
    page_refs = refs[:pps]
    perm_ref, w_ref, peb_ref, o_ref = refs[pps:]
    pairs = []
    for pr in range(pps // 2):
        both = jnp.concatenate([page_refs[2 * pr][0], page_refs[2 * pr + 1][0]], axis=1).astype(BF16)
        pairs.append(_dot_t(perm_ref[...], both))
    n_blk = o_ref.shape[1]
    acc = None
    for l in range(NSA_CMP_BLOCK):
        x = jnp.concatenate([p[8 * l:8 * (l + 1)] for p in pairs], axis=0)
        x = jnp.concatenate([x[:, :LANES], x[:, LANES:]], axis=0).astype(BF16)
        part = _dot(x, w_ref[l])
        acc = part if acc is None else acc + part
    y = jnp.concatenate([acc[:n_blk], acc[n_blk:]], axis=1)
    o_ref[0] = (y + peb_ref[0:1, :]).astype(o_ref.dtype)


def _compress_paged(page_table, cache_t, w_head, pe_bias, pps):
    Bd, n_pages = page_table.shape
    _, feat, page = cache_t.shape
    per_page = page // NSA_CMP_BLOCK
    assert per_page * 2 == 8 and feat == 2 * LANES, "a pair of pages must hold one sublane tile of blocks"
    n_blk = pps * per_page
    r = jnp.arange(2 * page)
    l, p2, n = r // 8, (r // per_page) % 2, r % per_page
    perm = (jnp.arange(2 * page)[None, :] == (p2 * page + n * NSA_CMP_BLOCK + l)[:, None]).astype(BF16)
    page_specs = [pl.BlockSpec((1, feat, page), lambda b, c, pt, k=k: (pt[b, c * pps + k], 0, 0)) for k in range(pps)]
    cst = lambda a: pl.BlockSpec(a.shape, lambda b, c, pt: (0,) * a.ndim)
    return pl.pallas_call(
        functools.partial(_compress_paged_kernel, pps=pps),
        grid_spec=pltpu.PrefetchScalarGridSpec(
            num_scalar_prefetch=1, grid=(Bd, n_pages // pps),
            in_specs=page_specs + [cst(perm), cst(w_head), cst(pe_bias)],
            out_specs=pl.BlockSpec((1, n_blk, feat), lambda b, c, pt: (b, c, 0))),
        out_shape=jax.ShapeDtypeStruct((Bd, n_pages * per_page, feat), BF16),
        compiler_params=_cp("parallel", "parallel"), name="compress_paged")(
            page_table, *([cache_t] * pps), perm, w_head, pe_bias)


def _cmp_select_kernel(q_ref, kv_ref, bias_ref, code_ref, o_ref, sel_ref, *, tq, n_rank, first_pos):
    i = pl.program_id(1)
    G, R = NSA_KV_HEADS, NSA_HEADS // NSA_KV_HEADS
    M = R * tq
    NC = kv_ref.shape[1]
    n_lanes = code_ref.shape[-1]
    code = code_ref[0]
    row_pos = first_pos + i * tq + lax.broadcasted_iota(jnp.int32, (M, 1), 0) % tq
    any_valid = (row_pos >= NSA_CMP_BLOCK - 1).astype(F32)
    lane = lax.broadcasted_iota(jnp.int32, (tq, n_lanes), 1)
    for g in range(G):
        kv = kv_ref[0, :, LANES * g:LANES * (g + 1)]
        s = _dot_t(q_ref[0, g, 0], kv) + bias_ref[0, g]
        m = jnp.max(s, axis=-1, keepdims=True)
        p = jnp.exp(s - m)
        p = p * (any_valid / jnp.sum(p, axis=-1, keepdims=True))
        o_ref[0, g, 0] = _dot(p.astype(BF16), kv).astype(o_ref.dtype)
        imp = p[0:tq]
        for r in range(1, R):
            imp = imp + p[r * tq:(r + 1) * tq]
        imp = imp[:, :NC // 2] + imp[:, NC // 2:]
        if n_lanes > NC // 2:
            imp = jnp.concatenate([imp, jnp.zeros((tq, n_lanes - NC // 2), F32)], axis=1)
        score = jnp.where(code == 0.0, imp, code)
        rank = jnp.zeros((tq, n_lanes), F32)
        for jp in range(n_rank):
            col = score[:, jp:jp + 1]
            ahead = jnp.where(col > score, 1.0, jnp.where(col == score, jnp.where(lane > jp, 1.0, 0.0), 0.0))
            rank = rank + ahead
        sel_neg = jnp.where(rank < float(NSA_TOP_N), jnp.where(code > -0.5 * SELECT_BIG, 0.0, NEG), NEG)
        sel_ref[0, g, 0] = sel_neg.astype(sel_ref.dtype)


def _cmp_select(q_stack, kvc, bias, code, tq, n_rank, first_pos):
    B, G, nq, M, _ = q_stack.shape
    NC = kvc.shape[1]
    n_lanes = code.shape[-1]
    return pl.pallas_call(
        functools.partial(_cmp_select_kernel, tq=tq, n_rank=n_rank, first_pos=first_pos), grid=(B, nq),
        in_specs=[pl.BlockSpec((1, G, 1, M, LANES), lambda b, i: (b, 0, i, 0, 0)),
                  pl.BlockSpec((1, NC, G * LANES), lambda b, i: (b, 0, 0)),
                  pl.BlockSpec((1, G, M, NC), lambda b, i: (i, 0, 0, 0)),
                  pl.BlockSpec((1, tq, n_lanes), lambda b, i: (i, 0, 0))],
        out_specs=[pl.BlockSpec((1, G, 1, M, LANES), lambda b, i: (b, 0, i, 0, 0)),
                   pl.BlockSpec((1, G, 1, tq, n_lanes), lambda b, i: (b, 0, i, 0, 0))],
        out_shape=[jax.ShapeDtypeStruct((B, G, nq, M, LANES), BF16),
                   jax.ShapeDtypeStruct((B, G, nq, tq, n_lanes), BF16)],
        compiler_params=_cp("parallel", "parallel"), name="cmp_select")(q_stack, kvc, bias, code)


def _sel_prompt_kernel(q_ref, sel_ref, kv_ref, oh_ref, bnear_ref, bfar_ref, o_ref, m_sc, acc_sc, *, tq, tk):
    i = pl.program_id(1)
    G = NSA_KV_HEADS
    M = q_ref.shape[3]
    R = M // tq
    far_end = jnp.maximum(i - 1, 0) * tq
    n_full = far_end // tk
    rem = far_end - n_full * tk
    ones = jnp.ones((tk, LANES), BF16)
    m_sc[...] = jnp.full(m_sc.shape, NEG, F32)
    acc_sc[...] = jnp.zeros(acc_sc.shape, F32)

    def scores(start, size):
        kv_all = kv_ref[0, pl.ds(start, size), :]
        onehot = oh_ref[pl.ds(start, size), :]
        out = []
        for g in range(G):
            q = jnp.concatenate([q_ref[0, g, 0], jnp.concatenate([sel_ref[0, g, 0]] * R, axis=0)], axis=1)
            keys = kv_all[:, LANES * g:LANES * (g + 1)]
            out.append((_dot_t(q, jnp.concatenate([keys, onehot], axis=1)), jnp.concatenate([keys, ones[:size]], axis=1)))
        return out

    def body(j, carry):
        for g, (s, v) in enumerate(scores(pl.multiple_of(j * tk, tk), tk)):
            _flash_step([s], [v], m_sc.at[g], acc_sc.at[g])
        return carry

    lax.fori_loop(0, n_full, body, 0)

    @pl.when(rem > 0)
    def _():
        keep = lax.broadcasted_iota(jnp.int32, (M, tk), 1) < rem
        for g, (s, v) in enumerate(scores(pl.multiple_of(n_full * tk, tk), tk)):
            _flash_step([jnp.where(keep, s, NEG)], [v], m_sc.at[g], acc_sc.at[g])

    prev = scores(pl.multiple_of(jnp.maximum(i - 1, 0) * tq, tq), tq)
    diag = scores(pl.multiple_of(i * tq, tq), tq)
    edge = jnp.where(i >= 1, 0.0, NEG)
    for g in range(G):
        far = bfar_ref[g]
        s0 = prev[g][0] + (bnear_ref[g, :, 0:tq] - far) + edge
        s1 = diag[g][0] + (bnear_ref[g, :, tq:2 * tq] - far)
        _flash_step([s0, s1], [prev[g][1], diag[g][1]], m_sc.at[g], acc_sc.at[g])
        acc = acc_sc[g]
        _store_heads(acc[:, :LANES] / acc[:, LANES:], o_ref, g, tq)


def _sel_prompt(q_stack, sel_rows, kv, onehot, bias_near, bias_far, tq, tk):
    B, G, nq, M, _ = q_stack.shape
    T = kv.shape[1]
    return pl.pallas_call(
        functools.partial(_sel_prompt_kernel, tq=tq, tk=tk), grid=(B, nq),
        in_specs=[pl.BlockSpec((1, G, 1, M, LANES), lambda b, i: (b, 0, i, 0, 0)),
                  pl.BlockSpec((1, G, 1, tq, LANES), lambda b, i: (b, 0, i, 0, 0)),
                  pl.BlockSpec((1, T, G * LANES), lambda b, i: (b, 0, 0)),
                  _const_spec(onehot.shape), _const_spec(bias_near.shape), _const_spec(bias_far.shape)],
        out_specs=pl.BlockSpec((1, tq, G * (M // tq) * HEAD_DIM), lambda b, i: (b, i, 0)),
        out_shape=jax.ShapeDtypeStruct((B, nq * tq, G * (M // tq) * HEAD_DIM), BF16),
        scratch_shapes=[pltpu.VMEM((G, M, LANES), F32), pltpu.VMEM((G, M, 2 * LANES), F32)],
        compiler_params=_cp("parallel", "parallel"), name="sel_prompt")(q_stack, sel_rows, kv, onehot, bias_near, bias_far)


def _cmp_select_t_kernel(q_ref, kv_ref, kvt_ref, bias_ref, code_ref, eye_ref, o_ref, sel_ref, *, tq):
    i = pl.program_id(1)
    G, R = NSA_KV_HEADS, NSA_HEADS // NSA_KV_HEADS
    M = R * tq
    NC = kv_ref.shape[1]
    n_sel = code_ref.shape[1]
    code = code_ref[0]
    col_pos = i * tq + lax.broadcasted_iota(jnp.int32, (1, M), 1) % tq
    any_valid = (col_pos >= NSA_CMP_BLOCK - 1).astype(F32)
    blk = lax.broadcasted_iota(jnp.int32, (n_sel, tq), 0)
    for g in range(G):
        kv = kv_ref[0, :, LANES * g:LANES * (g + 1)]
        bias = jnp.concatenate([bias_ref[g * R + r] for r in range(R)], axis=1)
        s = _dot_t(kv, q_ref[0, g, 0]) + bias
        p = jnp.exp(s - jnp.max(s, axis=0, keepdims=True))
        p = p * (any_valid / jnp.sum(p, axis=0, keepdims=True))
        o_t = _dot(kvt_ref[0, LANES * g:LANES * (g + 1), :], p.astype(BF16))
        _store_heads(_dot_t(eye_ref[...], o_t.astype(BF16)), o_ref, g, tq)
        imp = p[:, 0:tq]
        for r in range(1, R):
            imp = imp + p[:, r * tq:(r + 1) * tq]
        imp = imp[:NC // 2] + imp[NC // 2:]
        score = jnp.where(code == 0.0, imp, code)
        rank = jnp.zeros((n_sel, tq), F32)
        for jp in range(n_sel):
            row = score[jp:jp + 1, :]
            rank = rank + jnp.where(row > score, 1.0, jnp.where(row == score, jnp.where(blk > jp, 1.0, 0.0), 0.0))
        sel_neg = jnp.where(rank < float(NSA_TOP_N), jnp.where(code > -0.5 * SELECT_BIG, 0.0, NEG), NEG)
        padded = jnp.concatenate([sel_neg.astype(BF16), jnp.zeros((LANES - n_sel, tq), BF16)], axis=0)
        sel_ref[0, g, 0] = _dot_t(eye_ref[0:tq, 0:tq], padded).astype(sel_ref.dtype)


def _cmp_select_t(q_stack, kvc, kvc_t, bias_t, code_t, tq):
    B, G, nq, M, _ = q_stack.shape
    NC = kvc.shape[1]
    n_sel = code_t.shape[1]
    eye = jnp.eye(M, dtype=BF16)
    return pl.pallas_call(
        functools.partial(_cmp_select_t_kernel, tq=tq), grid=(B, nq),
        in_specs=[pl.BlockSpec((1, G, 1, M, LANES), lambda b, i: (b, 0, i, 0, 0)),
                  pl.BlockSpec((1, NC, G * LANES), lambda b, i: (b, 0, 0)),
                  pl.BlockSpec((1, G * LANES, NC), lambda b, i: (b, 0, 0)),
                  pl.BlockSpec((bias_t.shape[0], NC, tq), lambda b, i: (0, 0, i)),
                  pl.BlockSpec((1, n_sel, tq), lambda b, i: (i, 0, 0)), _const_spec(eye.shape)],
        out_specs=[pl.BlockSpec((1, tq, G * (M // tq) * HEAD_DIM), lambda b, i: (b, i, 0)),
                   pl.BlockSpec((1, G, 1, tq, LANES), lambda b, i: (b, 0, i, 0, 0))],
        out_shape=[jax.ShapeDtypeStruct((B, nq * tq, G * (M // tq) * HEAD_DIM), BF16),
                   jax.ShapeDtypeStruct((B, G, nq, tq, LANES), BF16)],
        compiler_params=_cp("parallel", "parallel"), name="cmp_select_t")(q_stack, kvc, kvc_t, bias_t, code_t, eye)


def _banded_kernel(q_ref, kv_ref, bias_ref, sink_ref, o_ref, *, tq, nw, G):
    i = pl.program_id(1)
    ones = jnp.ones((tq, LANES), BF16)
    for g in range(G):
        q = q_ref[0, g, 0]
        scores, vals = [], []
        for jj in range(nw + 1):
            kb = i - nw + jj
            start = pl.multiple_of(jnp.maximum(kb, 0) * tq, tq)
            kv = kv_ref[0, pl.ds(start, tq), LANES * g:LANES * (g + 1)]
            edge = jnp.where(kb >= 0, 0.0, NEG)
            scores.append(_dot_t(q, kv) + bias_ref[g, :, tq * jj:tq * (jj + 1)] + edge)
            vals.append(kv)
        sink = sink_ref[g]
        m = sink
        for s in scores:
            m = jnp.maximum(m, jnp.max(s, axis=-1, keepdims=True))
        acc = None
        for s, v in zip(scores, vals):
            pv = _dot(jnp.exp(s - m).astype(BF16), jnp.concatenate([v, ones], axis=1))
            acc = pv if acc is None else acc + pv
        _store_heads(acc[:, :LANES] / (acc[:, LANES:] + jnp.exp(sink - m)), o_ref, g, tq)


def _banded(q_stack, kv, bias, sinks, tq, nw):
    B, G, nq, M, _ = q_stack.shape
    T = kv.shape[1]
    return pl.pallas_call(
        functools.partial(_banded_kernel, tq=tq, nw=nw, G=G), grid=(B, nq),
        in_specs=[pl.BlockSpec((1, G, 1, M, LANES), lambda b, i: (b, 0, i, 0, 0)),
                  pl.BlockSpec((1, T, G * LANES), lambda b, i: (b, 0, 0)),
                  _const_spec(bias.shape), _const_spec(sinks.shape)],
        out_specs=pl.BlockSpec((1, tq, G * (M // tq) * HEAD_DIM), lambda b, i: (b, i, 0)),
        out_shape=jax.ShapeDtypeStruct((B, nq * tq, G * (M // tq) * HEAD_DIM), BF16),
        compiler_params=_cp("parallel", "parallel"), name="banded")(q_stack, kv, bias, sinks)


def _mla_decode_kernel(pt_ref, q_ref, *refs, pps):
    page_refs = refs[:pps]
    new_ref, nmask_ref, o_ref, m_sc, l_sc, acc_sc = refs[pps:]
    c = pl.program_id(1)

    @pl.when(c == 0)
    def _():
        _softmax_init(m_sc, l_sc, acc_sc)

    q = q_ref[0]
    nv = acc_sc.shape[-1]
    for half in range(N_CHAINS):
        rows = [r[0].astype(BF16) for r in page_refs[half::N_CHAINS]]
        _softmax_update([_dot(q, r) for r in rows], [r[:nv] for r in rows], m_sc.at[half], l_sc.at[half],
                        acc_sc.at[half], transposed_values=True)

    @pl.when(c == pl.num_programs(1) - 1)
    def _():
        nr = new_ref[0].astype(BF16)
        _softmax_update([_dot_t(q, nr) + nmask_ref[...]], [nr[:, :nv]], m_sc.at[0], l_sc.at[0], acc_sc.at[0])
        l, acc = _merge_chains(m_sc, l_sc, acc_sc)
        o_ref[0] = (acc / l).astype(o_ref.dtype)


def _mla_decode(page_table, q_cat, cache_t, new_rows, new_mask, pps):
    Bd, n_pages = page_table.shape
    _, width, page = cache_t.shape
    M = q_cat.shape[1]
    nv = width - MLA_ROPE
    page_specs = [pl.BlockSpec((1, width, page), lambda b, c, pt, k=k: (pt[b, c * pps + k], 0, 0)) for k in range(pps)]
    return pl.pallas_call(
        functools.partial(_mla_decode_kernel, pps=pps),
        grid_spec=pltpu.PrefetchScalarGridSpec(
            num_scalar_prefetch=1, grid=(Bd, n_pages // pps),
            in_specs=[pl.BlockSpec((1, M, width), lambda b, c, pt: (b, 0, 0))] + page_specs + [
                pl.BlockSpec((1,) + new_rows.shape[1:], lambda b, c, pt: (b, 0, 0)),
                pl.BlockSpec(new_mask.shape, lambda b, c, pt: (0, 0))],
            out_specs=pl.BlockSpec((1, M, nv), lambda b, c, pt: (b, 0, 0)),
            scratch_shapes=[pltpu.VMEM((N_CHAINS, M, 1), F32), pltpu.VMEM((N_CHAINS, M, 1), F32),
                            pltpu.VMEM((N_CHAINS, M, nv), F32)]),
        out_shape=jax.ShapeDtypeStruct((Bd, M, nv), BF16),
        compiler_params=_cp("parallel", "arbitrary"), name="mla_decode")(
            page_table, q_cat, *([cache_t] * pps), new_rows, new_mask)


def _sel_decode_kernel(pt_ref, q_ref, oh_ref, blast_ref, bfar_ref, *refs, pps):
    page_refs = refs[:pps]
    new_ref, nbias_ref, o_ref, m_sc, l_sc, acc_sc = refs[pps:]
    c = pl.program_id(1)
    last = c == pl.num_programs(1) - 1
    G = NSA_KV_HEADS

    @pl.when(c == 0)
    def _():
        _softmax_init(m_sc, l_sc, acc_sc)

    M = q_ref.shape[2]
    rows = lambda g: slice(M * g, M * (g + 1))
    for half in range(N_CHAINS):
        scores, vals = [], []
        for k in range(half, pps, N_CHAINS):
            kv_t = [page_refs[k][0, LANES * g:LANES * (g + 1), :].astype(BF16) for g in range(G)]
            onehot = oh_ref[c * pps + k]
            s = jnp.concatenate([_dot(q_ref[0, g], jnp.concatenate([kv_t[g], onehot], axis=0)) for g in range(G)],
                                axis=0)
            if k == pps - 1:
                s = s + jnp.where(last, blast_ref[...] - bfar_ref[...], 0.0)
            scores.append(s)
            vals.append(kv_t)
        m_prev = m_sc[half]
        m_new = m_prev
        for s in scores:
            m_new = jnp.maximum(m_new, jnp.max(s, axis=-1, keepdims=True))
        alpha = jnp.exp(m_prev - m_new)
        l_new = alpha * l_sc[half]
        acc = alpha * acc_sc[half]
        for s, kv_t in zip(scores, vals):
            p = jnp.exp(s - m_new)
            l_new = l_new + jnp.sum(p, axis=-1, keepdims=True)
            pb = p.astype(BF16)
            acc = acc + jnp.concatenate([_dot_t(pb[rows(g)], kv_t[g]) for g in range(G)], axis=0)
        m_sc[half] = m_new
        l_sc[half] = l_new
        acc_sc[half] = acc

    @pl.when(last)
    def _():
        kv = [new_ref[0, :, LANES * g:LANES * (g + 1)].astype(BF16) for g in range(G)]
        n_new = kv[0].shape[0]
        s = jnp.concatenate([_dot_t(q_ref[0, g][:, :LANES], kv[g]) for g in range(G)], axis=0)
        s = s + (nbias_ref[...] - bfar_ref[:, :n_new])
        m_fin = jnp.maximum(m_sc[0], jnp.max(s, axis=-1, keepdims=True))
        a_fin = jnp.exp(m_sc[0] - m_fin)
        p = jnp.exp(s - m_fin)
        pb = p.astype(BF16)
        l_sc[0] = a_fin * l_sc[0] + jnp.sum(p, axis=-1, keepdims=True)
        acc_sc[0] = a_fin * acc_sc[0] + jnp.concatenate([_dot(pb[rows(g)], kv[g]) for g in range(G)], axis=0)
        m_sc[0] = m_fin
        l, acc = _merge_chains(m_sc, l_sc, acc_sc)
        out = acc / l
        for g in range(G):
            o_ref[0, g] = out[rows(g)].astype(o_ref.dtype)


def _sel_decode(page_table, q_aug, onehot_t, bias_last, bias_far, cache_t, new_rows, new_bias, pps):
    Bd, n_pages = page_table.shape
    _, feat, page = cache_t.shape
    _, G, M, _ = q_aug.shape
    page_specs = [pl.BlockSpec((1, feat, page), lambda b, c, pt, k=k: (pt[b, c * pps + k], 0, 0)) for k in range(pps)]
    cst = lambda a: pl.BlockSpec(a.shape, lambda b, c, pt: (0,) * a.ndim)
    per_b = lambda a: pl.BlockSpec((1,) + a.shape[1:], lambda b, c, pt: (b,) + (0,) * (a.ndim - 1))
    return pl.pallas_call(
        functools.partial(_sel_decode_kernel, pps=pps),
        grid_spec=pltpu.PrefetchScalarGridSpec(
            num_scalar_prefetch=1, grid=(Bd, n_pages // pps),
            in_specs=[per_b(q_aug), cst(onehot_t), cst(bias_last), cst(bias_far)] + page_specs
            + [per_b(new_rows), cst(new_bias)],
            out_specs=pl.BlockSpec((1, G, M, LANES), lambda b, c, pt: (b, 0, 0, 0)),
            scratch_shapes=[pltpu.VMEM((N_CHAINS, G * M, 1), F32), pltpu.VMEM((N_CHAINS, G * M, 1), F32),
                            pltpu.VMEM((N_CHAINS, G * M, LANES), F32)]),
        out_shape=jax.ShapeDtypeStruct((Bd, G, M, LANES), BF16),
        compiler_params=_cp("parallel", "arbitrary"), name="sel_decode")(
            page_table, q_aug, onehot_t, bias_last, bias_far, *([cache_t] * pps), new_rows, new_bias)


def _win_decode_kernel(q_ref, buf_ref, new_ref, bias_ref, nbias_ref, sink_ref, o_ref, *, G, bb):
    def one_sequence(j, carry):
        for g in range(G):
            q = q_ref[j, g]
            kb = buf_ref[j, LANES * g:LANES * (g + 1), :].astype(BF16)
            kn = new_ref[j, :, LANES * g:LANES * (g + 1)].astype(BF16)
            sb = _dot(q, kb) + bias_ref[g]
            sn = _dot_t(q, kn) + nbias_ref[g]
            sink = sink_ref[g]
            m = jnp.maximum(jnp.maximum(jnp.max(sb, axis=-1, keepdims=True), jnp.max(sn, axis=-1, keepdims=True)), sink)
            pb = jnp.exp(sb - m)
            pn = jnp.exp(sn - m)
            l = jnp.exp(sink - m) + jnp.sum(pb, axis=-1, keepdims=True) + jnp.sum(pn, axis=-1, keepdims=True)
            acc = _dot_t(pb.astype(BF16), kb) + _dot(pn.astype(BF16), kn)
            o_ref[j, g] = (acc / l).astype(o_ref.dtype)
        return carry

    lax.fori_loop(0, bb, one_sequence, 0)


def _win_decode(q_stack, buf, new_rows, bias, new_bias, sinks):
    Bd, G, M, _ = q_stack.shape
    bb = 4 if Bd % 4 == 0 else 1
    per_b = lambda a: pl.BlockSpec((bb,) + a.shape[1:], lambda b: (b,) + (0,) * (a.ndim - 1))
    return pl.pallas_call(
        functools.partial(_win_decode_kernel, G=G, bb=bb), grid=(Bd // bb,),
        in_specs=[per_b(q_stack), per_b(buf), per_b(new_rows), _const_spec(bias.shape), _const_spec(new_bias.shape),
                  _const_spec(sinks.shape)],
        out_specs=pl.BlockSpec((bb, G, M, LANES), lambda b: (b, 0, 0, 0)),
        out_shape=jax.ShapeDtypeStruct((Bd, G, M, LANES), BF16),
        compiler_params=_cp("parallel"), name="win_decode")(q_stack, buf, new_rows, bias, new_bias, sinks)


def _ffn_chunk(f):
    for cand in range(min(f, 1536) // LANES, 0, -1):
        if f % (cand * LANES) == 0:
            return cand * LANES
    return f


def _out_ffn_kernel(*refs, mode, final, n_mods):
    it = iter(refs)
    y_ref = next(it)
    if mode == "c":
        attn_in = next(it)[0]
    else:
        oa_ref, ocmp_ref, osel_ref, owin_ref, gate_ref, eg_ref = (next(it) for _ in range(6))
        if mode == "ab_sample":
            wuv_ref = next(it)
    wout_ref = next(it)
    gt_ref, fsh_ref, fsc_ref, fgt_ref = (next(it) for _ in range(4))
    wg_ref, wu_ref, wo_ref = (next(it) for _ in range(3))
    gain_ref = next(it) if final else None
    o_ref = next(it)

    if mode == "c":
        attn = _dot(attn_in, wout_ref[...])
    else:
        gates = gate_ref[0]
        g_hi = gates.astype(BF16)
        g_lo = (gates - g_hi.astype(F32)).astype(BF16)
        o_b = None
        for br, ref in enumerate((ocmp_ref, osel_ref, owin_ref)):
            ge = _dot(g_hi, eg_ref[br]) + _dot(g_lo, eg_ref[br])
            term = ge * ref[0].astype(F32)
            o_b = term if o_b is None else o_b + term
        o_a = oa_ref[0]
        if mode == "ab_sample":
            o_a = _dot(o_a, wuv_ref[...]).astype(BF16)
        na = o_a.shape[-1]
        attn = _dot(o_a, wout_ref[0:na, :]) + _dot(o_b.astype(BF16), wout_ref[na:, :])
    y1 = y_ref[0] + gt_ref[0] * attn
    h = _modulate(y1, fsh_ref[0], fsc_ref[0]).astype(BF16)
    f = wg_ref.shape[1]
    fc = _ffn_chunk(f)
    acc = None
    for k in range(f // fc):
        g = _dot(h, wg_ref[:, fc * k:fc * (k + 1)])
        u = _dot(h, wu_ref[:, fc * k:fc * (k + 1)])
        a = (g * _sigmoid(g) * u).astype(BF16)
        part = _dot(a, wo_ref[fc * k:fc * (k + 1), :])
        acc = part if acc is None else acc + part
    y2 = y1 + fgt_ref[0] * acc
    if final:
        y2 = _rms(y2) * gain_ref[...]
    o_ref[0] = y2


def _out_ffn(y, attn_parts, weights, mods, ffn_w, gain, mode, tm):
    Bx, Tx, D = y.shape
    per_token = mods[0].shape[1] != 1
    mod_spec = (pl.BlockSpec((1, tm, D), lambda b, i: (b, i, 0)) if per_token
                else pl.BlockSpec((1, 1, D), lambda b, i: (b, 0, 0)))
    tok = lambda a: pl.BlockSpec((1, tm, a.shape[-1]), lambda b, i: (b, i, 0))
    single = lambda a: pl.BlockSpec(a.shape, lambda b, i: (0,) * a.ndim, pipeline_mode=pl.Buffered(1))
    args = [y] + list(attn_parts) + list(weights) + list(mods) + list(ffn_w)
    in_specs = ([tok(y)] + [tok(a) for a in attn_parts] + [single(a) for a in weights] + [mod_spec] * 4
                + [single(a) for a in ffn_w])
    final = gain is not None
    if final:
        args.append(gain)
        in_specs.append(single(gain))
    return pl.pallas_call(
        functools.partial(_out_ffn_kernel, mode=mode, final=final, n_mods=4), grid=(Bx, Tx // tm),
        in_specs=in_specs, out_specs=tok(y), out_shape=jax.ShapeDtypeStruct(y.shape, F32),
        compiler_params=_cp("parallel", "parallel"), name="out_ffn_" + mode)(*args)


def _stack_heads(q, G, R, tq):
    B, T, _ = q.shape
    q = q.reshape(B, T // tq, tq, G, R, HEAD_DIM).transpose(0, 3, 1, 4, 2, 5).reshape(B, G, T // tq, R * tq, HEAD_DIM)
    return jnp.pad(q, ((0, 0),) * 4 + ((0, LANES - HEAD_DIM),))


def _unstack_heads(o, R, tq):
    B, G, nq, _, _ = o.shape
    o = o[..., HEAD_DIM:].reshape(B, G, nq, R, tq, HEAD_DIM).transpose(0, 2, 4, 1, 3, 5)
    return o.reshape(B, nq * tq, G * R * HEAD_DIM)


def _stack_heads_sample(q, G, R, S, s_pad):
    Bd = q.shape[0] // S
    q = q.reshape(Bd, S, G, R, HEAD_DIM).transpose(0, 2, 3, 1, 4)
    q = jnp.pad(q, ((0, 0), (0, 0), (0, 0), (0, s_pad - S), (0, LANES - HEAD_DIM)))
    return q.reshape(Bd, G, R * s_pad, LANES)


def _unstack_heads_sample(o, R, S, s_pad):
    Bd, G, _, _ = o.shape
    o = o[..., HEAD_DIM:].reshape(Bd, G, R, s_pad, HEAD_DIM)[:, :, :, :S].transpose(0, 3, 1, 2, 4)
    return o.reshape(1, Bd * S, G * R * HEAD_DIM)


def _group_rows(t, G):
    H, rows, C = t.shape
    return t.reshape(G, (H // G) * rows, C)


def _even_odd(n):
    return jnp.concatenate([jnp.arange(0, n, 2), jnp.arange(1, n, 2)]).astype(jnp.int32)


def _rope_tables(pos):
    half = MLA_ROPE // 2
    freq = ROPE_THETA ** (-jnp.arange(half, dtype=F32) / half)
    ang = pos.astype(F32)[:, None] * freq[None, :]
    cos, sin = jnp.cos(ang), jnp.sin(ang)
    n = pos.shape[0]
    one, zero = jnp.ones, jnp.zeros
    cq = jnp.concatenate([one((n, MLA_NOPE), F32), cos, cos, zero((n, LANES - MLA_NOPE - MLA_ROPE), F32)], axis=1)
    sq = jnp.concatenate([zero((n, MLA_NOPE), F32), sin, sin, zero((n, LANES - MLA_NOPE - MLA_ROPE), F32)], axis=1)
    n_gate = 3 * NSA_HEADS
    cm = jnp.concatenate([cos, cos, one((n, n_gate), F32), zero((n, LANES - MLA_ROPE - n_gate), F32)], axis=1)
    sm = jnp.concatenate([sin, sin, zero((n, LANES - MLA_ROPE), F32)], axis=1)
    return cq, sq, cm, sm


def _layer0_weights(w_in_0, mla_q_norm, mla_w_uq, mla_kv_norm, mla_w_uk, mla_w_uv):
    D = w_in_0.shape[0]
    qr, kvr = mla_q_norm.shape[0], mla_kv_norm.shape[0]
    half = MLA_ROPE // 2
    o_kr = qr + kvr
    o_q = o_kr + MLA_ROPE
    o_cmp = o_q + NSA_HEADS * HEAD_DIM
    kvw = NSA_KV_HEADS * 2 * HEAD_DIM
    o_g = o_cmp + 3 * kvw
    w_kr = w_in_0[:, o_kr:o_q]
    w_g = w_in_0[:, o_g:]
    z = lambda n: jnp.zeros((D, n), F32)
    misc_a = jnp.concatenate([w_kr, w_g, z(LANES - MLA_ROPE - w_g.shape[1])], axis=1)
    misc_b = jnp.concatenate([-w_kr[:, half:], w_kr[:, :half], z(LANES - MLA_ROPE)], axis=1)
    w0 = jnp.concatenate([w_in_0[:, :o_kr], misc_a, misc_b, w_in_0[:, o_q:o_cmp] * ATTN_SCALE, w_in_0[:, o_cmp:o_g]],
                         axis=1).astype(BF16)
    H = MLA_HEADS
    wq = mla_w_uq.reshape(qr, H, MLA_NOPE + MLA_ROPE)
    nope, x1, x2 = wq[..., :MLA_NOPE], wq[..., MLA_NOPE:MLA_NOPE + half], wq[..., MLA_NOPE + half:]
    zq = lambda n: jnp.zeros((qr, H, n), F32)
    pad = LANES - MLA_NOPE - MLA_ROPE
    wuq = jnp.concatenate([nope, x1, x2, zq(pad)], axis=-1).reshape(qr, H * LANES).astype(BF16)
    wuqs = jnp.concatenate([zq(MLA_NOPE), -x2, x1, zq(pad)], axis=-1).reshape(qr, H * LANES).astype(BF16)
    k_top = jnp.pad(mla_w_uk, ((0, 0), (0, 0), (0, LANES - MLA_NOPE))).reshape(kvr, H * LANES)
    place = jnp.zeros((LANES, H, LANES), F32).at[jnp.arange(MLA_ROPE), :, MLA_NOPE + jnp.arange(MLA_ROPE)].set(1.0)
    kcat = jnp.concatenate([k_top, place.reshape(LANES, H * LANES)], axis=0).astype(BF16)
    even = (jnp.arange(H) % 2 == 0)[None, :, None]
    zv = jnp.zeros((kvr, H, LANES - MLA_V), F32)
    wuv = jnp.where(even, jnp.concatenate([mla_w_uv, zv], axis=-1), jnp.concatenate([zv, mla_w_uv], axis=-1))
    wuv = wuv.reshape(kvr, H * LANES).astype(BF16)
    ones_v, zero_v = jnp.ones((1, H, MLA_V), F32), jnp.zeros((1, H, MLA_V), F32)
    vones = jnp.where(even, jnp.concatenate([zero_v, ones_v], axis=-1), jnp.concatenate([ones_v, zero_v], axis=-1))
    vones = vones.reshape(1, H * LANES)
    width = kvr + MLA_ROPE
    blk = jnp.zeros((H, LANES, width), F32)
    blk = blk.at[:, :MLA_NOPE, :kvr].set(jnp.transpose(mla_w_uk, (1, 2, 0)))
    blk = blk.at[:, MLA_NOPE + jnp.arange(MLA_ROPE), kvr + jnp.arange(MLA_ROPE)].set(1.0)
    eye = jnp.eye(H, dtype=F32)
    a_abs = jnp.einsum("hij,hk->hikj", blk, eye).reshape(H * LANES, H * width).astype(BF16)
    wuv_bd = jnp.einsum("chd,hk->hckd", mla_w_uv, eye).reshape(H * kvr, H * MLA_V).astype(BF16)
    return dict(w0=w0, qn=mla_q_norm.reshape(1, qr), kvn=mla_kv_norm.reshape(1, kvr), wuq=wuq, wuqs=wuqs, kcat=kcat,
                wuv=wuv, vones=vones, a_abs=a_abs, wuv_bd=wuv_bd)


def _gate_expand():
    h = jnp.arange(NSA_HEADS)
    mats = []
    for br in range(3):
        m = jnp.zeros((LANES, NSA_HEADS, HEAD_DIM), F32).at[MLA_ROPE + 3 * h + br, h, :].set(1.0)
        mats.append(m.reshape(LANES, NSA_HEADS * HEAD_DIM))
    return jnp.stack(mats).astype(BF16)


def kernel(x_prompt, x_sample, cache_mla, cache_nsa_cmp, cache_nsa_sel, state_nsa_win, state_swa, page_table, c_prompt, c_sample, rel_bias_table, w_ada_0, b_ada_0, w_in_0, mla_q_norm, mla_w_uq, mla_kv_norm, mla_w_uk, mla_w_uv, nsa_w_cmp, nsa_pe_cmp, w_out_0, w_ffn_in_0, w_ffn_out_0, w_ada_1, b_ada_1, w_in_1, swa_sinks, w_out_1, w_ffn_in_1, w_ffn_out_1, final_norm):
    B, T, D = x_prompt.shape
    Bd, S, _ = x_sample.shape
    n_pool, PAGE, mla_w = cache_mla.shape
    n_pages = page_table.shape[1]
    PAST = n_pages * PAGE
    G, R = NSA_KV_HEADS, NSA_HEADS // NSA_KV_HEADS
    G1, R1 = SWA_KV_HEADS, SWA_HEADS // SWA_KV_HEADS
    tq = Q_BLOCK
    nq = T // tq
    NS = Bd * S
    S_PAD = 8
    tm_p = 512 if T % 512 == 0 else 256
    tm_s = 256 if NS % 256 == 0 else NS
    i32 = jnp.int32
    table = rel_bias_table.astype(F32)

    n_c = B + Bd
    c_all = jnp.pad(jnp.concatenate([c_prompt, c_sample], axis=0), ((0, (-n_c) % 8), (0, 0)))

    def mods_for(w_ada, b_ada):
        m = _ada(c_all, w_ada.astype(BF16), b_ada.reshape(1, -1))
        mp = [m[:B, k * D:(k + 1) * D][:, None, :] for k in range(6)]
        ms = [jnp.repeat(m[B:B + Bd, k * D:(k + 1) * D], S, axis=0)[None] for k in range(6)]
        return mp, ms

    mods0_p, mods0_s = mods_for(w_ada_0, b_ada_0)
    mods1_p, mods1_s = mods_for(w_ada_1, b_ada_1)
    xs = x_sample.reshape(1, NS, D)

    w0 = _layer0_weights(w_in_0, mla_q_norm, mla_w_uq, mla_kv_norm, mla_w_uk, mla_w_uv)
    tabs_p = _rope_tables(jnp.arange(T, dtype=i32))
    tabs_s = tuple(jnp.tile(t, (Bd, 1)) for t in _rope_tables(PAST + jnp.arange(S, dtype=i32)))
    (mla_p, q_mla, k_mla, v_mla, q_stack_p, cmp_p, sel_p, win_p, selb_p, winb_p, gate_p) = _proj0(
        x_prompt, mods0_p[0], mods0_p[1], w0, tabs_p, False, tm_p)
    (mla_s, qcat_s, qn_s, cmp_s, sel_s, win_s, gate_s) = _proj0(xs, mods0_s[0], mods0_s[1], w0, tabs_s, True, tm_s)

    o_a_p = _mla_prompt(q_mla, k_mla, v_mla, 512 if T % 512 == 0 else 256)

    eye2 = jnp.eye(2, dtype=F32)
    wbig = jnp.einsum("lcde,gh,ck->lgcdhke", nsa_w_cmp, eye2, eye2).reshape(NSA_CMP_BLOCK * 4 * HEAD_DIM, 4 * HEAD_DIM)
    wbig = wbig.astype(BF16)
    pe_rows = jnp.broadcast_to(nsa_pe_cmp[:, None], (NSA_CMP_BLOCK, G, 2, HEAD_DIM)).reshape(1, -1)
    pe_rows = jnp.broadcast_to(pe_rows, (8, pe_rows.shape[1]))
    pe_bias = _pe_bias(pe_rows, wbig)
    NC = T // NSA_CMP_BLOCK
    kvc_p = _compress_prompt(cmp_p.reshape(B, NC, -1), wbig, pe_bias)
    order_p = _even_odd(NC)
    kvc_p = kvc_p[:, order_p]

    qpos = jnp.arange(T, dtype=i32)
    dist = qpos[:, None] - (order_p * NSA_CMP_BLOCK + NSA_CMP_BLOCK - 1)[None, :]
    bias_cmp_p = _bias_tiles(table, _masked_bucket(dist, dist >= 0).T, NSA_HEADS)
    n_sel = T // NSA_SEL_BLOCK
    blk = jnp.arange(n_sel, dtype=i32)[None, :]
    cur = (qpos // NSA_SEL_BLOCK)[:, None]
    forced = (blk == 0) | (blk == cur) | (blk == cur - 1)
    causal = blk * NSA_SEL_BLOCK <= qpos[:, None]
    code_p = jnp.where(causal, jnp.where(forced, SELECT_BIG, 0.0), -SELECT_BIG).astype(F32).reshape(nq, tq, n_sel)
    assert n_sel <= LANES and n_sel == NC // 2, "selection blocks must fit one lane tile"
    o_cmp_p, sel_rows_p = _cmp_select_t(q_stack_p, kvc_p, kvc_p.transpose(0, 2, 1), bias_cmp_p,
                                        code_p.transpose(0, 2, 1), tq)

    onehot_p = (jnp.arange(LANES, dtype=i32)[None, :] == (qpos // NSA_SEL_BLOCK)[:, None]).astype(BF16)
    ql = jnp.arange(tq, dtype=i32)[:, None]
    d_near = ql + tq - jnp.arange(2 * tq, dtype=i32)[None, :]
    bk_near = jnp.concatenate([_masked_bucket(d_near, d_near >= 0), jnp.full((tq, 2 * tq), REL_BUCKETS - 1, i32)], axis=0)
    near_far = _bias_tiles(table, bk_near, NSA_HEADS)
    bias_near_p = _group_rows(near_far[:, :tq], G)
    bias_far_p = _group_rows(near_far[:, tq:, :LANES], G)
    tk_sel = 512 if T % 512 == 0 else tq
    o_sel_st = _sel_prompt(q_stack_p, sel_rows_p, selb_p, onehot_p, bias_near_p, bias_far_p, tq, tk_sel)

    def window_bias(window, n_heads, n_groups):
        nw = -(-window // tq)
        dw = ql + nw * tq - jnp.arange((nw + 1) * tq, dtype=i32)[None, :]
        t = _bias_tiles(table, _masked_bucket(dw, (dw >= 0) & (dw < window)), n_heads)
        return _group_rows(t, n_groups), nw

    bias_win_p, nw0 = window_bias(NSA_WINDOW, NSA_HEADS, G)
    no_sink0 = jnp.full((G, R * tq, 1), NEG, F32)
    o_win_st = _banded(q_stack_p, winb_p, bias_win_p, no_sink0, tq, nw0)

    o_sel_p, o_win_p = o_sel_st, o_win_st

    pps = 8 if n_pages % 8 == 0 else n_pages
    srow = jnp.arange(S_PAD, dtype=i32)
    s_real = jnp.minimum(srow, S - 1)
    q_cat = qcat_s.reshape(Bd, S * MLA_HEADS, mla_w)
    mla_new = jnp.pad(mla_s.reshape(Bd, S, mla_w), ((0, 0), (0, S_PAD - S), (0, 0)))
    s_of_row = jnp.repeat(jnp.arange(S, dtype=i32), MLA_HEADS)[:, None]
    new_mask = jnp.where((srow[None, :] <= s_of_row) & (srow[None, :] < S), 0.0, NEG).astype(F32)
    pps_m = 16 if n_pages % 16 == 0 else pps
    o_full_s = _mla_decode(page_table, q_cat, cache_mla.transpose(0, 2, 1), mla_new, new_mask, pps_m)
    o_full_s = o_full_s.reshape(1, NS, MLA_HEADS * (mla_w - MLA_ROPE))

    pps_c = 32 if n_pages % 32 == 0 else pps_m
    feat = G * 2 * HEAD_DIM
    cache_cmp_t = cache_nsa_cmp.transpose(0, 2, 3, 4, 1).reshape(n_pool, feat, PAGE)
    w_head = wbig.reshape(NSA_CMP_BLOCK, feat, feat)[:, :LANES, :LANES]
    kvc_s = _compress_paged(page_table, cache_cmp_t, w_head, pe_bias, pps_c)
    NCs = PAST // NSA_CMP_BLOCK
    order_s = _even_odd(NCs)
    kvc_s = kvc_s[:, order_s]
    q_stack_s = _stack_heads_sample(qn_s[0], G, R, S, S_PAD)
    pos_s = PAST + s_real
    dist_s = pos_s[:, None] - (order_s * NSA_CMP_BLOCK + NSA_CMP_BLOCK - 1)[None, :]
    bias_cmp_s = _group_rows(_bias_tiles(table, _masked_bucket(dist_s, dist_s >= 0), NSA_HEADS), G)[None]
    n_past_blk = PAST // NSA_SEL_BLOCK
    n_sel_s = n_past_blk + -(-S // NSA_SEL_BLOCK)
    sel_lanes = -(-n_sel_s // LANES) * LANES
    blk_s = jnp.arange(sel_lanes, dtype=i32)[None, :]
    cur_s = (pos_s // NSA_SEL_BLOCK)[:, None]
    forced_s = (blk_s == 0) | (blk_s == cur_s) | (blk_s == cur_s - 1)
    causal_s = (blk_s * NSA_SEL_BLOCK <= pos_s[:, None]) & (blk_s < n_sel_s)
    code_s = jnp.where(causal_s, jnp.where(forced_s, SELECT_BIG, 0.0), -SELECT_BIG).astype(F32)[None]
    o_cmp_ss, sel_mask_s = _cmp_select(q_stack_s[:, :, None], kvc_s, bias_cmp_s, code_s, S_PAD, n_sel_s, PAST)
    o_cmp_s = _unstack_heads_sample(o_cmp_ss[:, :, 0], R, S, S_PAD)

    assert n_past_blk <= LANES and PAGE >= REL_MAX_DISTANCE, "past selection blocks must fit one lane tile"
    sel_past = jnp.pad(sel_mask_s[:, :, 0, :, :n_past_blk], ((0, 0),) * 3 + ((0, LANES - n_past_blk),))
    q_aug_s = jnp.concatenate([q_stack_s, jnp.tile(sel_past, (1, 1, R, 1))], axis=-1)
    key_blk_s = (jnp.arange(PAST, dtype=i32) // NSA_SEL_BLOCK).reshape(n_pages, 1, PAGE)
    onehot_s = (jnp.arange(LANES, dtype=i32)[None, :, None] == key_blk_s).astype(BF16)
    d_last = pos_s[:, None] - (PAST - PAGE + jnp.arange(PAGE, dtype=i32))[None, :]
    bk_last = jnp.concatenate([_masked_bucket(d_last, d_last >= 0), jnp.full((S_PAD, PAGE), REL_BUCKETS - 1, i32)], axis=0)
    last_far = _bias_tiles(table, bk_last, NSA_HEADS)
    bias_last_s = _group_rows(last_far[:, :S_PAD], G)
    bias_far_s = _group_rows(last_far[:, S_PAD:], G)
    d_new = s_real[:, None] - srow[None, :]
    bk_new = _masked_bucket(d_new, (d_new >= 0) & (srow[None, :] < S))
    nbias_nsa = _group_rows(_bias_tiles(table, bk_new, NSA_HEADS), G)
    sel_new8 = jnp.pad(sel_s.reshape(Bd, S, -1), ((0, 0), (0, S_PAD - S), (0, 0)))
    cache_sel_t = cache_nsa_sel.transpose(0, 2, 3, 4, 1).reshape(n_pool, feat, PAGE)
    flat = lambda a: a.reshape(-1, a.shape[-1])
    o_sel_ss = _sel_decode(page_table, q_aug_s, onehot_s, flat(bias_last_s), flat(bias_far_s), cache_sel_t, sel_new8,
                           flat(nbias_nsa), pps_m)
    o_sel_s = _unstack_heads_sample(o_sel_ss, R, S, S_PAD)

    def state_bias(wb, window, n_heads, n_groups):
        d = wb + s_real[:, None] - jnp.arange(wb, dtype=i32)[None, :]
        return _group_rows(_bias_tiles(table, _masked_bucket(d, d < window), n_heads), n_groups)

    wb0 = state_nsa_win.shape[1]
    win_new8 = jnp.pad(win_s.reshape(Bd, S, -1), ((0, 0), (0, S_PAD - S), (0, 0)))
    no_sink0_s = jnp.full((G, R * S_PAD, 1), NEG, F32)
    o_win_ss = _win_decode(q_stack_s, state_nsa_win.transpose(0, 2, 3, 4, 1).reshape(Bd, feat, wb0), win_new8,
                           state_bias(wb0, NSA_WINDOW, NSA_HEADS, G), nbias_nsa, no_sink0_s)
    o_win_s = _unstack_heads_sample(o_win_ss, R, S, S_PAD)

    eg = _gate_expand()
    w_out0 = w_out_0.astype(BF16)
    f = w_ffn_out_0.shape[0]
    ffn0 = (w_ffn_in_0[:, :f].astype(BF16), w_ffn_in_0[:, f:].astype(BF16), w_ffn_out_0.astype(BF16))
    y1_p = _out_ffn(x_prompt, [o_a_p, o_cmp_p, o_sel_p, o_win_p, gate_p], [eg, w_out0], mods0_p[2:], ffn0, None,
                    "ab_prompt", tm_p)
    y1_s = _out_ffn(xs, [o_full_s, o_cmp_s, o_sel_s, o_win_s, gate_s], [eg, w0["wuv_bd"], w_out0], mods0_s[2:], ffn0,
                    None, "ab_sample", tm_s)

    nq1 = SWA_HEADS * HEAD_DIM
    w1 = jnp.concatenate([w_in_1[:, :nq1] * ATTN_SCALE, w_in_1[:, nq1:]], axis=1).astype(BF16)
    q1_p, kv1_p, kvb1_p = _proj1(y1_p, mods1_p[0], mods1_p[1], w1, nq1, tm_p, True)
    q1_s, kv1_s, _ = _proj1(y1_s, mods1_s[0], mods1_s[1], w1, nq1, tm_s, False)
    bias_swa_p, nw1 = window_bias(SWA_WINDOW, SWA_HEADS, G1)
    sink_p = jnp.repeat(swa_sinks.astype(F32).reshape(G1, R1), tq, axis=1).reshape(G1, R1 * tq, 1)
    o_c_p = _banded(q1_p, kvb1_p, bias_swa_p, sink_p, tq, nw1)

    wb1 = state_swa.shape[1]
    kv_new8 = jnp.pad(kv1_s.reshape(Bd, S, -1), ((0, 0), (0, S_PAD - S), (0, 0)))
    d_new1 = s_real[:, None] - srow[None, :]
    nbias_swa = _group_rows(_bias_tiles(table, _masked_bucket(d_new1, (d_new1 >= 0) & (srow[None, :] < S)), SWA_HEADS),
                            G1)
    sink_s = jnp.repeat(swa_sinks.astype(F32).reshape(G1, R1), S_PAD, axis=1).reshape(G1, R1 * S_PAD, 1)
    state_swa_t = state_swa.transpose(0, 2, 3, 4, 1).reshape(Bd, G1 * 2 * HEAD_DIM, wb1)
    o_c_ss = _win_decode(_stack_heads_sample(q1_s[0], G1, R1, S, S_PAD), state_swa_t, kv_new8,
                         state_bias(wb1, SWA_WINDOW, SWA_HEADS, G1), nbias_swa, sink_s)
    o_c_s = _unstack_heads_sample(o_c_ss, R1, S, S_PAD)

    w_out1 = w_out_1.astype(BF16)
    f1 = w_ffn_out_1.shape[0]
    ffn1 = (w_ffn_in_1[:, :f1].astype(BF16), w_ffn_in_1[:, f1:].astype(BF16), w_ffn_out_1.astype(BF16))
    gain = final_norm.reshape(1, D).astype(F32)
    y_prompt = _out_ffn(y1_p, [o_c_p], [w_out1], mods1_p[2:], ffn1, gain, "c", tm_p)
    y_sample = _out_ffn(y1_s, [o_c_s], [w_out1], mods1_s[2:], ffn1, gain, "c", tm_s).reshape(Bd, S, D)

    row5 = lambda a, lead, g: a.reshape(lead + (g, 2, HEAD_DIM))
    win_p5 = row5(win_p, (B, T), G)
    kv1_p5 = row5(kv1_p, (B, T), G1)
    win_s5 = row5(win_s, (Bd, S), G)
    kv1_s5 = row5(kv1_s, (Bd, S), G1)
    return (y_prompt, y_sample, mla_p, mla_s.reshape(Bd, S, mla_w),
            row5(cmp_p, (B, T), G), row5(cmp_s, (Bd, S), G), row5(sel_p, (B, T), G), row5(sel_s, (Bd, S), G),
            win_p5[:, T - min(NSA_WINDOW, T):], jnp.concatenate([state_nsa_win, win_s5], axis=1)[:, S:],
            kv1_p5[:, T - min(SWA_WINDOW, T):], jnp.concatenate([state_swa, kv1_s5], axis=1)[:, S:])
```

```python
import functools
import math

import jax
import jax.numpy as jnp
import numpy as np
from jax import lax
from jax.experimental import pallas as pl
from jax.experimental.pallas import tpu as pltpu

F32 = jnp.float32
BF16 = jnp.bfloat16

MLA_HEADS, MLA_NOPE, MLA_ROPE, MLA_V = 8, 64, 32, 64
ROPE_THETA = 10000.0
NSA_HEADS, NSA_KV_HEADS, HEAD_DIM = 8, 2, 64
NSA_CMP_BLOCK, NSA_SEL_BLOCK, NSA_TOP_N, NSA_WINDOW = 32, 64, 16, 512
SWA_HEADS, SWA_KV_HEADS, SWA_WINDOW = 16, 4, 128
REL_BUCKETS, REL_MAX_DISTANCE = 32, 128
Q_BLOCK = 128
NORM_EPS = 1e-6
NEG = -1e30
SELECT_BIG = 1e9
MLA_SCALE = (MLA_NOPE + MLA_ROPE) ** -0.5
ATTN_SCALE = HEAD_DIM ** -0.5

LANES = 128
VMEM_LIMIT = 52 * 1024 * 1024


def _cp(*sem):
    return pltpu.CompilerParams(dimension_semantics=sem, vmem_limit_bytes=VMEM_LIMIT)


def _dot(a, b):
    return jnp.dot(a, b, preferred_element_type=F32)


def _dot_t(a, b):
    return lax.dot_general(a, b, (((1,), (1,)), ((), ())), preferred_element_type=F32)


def _rms(x):
    return x * lax.rsqrt(jnp.mean(x * x, axis=-1, keepdims=True) + NORM_EPS)


def _sigmoid(x):
    return 1.0 / (1.0 + jnp.exp(-x))


def _const_spec(shape):
    n = len(shape)
    return pl.BlockSpec(shape, lambda *_: (0,) * n)


def _softmax_update(scores, values, m_ref, l_ref, acc_ref, transposed_values=False):
    m_prev = m_ref[...]
    m_new = m_prev
    for s in scores:
        m_new = jnp.maximum(m_new, jnp.max(s, axis=-1, keepdims=True))
    alpha = jnp.exp(m_prev - m_new)
    l_new = alpha * l_ref[...]
    acc = alpha * acc_ref[...]
    for s, v in zip(scores, values):
        p = jnp.exp(s - m_new)
        l_new = l_new + jnp.sum(p, axis=-1, keepdims=True)
        pb = p.astype(BF16)
        acc = acc + (_dot_t(pb, v) if transposed_values else _dot(pb, v))
    m_ref[...] = m_new
    l_ref[...] = l_new
    acc_ref[...] = acc


def _lane_tile(x, width):
    reps = width // x.shape[-1]
    return x if reps == 1 else jnp.concatenate([x] * reps, axis=1)


def _flash_step(scores, values, m_ref, acc_ref):
    m_prev = m_ref[...]
    m_cur = None
    for s in scores:
        mx = jnp.max(s, axis=-1, keepdims=True)
        m_cur = mx if m_cur is None else jnp.maximum(m_cur, mx)
    m_new = jnp.maximum(m_prev, m_cur)
    acc = acc_ref[...] * _lane_tile(jnp.exp(m_prev - m_new), acc_ref.shape[-1])
    for s, v in zip(scores, values):
        p = jnp.exp(s - _lane_tile(m_new, s.shape[-1]))
        acc = acc + _dot(p.astype(BF16), v)
    m_ref[...] = m_new
    acc_ref[...] = acc


N_CHAINS = 1


def _merge_chains(m_ref, l_ref, acc_ref):
    m = m_ref[0]
    for k in range(1, m_ref.shape[0]):
        m = jnp.maximum(m, m_ref[k])
    l = acc = None
    for k in range(m_ref.shape[0]):
        w = jnp.exp(m_ref[k] - m)
        l = w * l_ref[k] if l is None else l + w * l_ref[k]
        acc = w * acc_ref[k] if acc is None else acc + w * acc_ref[k]
    return l, acc


def _softmax_init(m_ref, l_ref, acc_ref):
    m_ref[...] = jnp.full(m_ref.shape, NEG, F32)
    l_ref[...] = jnp.zeros(l_ref.shape, F32)
    acc_ref[...] = jnp.zeros(acc_ref.shape, F32)


def _ada_kernel(c_ref, w_ref, b_ref, o_ref):
    c = c_ref[...]
    a = (c * _sigmoid(c)).astype(BF16)
    o_ref[...] = _dot(a, w_ref[...]) + b_ref[...]


def _ada(c, w, b):
    M, D = c.shape
    N = w.shape[1]
    tn = 1024 if N % 1024 == 0 else N
    return pl.pallas_call(
        _ada_kernel, grid=(N // tn,),
        in_specs=[pl.BlockSpec((M, D), lambda j: (0, 0)), pl.BlockSpec((D, tn), lambda j: (0, j)),
                  pl.BlockSpec((1, tn), lambda j: (0, j))],
        out_specs=pl.BlockSpec((M, tn), lambda j: (0, j)),
        out_shape=jax.ShapeDtypeStruct((M, N), F32), compiler_params=_cp("parallel"), name="ada")(c, w, b)


def _bias_kernel(tab_ref, bkt_ref, o_ref):
    h = pl.program_id(0)
    bkt = bkt_ref[...]
    acc = jnp.full(bkt.shape, NEG, F32)
    for b in range(REL_BUCKETS):
        acc = jnp.where(bkt == b, tab_ref[b, h], acc)
    o_ref[0] = acc


def _bias_tiles(table, buckets, n_heads):
    R, C = buckets.shape
    tr = R
    for cand in (512, 256, 128):
        if R > cand and R % cand == 0:
            tr = cand
            break
    return pl.pallas_call(
        _bias_kernel,
        grid_spec=pltpu.PrefetchScalarGridSpec(
            num_scalar_prefetch=1, grid=(n_heads, R // tr),
            in_specs=[pl.BlockSpec((tr, C), lambda h, r, tab: (r, 0))],
            out_specs=pl.BlockSpec((1, tr, C), lambda h, r, tab: (h, r, 0))),
        out_shape=jax.ShapeDtypeStruct((n_heads, R, C), F32),
        compiler_params=_cp("parallel", "parallel"), name="rel_bias")(table, buckets)


def _bucket_thresholds():
    exact = REL_BUCKETS // 2
    n = np.arange(exact, REL_MAX_DISTANCE + 1)
    scaled = np.log(n / exact) / math.log(REL_MAX_DISTANCE / exact)
    large = np.minimum(exact + np.trunc(scaled * (REL_BUCKETS - exact)).astype(np.int64), REL_BUCKETS - 1)
    return [int(n[np.argmax(large >= b)]) for b in range(exact + 1, REL_BUCKETS)]


def _t5_bucket(dist):
    n = jnp.maximum(dist, 0)
    exact = REL_BUCKETS // 2
    large = jnp.full(n.shape, exact, jnp.int32)
    for thr in _bucket_thresholds():
        large = large + jnp.where(n >= thr, 1, 0)
    return jnp.where(n < exact, n, large)


def _masked_bucket(dist, valid):
    return jnp.where(valid, _t5_bucket(dist), -1).astype(jnp.int32)


def _modulate(x, shift, scale):
    return _rms(x) * (1.0 + scale) + shift


def _store_stacked_heads(q, ref):
    _, G, n_tiles, M, _ = ref.shape
    R = M * n_tiles // q.shape[0]
    tq = M // R
    lane = lax.broadcasted_iota(jnp.int32, (q.shape[0], LANES), 1)
    for h in range(G * R):
        pair = q[:, LANES * (h // 2):LANES * (h // 2 + 1)]
        if h % 2:
            pair = pltpu.roll(pair, HEAD_DIM, 1)
        head = jnp.where(lane < HEAD_DIM, pair, 0.0).astype(ref.dtype)
        g, r = divmod(h, R)
        for t in range(n_tiles):
            ref[0, g, t, r * tq:(r + 1) * tq, :] = head[t * tq:(t + 1) * tq]


def _store_heads(o, o_ref, g, tq):
    R = o.shape[0] // tq
    lane = lax.broadcasted_iota(jnp.int32, (tq, LANES), 1)
    for p in range(R // 2):
        even = o[(2 * p) * tq:(2 * p + 1) * tq]
        odd = o[(2 * p + 1) * tq:(2 * p + 2) * tq]
        tile = jnp.where(lane < HEAD_DIM, pltpu.roll(even, HEAD_DIM, 1), odd)
        col = (g * R + 2 * p) * HEAD_DIM
        o_ref[0, :, col:col + LANES] = tile.astype(o_ref.dtype)


def _proj0_kernel(x_ref, sh_ref, sc_ref, w0_ref, qn_ref, kvn_ref, wuq_ref, wuqs_ref, cq_ref, sq_ref, cm_ref,
                  sm_ref, wa_ref, wb_ref, vones_ref, *outs, sample):
    h = _modulate(x_ref[0], sh_ref[0], sc_ref[0]).astype(BF16)
    y = _dot(h, w0_ref[...])
    qn = (_rms(y[:, 0:256]) * qn_ref[...]).astype(BF16)
    ckv = _rms(y[:, 256:512]) * kvn_ref[...]
    misc = y[:, 512:640] * cm_ref[...] + y[:, 640:768] * sm_ref[...]
    cq = jnp.concatenate([cq_ref[...]] * MLA_HEADS, axis=1)
    sq = jnp.concatenate([sq_ref[...]] * MLA_HEADS, axis=1)
    q_rot = ((_dot(qn, wuq_ref[...]) * cq + _dot(qn, wuqs_ref[...]) * sq) * MLA_SCALE).astype(BF16)
    ckv_b = ckv.astype(BF16)
    if sample:
        rows_ref, qcat_ref, qnsa_ref, cmp_ref, sel_ref, win_ref, gate_ref = outs
        qcat_ref[0] = _dot(q_rot, wa_ref[...]).astype(BF16)
    else:
        rows_ref, qmla_ref, kcat_ref, vmla_ref, qnsa_ref, cmp_ref, sel_ref, win_ref, selb_ref, winb_ref, gate_ref = outs
        qmla_ref[0] = q_rot
        kcat_ref[0] = _dot(jnp.concatenate([ckv_b, misc.astype(BF16)], axis=1), wa_ref[...]).astype(BF16)
        vmla_ref[0] = (_dot(ckv_b, wb_ref[...]) + vones_ref[...]).astype(BF16)
        selb_ref[0] = y[:, 1536:1792].astype(BF16)
        winb_ref[0] = y[:, 1792:2048].astype(BF16)
    rows_ref[0, :, 0:256] = ckv
    rows_ref[0, :, 256:288] = misc[:, 0:MLA_ROPE]
    if sample:
        qnsa_ref[0] = y[:, 768:1280].astype(BF16)
    else:
        _store_stacked_heads(y[:, 768:1280], qnsa_ref)
    cmp_ref[0] = y[:, 1280:1536]
    sel_ref[0] = y[:, 1536:1792]
    win_ref[0] = y[:, 1792:2048]
    gate_ref[0] = _sigmoid(misc)


def _proj0(x, shift, scale, w, tabs, sample, tm):
    Bx, Tx, D = x.shape
    per_token = shift.shape[1] != 1
    mod_spec = (pl.BlockSpec((1, tm, D), lambda b, i: (b, i, 0)) if per_token
                else pl.BlockSpec((1, 1, D), lambda b, i: (b, 0, 0)))
    tok = lambda n: pl.BlockSpec((1, tm, n), lambda b, i: (b, i, 0))
    tab_spec = pl.BlockSpec((tm, LANES), lambda b, i: (i, 0))
    wa, wb = (w["a_abs"], w["wuv"]) if sample else (w["kcat"], w["wuv"])
    in_specs = [tok(D), mod_spec, mod_spec, _const_spec(w["w0"].shape), _const_spec((1, 256)), _const_spec((1, 256)),
                _const_spec(w["wuq"].shape), _const_spec(w["wuqs"].shape), tab_spec, tab_spec, tab_spec, tab_spec,
                _const_spec(wa.shape), _const_spec(wb.shape), _const_spec(w["vones"].shape)]
    sd = lambda n, dt: jax.ShapeDtypeStruct((Bx, Tx, n), dt)
    if sample:
        out_shape = [sd(288, F32), sd(MLA_HEADS * 288, BF16), sd(512, BF16), sd(256, F32), sd(256, F32), sd(256, F32),
                     sd(LANES, F32)]
    else:
        out_shape = [sd(288, F32), sd(1024, BF16), sd(1024, BF16), sd(1024, BF16), sd(512, BF16), sd(256, F32),
                     sd(256, F32), sd(256, F32), sd(256, BF16), sd(256, BF16), sd(LANES, F32)]
    out_specs = [tok(s.shape[-1]) for s in out_shape]
    if not sample:
        G, R, nt = NSA_KV_HEADS, NSA_HEADS // NSA_KV_HEADS, tm // Q_BLOCK
        out_shape[4] = jax.ShapeDtypeStruct((Bx, G, Tx // Q_BLOCK, R * Q_BLOCK, LANES), BF16)
        out_specs[4] = pl.BlockSpec((1, G, nt, R * Q_BLOCK, LANES), lambda b, i: (b, 0, i, 0, 0))
    return pl.pallas_call(
        functools.partial(_proj0_kernel, sample=sample), grid=(Bx, Tx // tm), in_specs=in_specs, out_specs=out_specs,
        out_shape=out_shape, compiler_params=_cp("parallel", "parallel"),
        name="proj0_sample" if sample else "proj0_prompt")(
            x, shift, scale, w["w0"], w["qn"], w["kvn"], w["wuq"], w["wuqs"], *tabs, wa, wb, w["vones"])


def _proj1_kernel(x_ref, sh_ref, sc_ref, w_ref, q_ref, kv_ref, kvb_ref, *, nq, stacked):
    h = _modulate(x_ref[0], sh_ref[0], sc_ref[0]).astype(BF16)
    y = _dot(h, w_ref[...])
    if stacked:
        _store_stacked_heads(y[:, :nq], q_ref)
    else:
        q_ref[0] = y[:, :nq].astype(BF16)
    kv_ref[0] = y[:, nq:]
    kvb_ref[0] = y[:, nq:].astype(BF16)


def _proj1(x, shift, scale, w1, nq, tm, stacked):
    Bx, Tx, D = x.shape
    nkv = w1.shape[1] - nq
    per_token = shift.shape[1] != 1
    mod_spec = (pl.BlockSpec((1, tm, D), lambda b, i: (b, i, 0)) if per_token
                else pl.BlockSpec((1, 1, D), lambda b, i: (b, 0, 0)))
    tok = lambda n: pl.BlockSpec((1, tm, n), lambda b, i: (b, i, 0))
    q_shape, q_spec = jax.ShapeDtypeStruct((Bx, Tx, nq), BF16), tok(nq)
    if stacked:
        G, R, nt = SWA_KV_HEADS, SWA_HEADS // SWA_KV_HEADS, tm // Q_BLOCK
        q_shape = jax.ShapeDtypeStruct((Bx, G, Tx // Q_BLOCK, R * Q_BLOCK, LANES), BF16)
        q_spec = pl.BlockSpec((1, G, nt, R * Q_BLOCK, LANES), lambda b, i: (b, 0, i, 0, 0))
    return pl.pallas_call(
        functools.partial(_proj1_kernel, nq=nq, stacked=stacked), grid=(Bx, Tx // tm),
        in_specs=[tok(D), mod_spec, mod_spec, _const_spec(w1.shape)], out_specs=[q_spec, tok(nkv), tok(nkv)],
        out_shape=[q_shape, jax.ShapeDtypeStruct((Bx, Tx, nkv), F32), jax.ShapeDtypeStruct((Bx, Tx, nkv), BF16)],
        compiler_params=_cp("parallel", "parallel"), name="proj1")(x, shift, scale, w1)


def _mla_prompt_kernel(q_ref, k_ref, v_ref, o_ref, m_sc, acc_sc, *, tq):
    i = pl.program_id(2)
    q = q_ref[0]
    causal = lax.broadcasted_iota(jnp.int32, (tq, tq), 1) <= lax.broadcasted_iota(jnp.int32, (tq, tq), 0)
    m_sc[...] = jnp.full(m_sc.shape, NEG, F32)
    acc_sc[...] = jnp.zeros(acc_sc.shape, F32)

    def step(j, masked):
        start = pl.multiple_of(j * tq, tq)
        kk = k_ref[0, pl.ds(start, tq), :]
        vv = v_ref[0, pl.ds(start, tq), :]
        for e in range(2):
            sl = slice(LANES * e, LANES * (e + 1))
            s = _dot_t(q[:, sl], kk[:, sl])
            if masked:
                s = jnp.where(causal, s, NEG)
            _flash_step([s], [vv[:, sl]], m_sc.at[e], acc_sc.at[e])

    def body(j, carry):
        step(j, False)
        return carry

    lax.fori_loop(0, i, body, 0)
    step(i, True)
    a0, a1 = acc_sc[0], acc_sc[1]
    o0 = a0 / pltpu.roll(a0, MLA_V, 1)
    o1 = a1 / pltpu.roll(a1, MLA_V, 1)
    lane = lax.broadcasted_iota(jnp.int32, (tq, LANES), 1)
    o_ref[0] = jnp.where(lane < MLA_V, o0, o1).astype(o_ref.dtype)


def _mla_prompt(q, k, v, tq):
    B, T, _ = q.shape
    return pl.pallas_call(
        functools.partial(_mla_prompt_kernel, tq=tq), grid=(B, MLA_HEADS // 2, T // tq),
        in_specs=[pl.BlockSpec((1, tq, 2 * LANES), lambda b, h, i: (b, i, h)),
                  pl.BlockSpec((1, T, 2 * LANES), lambda b, h, i: (b, 0, h)),
                  pl.BlockSpec((1, T, 2 * LANES), lambda b, h, i: (b, 0, h))],
        out_specs=pl.BlockSpec((1, tq, LANES), lambda b, h, i: (b, i, h)),
        out_shape=jax.ShapeDtypeStruct((B, T, MLA_HEADS * MLA_V), BF16),
        scratch_shapes=[pltpu.VMEM((2, tq, LANES), F32), pltpu.VMEM((2, tq, LANES), F32)],
        compiler_params=_cp("parallel", "parallel", "parallel"), name="mla_prompt")(q, k, v)


def _pe_bias_kernel(pe_ref, w_ref, o_ref):
    o_ref[...] = _dot(pe_ref[...].astype(BF16), w_ref[...])


def _pe_bias(pe_rows, wbig):
    return pl.pallas_call(_pe_bias_kernel, out_shape=jax.ShapeDtypeStruct((pe_rows.shape[0], wbig.shape[1]), F32),
                          compiler_params=_cp(), name="pe_bias")(pe_rows, wbig)


def _compress_prompt_kernel(x_ref, w_ref, peb_ref, o_ref):
    o_ref[0] = (_dot(x_ref[0].astype(BF16), w_ref[...]) + peb_ref[0:1, :]).astype(o_ref.dtype)


def _compress_prompt(x, wbig, pe_bias):
    B, NC, K = x.shape
    return pl.pallas_call(
        _compress_prompt_kernel, grid=(B,),
        in_specs=[pl.BlockSpec((1, NC, K), lambda b: (b, 0, 0)), _const_spec(wbig.shape), _const_spec(pe_bias.shape)],
        out_specs=pl.BlockSpec((1, NC, 256), lambda b: (b, 0, 0)),
        out_shape=jax.ShapeDtypeStruct((B, NC, 256), BF16), compiler_params=_cp("parallel"),
        name="compress_prompt")(x, wbig, pe_bias)


def _compress_paged_kernel(pt_ref, *refs, pps):
    page_refs = refs[:pps]
    perm_ref, w_ref, peb_ref, o_ref = refs[pps:]
    pairs = []
    for pr in range(pps // 2):
        both = jnp.concatenate([page_refs[2 * pr][0], page_refs[2 * pr + 1][0]], axis=1).astype(BF16)
        pairs.append(_dot_t(perm_ref[...], both))
    n_blk = o_ref.shape[1]
    acc = None
    for l in range(NSA_CMP_BLOCK):
        x = jnp.concatenate([p[8 * l:8 * (l + 1)] for p in pairs], axis=0)
        x = jnp.concatenate([x[:, :LANES], x[:, LANES:]], axis=0).astype(BF16)
        part = _dot(x, w_ref[l])
        acc = part if acc is None else acc + part
    y = jnp.concatenate([acc[:n_blk], acc[n_blk:]], axis=1)
    o_ref[0] = (y + peb_ref[0:1, :]).astype(o_ref.dtype)


def _compress_paged(page_table, cache_t, w_head, pe_bias, pps):
    Bd, n_pages = page_table.shape
    _, feat, page = cache_t.shape
    per_page = page // NSA_CMP_BLOCK
    assert per_page * 2 == 8 and feat == 2 * LANES, "a pair of pages must hold one sublane tile of blocks"
    n_blk = pps * per_page
    r = jnp.arange(2 * page)
    l, p2, n = r // 8, (r // per_page) % 2, r % per_page
    perm = (jnp.arange(2 * page)[None, :] == (p2 * page + n * NSA_CMP_BLOCK + l)[:, None]).astype(BF16)
    page_specs = [pl.BlockSpec((1, feat, page), lambda b, c, pt, k=k: (pt[b, c * pps + k], 0, 0)) for k in range(pps)]
    cst = lambda a: pl.BlockSpec(a.shape, lambda b, c, pt: (0,) * a.ndim)
    return pl.pallas_call(
        functools.partial(_compress_paged_kernel, pps=pps),
        grid_spec=pltpu.PrefetchScalarGridSpec(
            num_scalar_prefetch=1, grid=(Bd, n_pages // pps),
            in_specs=page_specs + [cst(perm), cst(w_head), cst(pe_bias)],
            out_specs=pl.BlockSpec((1, n_blk, feat), lambda b, c, pt: (b, c, 0))),
        out_shape=jax.ShapeDtypeStruct((Bd, n_pages * per_page, feat), BF16),
        compiler_params=_cp("parallel", "parallel"), name="compress_paged")(
            page_table, *([cache_t] * pps), perm, w_head, pe_bias)


def _cmp_select_kernel(q_ref, kv_ref, bias_ref, code_ref, o_ref, sel_ref, *, tq, n_rank, first_pos):
    i = pl.program_id(1)
    G, R = NSA_KV_HEADS, NSA_HEADS // NSA_KV_HEADS
    M = R * tq
    NC = kv_ref.shape[1]
    n_lanes = code_ref.shape[-1]
    code = code_ref[0]
    row_pos = first_pos + i * tq + lax.broadcasted_iota(jnp.int32, (M, 1), 0) % tq
    any_valid = (row_pos >= NSA_CMP_BLOCK - 1).astype(F32)
    lane = lax.broadcasted_iota(jnp.int32, (tq, n_lanes), 1)
    for g in range(G):
        kv = kv_ref[0, :, LANES * g:LANES * (g + 1)]
        s = _dot_t(q_ref[0, g, 0], kv) + bias_ref[0, g]
        m = jnp.max(s, axis=-1, keepdims=True)
        p = jnp.exp(s - m)
        p = p * (any_valid / jnp.sum(p, axis=-1, keepdims=True))
        o_ref[0, g, 0] = _dot(p.astype(BF16), kv).astype(o_ref.dtype)
        imp = p[0:tq]
        for r in range(1, R):
            imp = imp + p[r * tq:(r + 1) * tq]
        imp = imp[:, :NC // 2] + imp[:, NC // 2:]
        if n_lanes > NC // 2:
            imp = jnp.concatenate([imp, jnp.zeros((tq, n_lanes - NC // 2), F32)], axis=1)
        score = jnp.where(code == 0.0, imp, code)
        rank = jnp.zeros((tq, n_lanes), F32)
        for jp in range(n_rank):
            col = score[:, jp:jp + 1]
            ahead = jnp.where(col > score, 1.0, jnp.where(col == score, jnp.where(lane > jp, 1.0, 0.0), 0.0))
            rank = rank + ahead
        sel_neg = jnp.where(rank < float(NSA_TOP_N), jnp.where(code > -0.5 * SELECT_BIG, 0.0, NEG), NEG)
        sel_ref[0, g, 0] = sel_neg.astype(sel_ref.dtype)


def _cmp_select(q_stack, kvc, bias, code, tq, n_rank, first_pos):
    B, G, nq, M, _ = q_stack.shape
    NC = kvc.shape[1]
    n_lanes = code.shape[-1]
    return pl.pallas_call(
        functools.partial(_cmp_select_kernel, tq=tq, n_rank=n_rank, first_pos=first_pos), grid=(B, nq),
        in_specs=[pl.BlockSpec((1, G, 1, M, LANES), lambda b, i: (b, 0, i, 0, 0)),
                  pl.BlockSpec((1, NC, G * LANES), lambda b, i: (b, 0, 0)),
                  pl.BlockSpec((1, G, M, NC), lambda b, i: (i, 0, 0, 0)),
                  pl.BlockSpec((1, tq, n_lanes), lambda b, i: (i, 0, 0))],
        out_specs=[pl.BlockSpec((1, G, 1, M, LANES), lambda b, i: (b, 0, i, 0, 0)),
                   pl.BlockSpec((1, G, 1, tq, n_lanes), lambda b, i: (b, 0, i, 0, 0))],
        out_shape=[jax.ShapeDtypeStruct((B, G, nq, M, LANES), BF16),
                   jax.ShapeDtypeStruct((B, G, nq, tq, n_lanes), BF16)],
        compiler_params=_cp("parallel", "parallel"), name="cmp_select")(q_stack, kvc, bias, code)


def _sel_prompt_kernel(q_ref, sel_ref, kv_ref, oh_ref, bnear_ref, bfar_ref, o_ref, m_sc, acc_sc, *, tq, tk):
    i = pl.program_id(1)
    G = NSA_KV_HEADS
    M = q_ref.shape[3]
    R = M // tq
    far_end = jnp.maximum(i - 1, 0) * tq
    n_full = far_end // tk
    rem = far_end - n_full * tk
    ones = jnp.ones((tk, LANES), BF16)
    m_sc[...] = jnp.full(m_sc.shape, NEG, F32)
    acc_sc[...] = jnp.zeros(acc_sc.shape, F32)

    def scores(start, size):
        kv_all = kv_ref[0, pl.ds(start, size), :]
        onehot = oh_ref[pl.ds(start, size), :]
        out = []
        for g in range(G):
            q = jnp.concatenate([q_ref[0, g, 0], jnp.concatenate([sel_ref[0, g, 0]] * R, axis=0)], axis=1)
            keys = kv_all[:, LANES * g:LANES * (g + 1)]
            out.append((_dot_t(q, jnp.concatenate([keys, onehot], axis=1)), jnp.concatenate([keys, ones[:size]], axis=1)))
        return out

    def body(j, carry):
        for g, (s, v) in enumerate(scores(pl.multiple_of(j * tk, tk), tk)):
            _flash_step([s], [v], m_sc.at[g], acc_sc.at[g])
        return carry

    lax.fori_loop(0, n_full, body, 0)

    @pl.when(rem > 0)
    def _():
        keep = lax.broadcasted_iota(jnp.int32, (M, tk), 1) < rem
        for g, (s, v) in enumerate(scores(pl.multiple_of(n_full * tk, tk), tk)):
            _flash_step([jnp.where(keep, s, NEG)], [v], m_sc.at[g], acc_sc.at[g])

    prev = scores(pl.multiple_of(jnp.maximum(i - 1, 0) * tq, tq), tq)
    diag = scores(pl.multiple_of(i * tq, tq), tq)
    edge = jnp.where(i >= 1, 0.0, NEG)
    for g in range(G):
        far = bfar_ref[g]
        s0 = prev[g][0] + (bnear_ref[g, :, 0:tq] - far) + edge
        s1 = diag[g][0] + (bnear_ref[g, :, tq:2 * tq] - far)
        _flash_step([s0, s1], [prev[g][1], diag[g][1]], m_sc.at[g], acc_sc.at[g])
        acc = acc_sc[g]
        _store_heads(acc[:, :LANES] / acc[:, LANES:], o_ref, g, tq)


def _sel_prompt(q_stack, sel_rows, kv, onehot, bias_near, bias_far, tq, tk):
    B, G, nq, M, _ = q_stack.shape
    T = kv.shape[1]
    return pl.pallas_call(
        functools.partial(_sel_prompt_kernel, tq=tq, tk=tk), grid=(B, nq),
        in_specs=[pl.BlockSpec((1, G, 1, M, LANES), lambda b, i: (b, 0, i, 0, 0)),
                  pl.BlockSpec((1, G, 1, tq, LANES), lambda b, i: (b, 0, i, 0, 0)),
                  pl.BlockSpec((1, T, G * LANES), lambda b, i: (b, 0, 0)),
                  _const_spec(onehot.shape), _const_spec(bias_near.shape), _const_spec(bias_far.shape)],
        out_specs=pl.BlockSpec((1, tq, G * (M // tq) * HEAD_DIM), lambda b, i: (b, i, 0)),
        out_shape=jax.ShapeDtypeStruct((B, nq * tq, G * (M // tq) * HEAD_DIM), BF16),
        scratch_shapes=[pltpu.VMEM((G, M, LANES), F32), pltpu.VMEM((G, M, 2 * LANES), F32)],
        compiler_params=_cp("parallel", "parallel"), name="sel_prompt")(q_stack, sel_rows, kv, onehot, bias_near, bias_far)


def _cmp_select_t_kernel(q_ref, kv_ref, kvt_ref, bias_ref, code_ref, eye_ref, o_ref, sel_ref, *, tq):
    i = pl.program_id(1)
    G, R = NSA_KV_HEADS, NSA_HEADS // NSA_KV_HEADS
    M = R * tq
    NC = kv_ref.shape[1]
    n_sel = code_ref.shape[1]
    code = code_ref[0]
    col_pos = i * tq + lax.broadcasted_iota(jnp.int32, (1, M), 1) % tq
    any_valid = (col_pos >= NSA_CMP_BLOCK - 1).astype(F32)
    blk = lax.broadcasted_iota(jnp.int32, (n_sel, tq), 0)
    for g in range(G):
        kv = kv_ref[0, :, LANES * g:LANES * (g + 1)]
        bias = jnp.concatenate([bias_ref[g * R + r] for r in range(R)], axis=1)
        s = _dot_t(kv, q_ref[0, g, 0]) + bias
        p = jnp.exp(s - jnp.max(s, axis=0, keepdims=True))
        p = p * (any_valid / jnp.sum(p, axis=0, keepdims=True))
        o_t = _dot(kvt_ref[0, LANES * g:LANES * (g + 1), :], p.astype(BF16))
        _store_heads(_dot_t(eye_ref[...], o_t.astype(BF16)), o_ref, g, tq)
        imp = p[:, 0:tq]
        for r in range(1, R):
            imp = imp + p[:, r * tq:(r + 1) * tq]
        imp = imp[:NC // 2] + imp[NC // 2:]
        score = jnp.where(code == 0.0, imp, code)
        rank = jnp.zeros((n_sel, tq), F32)
        for jp in range(n_sel):
            row = score[jp:jp + 1, :]
            rank = rank + jnp.where(row > score, 1.0, jnp.where(row == score, jnp.where(blk > jp, 1.0, 0.0), 0.0))
        sel_neg = jnp.where(rank < float(NSA_TOP_N), jnp.where(code > -0.5 * SELECT_BIG, 0.0, NEG), NEG)
        padded = jnp.concatenate([sel_neg.astype(BF16), jnp.zeros((LANES - n_sel, tq), BF16)], axis=0)
        sel_ref[0, g, 0] = _dot_t(eye_ref[0:tq, 0:tq], padded).astype(sel_ref.dtype)


def _cmp_select_t(q_stack, kvc, kvc_t, bias_t, code_t, tq):
    B, G, nq, M, _ = q_stack.shape
    NC = kvc.shape[1]
    n_sel = code_t.shape[1]
    eye = jnp.eye(M, dtype=BF16)
    return pl.pallas_call(
        functools.partial(_cmp_select_t_kernel, tq=tq), grid=(B, nq),
        in_specs=[pl.BlockSpec((1, G, 1, M, LANES), lambda b, i: (b, 0, i, 0, 0)),
                  pl.BlockSpec((1, NC, G * LANES), lambda b, i: (b, 0, 0)),
                  pl.BlockSpec((1, G * LANES, NC), lambda b, i: (b, 0, 0)),
                  pl.BlockSpec((bias_t.shape[0], NC, tq), lambda b, i: (0, 0, i)),
                  pl.BlockSpec((1, n_sel, tq), lambda b, i: (i, 0, 0)), _const_spec(eye.shape)],
        out_specs=[pl.BlockSpec((1, tq, G * (M // tq) * HEAD_DIM), lambda b, i: (b, i, 0)),
                   pl.BlockSpec((1, G, 1, tq, LANES), lambda b, i: (b, 0, i, 0, 0))],
        out_shape=[jax.ShapeDtypeStruct((B, nq * tq, G * (M // tq) * HEAD_DIM), BF16),
                   jax.ShapeDtypeStruct((B, G, nq, tq, LANES), BF16)],
        compiler_params=_cp("parallel", "parallel"), name="cmp_select_t")(q_stack, kvc, kvc_t, bias_t, code_t, eye)


def _banded_kernel(q_ref, kv_ref, bias_ref, sink_ref, o_ref, *, tq, nw, G):
    i = pl.program_id(1)
    ones = jnp.ones((tq, LANES), BF16)
    for g in range(G):
        q = q_ref[0, g, 0]
        scores, vals = [], []
        for jj in range(nw + 1):
            kb = i - nw + jj
            start = pl.multiple_of(jnp.maximum(kb, 0) * tq, tq)
            kv = kv_ref[0, pl.ds(start, tq), LANES * g:LANES * (g + 1)]
            edge = jnp.where(kb >= 0, 0.0, NEG)
            scores.append(_dot_t(q, kv) + bias_ref[g, :, tq * jj:tq * (jj + 1)] + edge)
            vals.append(kv)
        sink = sink_ref[g]
        m = sink
        for s in scores:
            m = jnp.maximum(m, jnp.max(s, axis=-1, keepdims=True))
        acc = None
        for s, v in zip(scores, vals):
            pv = _dot(jnp.exp(s - m).astype(BF16), jnp.concatenate([v, ones], axis=1))
            acc = pv if acc is None else acc + pv
        _store_heads(acc[:, :LANES] / (acc[:, LANES:] + jnp.exp(sink - m)), o_ref, g, tq)


def _banded(q_stack, kv, bias, sinks, tq, nw):
    B, G, nq, M, _ = q_stack.shape
    T = kv.shape[1]
    return pl.pallas_call(
        functools.partial(_banded_kernel, tq=tq, nw=nw, G=G), grid=(B, nq),
        in_specs=[pl.BlockSpec((1, G, 1, M, LANES), lambda b, i: (b, 0, i, 0, 0)),
                  pl.BlockSpec((1, T, G * LANES), lambda b, i: (b, 0, 0)),
                  _const_spec(bias.shape), _const_spec(sinks.shape)],
        out_specs=pl.BlockSpec((1, tq, G * (M // tq) * HEAD_DIM), lambda b, i: (b, i, 0)),
        out_shape=jax.ShapeDtypeStruct((B, nq * tq, G * (M // tq) * HEAD_DIM), BF16),
        compiler_params=_cp("parallel", "parallel"), name="banded")(q_stack, kv, bias, sinks)


def _mla_decode_kernel(pt_ref, q_ref, *refs, pps):
    page_refs = refs[:pps]
    new_ref, nmask_ref, o_ref, m_sc, l_sc, acc_sc = refs[pps:]
    c = pl.program_id(1)

    @pl.when(c == 0)
    def _():
        _softmax_init(m_sc, l_sc, acc_sc)

    q = q_ref[0]
    nv = acc_sc.shape[-1]
    for half in range(N_CHAINS):
        rows = [r[0].astype(BF16) for r in page_refs[half::N_CHAINS]]
        _softmax_update([_dot(q, r) for r in rows], [r[:nv] for r in rows], m_sc.at[half], l_sc.at[half],
                        acc_sc.at[half], transposed_values=True)

    @pl.when(c == pl.num_programs(1) - 1)
    def _():
        nr = new_ref[0].astype(BF16)
        _softmax_update([_dot_t(q, nr) + nmask_ref[...]], [nr[:, :nv]], m_sc.at[0], l_sc.at[0], acc_sc.at[0])
        l, acc = _merge_chains(m_sc, l_sc, acc_sc)
        o_ref[0] = (acc / l).astype(o_ref.dtype)


def _mla_decode(page_table, q_cat, cache_t, new_rows, new_mask, pps):
    Bd, n_pages = page_table.shape
    _, width, page = cache_t.shape
    M = q_cat.shape[1]
    nv = width - MLA_ROPE
    page_specs = [pl.BlockSpec((1, width, page), lambda b, c, pt, k=k: (pt[b, c * pps + k], 0, 0)) for k in range(pps)]
    return pl.pallas_call(
        functools.partial(_mla_decode_kernel, pps=pps),
        grid_spec=pltpu.PrefetchScalarGridSpec(
            num_scalar_prefetch=1, grid=(Bd, n_pages // pps),
            in_specs=[pl.BlockSpec((1, M, width), lambda b, c, pt: (b, 0, 0))] + page_specs + [
                pl.BlockSpec((1,) + new_rows.shape[1:], lambda b, c, pt: (b, 0, 0)),
                pl.BlockSpec(new_mask.shape, lambda b, c, pt: (0, 0))],
            out_specs=pl.BlockSpec((1, M, nv), lambda b, c, pt: (b, 0, 0)),
            scratch_shapes=[pltpu.VMEM((N_CHAINS, M, 1), F32), pltpu.VMEM((N_CHAINS, M, 1), F32),
                            pltpu.VMEM((N_CHAINS, M, nv), F32)]),
        out_shape=jax.ShapeDtypeStruct((Bd, M, nv), BF16),
        compiler_params=_cp("parallel", "arbitrary"), name="mla_decode")(
            page_table, q_cat, *([cache_t] * pps), new_rows, new_mask)


def _sel_decode_kernel(pt_ref, q_ref, oh_ref, blast_ref, bfar_ref, *refs, pps):
    page_refs = refs[:pps]
    new_ref, nbias_ref, o_ref, m_sc, l_sc, acc_sc = refs[pps:]
    c = pl.program_id(1)
    last = c == pl.num_programs(1) - 1
    G = NSA_KV_HEADS

    @pl.when(c == 0)
    def _():
        _softmax_init(m_sc, l_sc, acc_sc)

    M = q_ref.shape[2]
    rows = lambda g: slice(M * g, M * (g + 1))
    for half in range(N_CHAINS):
        scores, vals = [], []
        for k in range(half, pps, N_CHAINS):
            kv_t = [page_refs[k][0, LANES * g:LANES * (g + 1), :].astype(BF16) for g in range(G)]
            onehot = oh_ref[c * pps + k]
            s = jnp.concatenate([_dot(q_ref[0, g], jnp.concatenate([kv_t[g], onehot], axis=0)) for g in range(G)],
                                axis=0)
            if k == pps - 1:
                s = s + jnp.where(last, blast_ref[...] - bfar_ref[...], 0.0)
            scores.append(s)
            vals.append(kv_t)
        m_prev = m_sc[half]
        m_new = m_prev
        for s in scores:
            m_new = jnp.maximum(m_new, jnp.max(s, axis=-1, keepdims=True))
        alpha = jnp.exp(m_prev - m_new)
        l_new = alpha * l_sc[half]
        acc = alpha * acc_sc[half]
        for s, kv_t in zip(scores, vals):
            p = jnp.exp(s - m_new)
            l_new = l_new + jnp.sum(p, axis=-1, keepdims=True)
            pb = p.astype(BF16)
            acc = acc + jnp.concatenate([_dot_t(pb[rows(g)], kv_t[g]) for g in range(G)], axis=0)
        m_sc[half] = m_new
        l_sc[half] = l_new
        acc_sc[half] = acc

    @pl.when(last)
    def _():
        kv = [new_ref[0, :, LANES * g:LANES * (g + 1)].astype(BF16) for g in range(G)]
        n_new = kv[0].shape[0]
        s = jnp.concatenate([_dot_t(q_ref[0, g][:, :LANES], kv[g]) for g in range(G)], axis=0)
        s = s + (nbias_ref[...] - bfar_ref[:, :n_new])
        m_fin = jnp.maximum(m_sc[0], jnp.max(s, axis=-1, keepdims=True))
        a_fin = jnp.exp(m_sc[0] - m_fin)
        p = jnp.exp(s - m_fin)
        pb = p.astype(BF16)
        l_sc[0] = a_fin * l_sc[0] + jnp.sum(p, axis=-1, keepdims=True)
        acc_sc[0] = a_fin * acc_sc[0] + jnp.concatenate([_dot(pb[rows(g)], kv[g]) for g in range(G)], axis=0)
        m_sc[0] = m_fin
        l, acc = _merge_chains(m_sc, l_sc, acc_sc)
        out = acc / l
        for g in range(G):
            o_ref[0, g] = out[rows(g)].astype(o_ref.dtype)


def _sel_decode(page_table, q_aug, onehot_t, bias_last, bias_far, cache_t, new_rows, new_bias, pps):
    Bd, n_pages = page_table.shape
    _, feat, page = cache_t.shape
    _, G, M, _ = q_aug.shape
    page_specs = [pl.BlockSpec((1, feat, page), lambda b, c, pt, k=k: (pt[b, c * pps + k], 0, 0)) for k in range(pps)]
    cst = lambda a: pl.BlockSpec(a.shape, lambda b, c, pt: (0,) * a.ndim)
    per_b = lambda a: pl.BlockSpec((1,) + a.shape[1:], lambda b, c, pt: (b,) + (0,) * (a.ndim - 1))
    return pl.pallas_call(
        functools.partial(_sel_decode_kernel, pps=pps),
        grid_spec=pltpu.PrefetchScalarGridSpec(
            num_scalar_prefetch=1, grid=(Bd, n_pages // pps),
            in_specs=[per_b(q_aug), cst(onehot_t), cst(bias_last), cst(bias_far)] + page_specs
            + [per_b(new_rows), cst(new_bias)],
            out_specs=pl.BlockSpec((1, G, M, LANES), lambda b, c, pt: (b, 0, 0, 0)),
            scratch_shapes=[pltpu.VMEM((N_CHAINS, G * M, 1), F32), pltpu.VMEM((N_CHAINS, G * M, 1), F32),
                            pltpu.VMEM((N_CHAINS, G * M, LANES), F32)]),
        out_shape=jax.ShapeDtypeStruct((Bd, G, M, LANES), BF16),
        compiler_params=_cp("parallel", "arbitrary"), name="sel_decode")(
            page_table, q_aug, onehot_t, bias_last, bias_far, *([cache_t] * pps), new_rows, new_bias)


def _win_decode_kernel(q_ref, buf_ref, new_ref, bias_ref, nbias_ref, sink_ref, o_ref, *, G, bb):
    def one_sequence(j, carry):
        for g in range(G):
            q = q_ref[j, g]
            kb = buf_ref[j, LANES * g:LANES * (g + 1), :].astype(BF16)
            kn = new_ref[j, :, LANES * g:LANES * (g + 1)].astype(BF16)
            sb = _dot(q, kb) + bias_ref[g]
            sn = _dot_t(q, kn) + nbias_ref[g]
            sink = sink_ref[g]
            m = jnp.maximum(jnp.maximum(jnp.max(sb, axis=-1, keepdims=True), jnp.max(sn, axis=-1, keepdims=True)), sink)
            pb = jnp.exp(sb - m)
            pn = jnp.exp(sn - m)
            l = jnp.exp(sink - m) + jnp.sum(pb, axis=-1, keepdims=True) + jnp.sum(pn, axis=-1, keepdims=True)
            acc = _dot_t(pb.astype(BF16), kb) + _dot(pn.astype(BF16), kn)
            o_ref[j, g] = (acc / l).astype(o_ref.dtype)
        return carry

    lax.fori_loop(0, bb, one_sequence, 0)


def _win_decode(q_stack, buf, new_rows, bias, new_bias, sinks):
    Bd, G, M, _ = q_stack.shape
    bb = 4 if Bd % 4 == 0 else 1
    per_b = lambda a: pl.BlockSpec((bb,) + a.shape[1:], lambda b: (b,) + (0,) * (a.ndim - 1))
    return pl.pallas_call(
        functools.partial(_win_decode_kernel, G=G, bb=bb), grid=(Bd // bb,),
        in_specs=[per_b(q_stack), per_b(buf), per_b(new_rows), _const_spec(bias.shape), _const_spec(new_bias.shape),
                  _const_spec(sinks.shape)],
        out_specs=pl.BlockSpec((bb, G, M, LANES), lambda b: (b, 0, 0, 0)),
        out_shape=jax.ShapeDtypeStruct((Bd, G, M, LANES), BF16),
        compiler_params=_cp("parallel"), name="win_decode")(q_stack, buf, new_rows, bias, new_bias, sinks)


def _ffn_chunk(f):
    for cand in range(min(f, 1536) // LANES, 0, -1):
        if f % (cand * LANES) == 0:
            return cand * LANES
    return f


def _out_ffn_kernel(*refs, mode, final, n_mods):
    it = iter(refs)
    y_ref = next(it)
    if mode == "c":
        attn_in = next(it)[0]
    else:
        oa_ref, ocmp_ref, osel_ref, owin_ref, gate_ref, eg_ref = (next(it) for _ in range(6))
        if mode == "ab_sample":
            wuv_ref = next(it)
    wout_ref = next(it)
    gt_ref, fsh_ref, fsc_ref, fgt_ref = (next(it) for _ in range(4))
    wg_ref, wu_ref, wo_ref = (next(it) for _ in range(3))
    gain_ref = next(it) if final else None
    o_ref = next(it)

    if mode == "c":
        attn = _dot(attn_in, wout_ref[...])
    else:
        gates = gate_ref[0]
        g_hi = gates.astype(BF16)
        g_lo = (gates - g_hi.astype(F32)).astype(BF16)
        o_b = None
        for br, ref in enumerate((ocmp_ref, osel_ref, owin_ref)):
            ge = _dot(g_hi, eg_ref[br]) + _dot(g_lo, eg_ref[br])
            term = ge * ref[0].astype(F32)
            o_b = term if o_b is None else o_b + term
        o_a = oa_ref[0]
        if mode == "ab_sample":
            o_a = _dot(o_a, wuv_ref[...]).astype(BF16)
        na = o_a.shape[-1]
        attn = _dot(o_a, wout_ref[0:na, :]) + _dot(o_b.astype(BF16), wout_ref[na:, :])
    y1 = y_ref[0] + gt_ref[0] * attn
    h = _modulate(y1, fsh_ref[0], fsc_ref[0]).astype(BF16)
    f = wg_ref.shape[1]
    fc = _ffn_chunk(f)
    acc = None
    for k in range(f // fc):
        g = _dot(h, wg_ref[:, fc * k:fc * (k + 1)])
        u = _dot(h, wu_ref[:, fc * k:fc * (k + 1)])
        a = (g * _sigmoid(g) * u).astype(BF16)
        part = _dot(a, wo_ref[fc * k:fc * (k + 1), :])
        acc = part if acc is None else acc + part
    y2 = y1 + fgt_ref[0] * acc
    if final:
        y2 = _rms(y2) * gain_ref[...]
    o_ref[0] = y2


def _out_ffn(y, attn_parts, weights, mods, ffn_w, gain, mode, tm):
    Bx, Tx, D = y.shape
    per_token = mods[0].shape[1] != 1
    mod_spec = (pl.BlockSpec((1, tm, D), lambda b, i: (b, i, 0)) if per_token
                else pl.BlockSpec((1, 1, D), lambda b, i: (b, 0, 0)))
    tok = lambda a: pl.BlockSpec((1, tm, a.shape[-1]), lambda b, i: (b, i, 0))
    single = lambda a: pl.BlockSpec(a.shape, lambda b, i: (0,) * a.ndim, pipeline_mode=pl.Buffered(1))
    args = [y] + list(attn_parts) + list(weights) + list(mods) + list(ffn_w)
    in_specs = ([tok(y)] + [tok(a) for a in attn_parts] + [single(a) for a in weights] + [mod_spec] * 4
                + [single(a) for a in ffn_w])
    final = gain is not None
    if final:
        args.append(gain)
        in_specs.append(single(gain))
    return pl.pallas_call(
        functools.partial(_out_ffn_kernel, mode=mode, final=final, n_mods=4), grid=(Bx, Tx // tm),
        in_specs=in_specs, out_specs=tok(y), out_shape=jax.ShapeDtypeStruct(y.shape, F32),
        compiler_params=_cp("parallel", "parallel"), name="out_ffn_" + mode)(*args)


def _stack_heads(q, G, R, tq):
    B, T, _ = q.shape
    q = q.reshape(B, T // tq, tq, G, R, HEAD_DIM).transpose(0, 3, 1, 4, 2, 5).reshape(B, G, T // tq, R * tq, HEAD_DIM)
    return jnp.pad(q, ((0, 0),) * 4 + ((0, LANES - HEAD_DIM),))


def _unstack_heads(o, R, tq):
    B, G, nq, _, _ = o.shape
    o = o[..., HEAD_DIM:].reshape(B, G, nq, R, tq, HEAD_DIM).transpose(0, 2, 4, 1, 3, 5)
    return o.reshape(B, nq * tq, G * R * HEAD_DIM)


def _stack_heads_sample(q, G, R, S, s_pad):
    Bd = q.shape[0] // S
    q = q.reshape(Bd, S, G, R, HEAD_DIM).transpose(0, 2, 3, 1, 4)
    q = jnp.pad(q, ((0, 0), (0, 0), (0, 0), (0, s_pad - S), (0, LANES - HEAD_DIM)))
    return q.reshape(Bd, G, R * s_pad, LANES)


def _unstack_heads_sample(o, R, S, s_pad):
    Bd, G, _, _ = o.shape
    o = o[..., HEAD_DIM:].reshape(Bd, G, R, s_pad, HEAD_DIM)[:, :, :, :S].transpose(0, 3, 1, 2, 4)
    return o.reshape(1, Bd * S, G * R * HEAD_DIM)


def _group_rows(t, G):
    H, rows, C = t.shape
    return t.reshape(G, (H // G) * rows, C)


def _even_odd(n):
    return jnp.concatenate([jnp.arange(0, n, 2), jnp.arange(1, n, 2)]).astype(jnp.int32)


def _rope_tables(pos):
    half = MLA_ROPE // 2
    freq = ROPE_THETA ** (-jnp.arange(half, dtype=F32) / half)
    ang = pos.astype(F32)[:, None] * freq[None, :]
    cos, sin = jnp.cos(ang), jnp.sin(ang)
    n = pos.shape[0]
    one, zero = jnp.ones, jnp.zeros
    cq = jnp.concatenate([one((n, MLA_NOPE), F32), cos, cos, zero((n, LANES - MLA_NOPE - MLA_ROPE), F32)], axis=1)
    sq = jnp.concatenate([zero((n, MLA_NOPE), F32), sin, sin, zero((n, LANES - MLA_NOPE - MLA_ROPE), F32)], axis=1)
    n_gate = 3 * NSA_HEADS
    cm = jnp.concatenate([cos, cos, one((n, n_gate), F32), zero((n, LANES - MLA_ROPE - n_gate), F32)], axis=1)
    sm = jnp.concatenate([sin, sin, zero((n, LANES - MLA_ROPE), F32)], axis=1)
    return cq, sq, cm, sm


def _layer0_weights(w_in_0, mla_q_norm, mla_w_uq, mla_kv_norm, mla_w_uk, mla_w_uv):
    D = w_in_0.shape[0]
    qr, kvr = mla_q_norm.shape[0], mla_kv_norm.shape[0]
    half = MLA_ROPE // 2
    o_kr = qr + kvr
    o_q = o_kr + MLA_ROPE
    o_cmp = o_q + NSA_HEADS * HEAD_DIM
    kvw = NSA_KV_HEADS * 2 * HEAD_DIM
    o_g = o_cmp + 3 * kvw
    w_kr = w_in_0[:, o_kr:o_q]
    w_g = w_in_0[:, o_g:]
    z = lambda n: jnp.zeros((D, n), F32)
    misc_a = jnp.concatenate([w_kr, w_g, z(LANES - MLA_ROPE - w_g.shape[1])], axis=1)
    misc_b = jnp.concatenate([-w_kr[:, half:], w_kr[:, :half], z(LANES - MLA_ROPE)], axis=1)
    w0 = jnp.concatenate([w_in_0[:, :o_kr], misc_a, misc_b, w_in_0[:, o_q:o_cmp] * ATTN_SCALE, w_in_0[:, o_cmp:o_g]],
                         axis=1).astype(BF16)
    H = MLA_HEADS
    wq = mla_w_uq.reshape(qr, H, MLA_NOPE + MLA_ROPE)
    nope, x1, x2 = wq[..., :MLA_NOPE], wq[..., MLA_NOPE:MLA_NOPE + half], wq[..., MLA_NOPE + half:]
    zq = lambda n: jnp.zeros((qr, H, n), F32)
    pad = LANES - MLA_NOPE - MLA_ROPE
    wuq = jnp.concatenate([nope, x1, x2, zq(pad)], axis=-1).reshape(qr, H * LANES).astype(BF16)
    wuqs = jnp.concatenate([zq(MLA_NOPE), -x2, x1, zq(pad)], axis=-1).reshape(qr, H * LANES).astype(BF16)
    k_top = jnp.pad(mla_w_uk, ((0, 0), (0, 0), (0, LANES - MLA_NOPE))).reshape(kvr, H * LANES)
    place = jnp.zeros((LANES, H, LANES), F32).at[jnp.arange(MLA_ROPE), :, MLA_NOPE + jnp.arange(MLA_ROPE)].set(1.0)
    kcat = jnp.concatenate([k_top, place.reshape(LANES, H * LANES)], axis=0).astype(BF16)
    even = (jnp.arange(H) % 2 == 0)[None, :, None]
    zv = jnp.zeros((kvr, H, LANES - MLA_V), F32)
    wuv = jnp.where(even, jnp.concatenate([mla_w_uv, zv], axis=-1), jnp.concatenate([zv, mla_w_uv], axis=-1))
    wuv = wuv.reshape(kvr, H * LANES).astype(BF16)
    ones_v, zero_v = jnp.ones((1, H, MLA_V), F32), jnp.zeros((1, H, MLA_V), F32)
    vones = jnp.where(even, jnp.concatenate([zero_v, ones_v], axis=-1), jnp.concatenate([ones_v, zero_v], axis=-1))
    vones = vones.reshape(1, H * LANES)
    width = kvr + MLA_ROPE
    blk = jnp.zeros((H, LANES, width), F32)
    blk = blk.at[:, :MLA_NOPE, :kvr].set(jnp.transpose(mla_w_uk, (1, 2, 0)))
    blk = blk.at[:, MLA_NOPE + jnp.arange(MLA_ROPE), kvr + jnp.arange(MLA_ROPE)].set(1.0)
    eye = jnp.eye(H, dtype=F32)
    a_abs = jnp.einsum("hij,hk->hikj", blk, eye).reshape(H * LANES, H * width).astype(BF16)
    wuv_bd = jnp.einsum("chd,hk->hckd", mla_w_uv, eye).reshape(H * kvr, H * MLA_V).astype(BF16)
    return dict(w0=w0, qn=mla_q_norm.reshape(1, qr), kvn=mla_kv_norm.reshape(1, kvr), wuq=wuq, wuqs=wuqs, kcat=kcat,
                wuv=wuv, vones=vones, a_abs=a_abs, wuv_bd=wuv_bd)


def _gate_expand():
    h = jnp.arange(NSA_HEADS)
    mats = []
    for br in range(3):
        m = jnp.zeros((LANES, NSA_HEADS, HEAD_DIM), F32).at[MLA_ROPE + 3 * h + br, h, :].set(1.0)
        mats.append(m.reshape(LANES, NSA_HEADS * HEAD_DIM))
    return jnp.stack(mats).astype(BF16)


def kernel(x_prompt, x_sample, cache_mla, cache_nsa_cmp, cache_nsa_sel, state_nsa_win, state_swa, page_table, c_prompt, c_sample, rel_bias_table, w_ada_0, b_ada_0, w_in_0, mla_q_norm, mla_w_uq, mla_kv_norm, mla_w_uk, mla_w_uv, nsa_w_cmp, nsa_pe_cmp, w_out_0, w_ffn_in_0, w_ffn_out_0, w_ada_1, b_ada_1, w_in_1, swa_sinks, w_out_1, w_ffn_in_1, w_ffn_out_1, final_norm):
    B, T, D = x_prompt.shape
    Bd, S, _ = x_sample.shape
    n_pool, PAGE, mla_w = cache_mla.shape
    n_pages = page_table.shape[1]
    PAST = n_pages * PAGE
    G, R = NSA_KV_HEADS, NSA_HEADS // NSA_KV_HEADS
    G1, R1 = SWA_KV_HEADS, SWA_HEADS // SWA_KV_HEADS
    tq = Q_BLOCK
    nq = T // tq
    NS = Bd * S
    S_PAD = 8
    tm_p = 512 if T % 512 == 0 else 256
    tm_s = 256 if NS % 256 == 0 else NS
    i32 = jnp.int32
    table = rel_bias_table.astype(F32)

    n_c = B + Bd
    c_all = jnp.pad(jnp.concatenate([c_prompt, c_sample], axis=0), ((0, (-n_c) % 8), (0, 0)))

    def mods_for(w_ada, b_ada):
        m = _ada(c_all, w_ada.astype(BF16), b_ada.reshape(1, -1))
        mp = [m[:B, k * D:(k + 1) * D][:, None, :] for k in range(6)]
        ms = [jnp.repeat(m[B:B + Bd, k * D:(k + 1) * D], S, axis=0)[None] for k in range(6)]
        return mp, ms

    mods0_p, mods0_s = mods_for(w_ada_0, b_ada_0)
    mods1_p, mods1_s = mods_for(w_ada_1, b_ada_1)
    xs = x_sample.reshape(1, NS, D)

    w0 = _layer0_weights(w_in_0, mla_q_norm, mla_w_uq, mla_kv_norm, mla_w_uk, mla_w_uv)
    tabs_p = _rope_tables(jnp.arange(T, dtype=i32))
    tabs_s = tuple(jnp.tile(t, (Bd, 1)) for t in _rope_tables(PAST + jnp.arange(S, dtype=i32)))
    (mla_p, q_mla, k_mla, v_mla, q_stack_p, cmp_p, sel_p, win_p, selb_p, winb_p, gate_p) = _proj0(
        x_prompt, mods0_p[0], mods0_p[1], w0, tabs_p, False, tm_p)
    (mla_s, qcat_s, qn_s, cmp_s, sel_s, win_s, gate_s) = _proj0(xs, mods0_s[0], mods0_s[1], w0, tabs_s, True, tm_s)

    o_a_p = _mla_prompt(q_mla, k_mla, v_mla, 512 if T % 512 == 0 else 256)

    eye2 = jnp.eye(2, dtype=F32)
    wbig = jnp.einsum("lcde,gh,ck->lgcdhke", nsa_w_cmp, eye2, eye2).reshape(NSA_CMP_BLOCK * 4 * HEAD_DIM, 4 * HEAD_DIM)
    wbig = wbig.astype(BF16)
    pe_rows = jnp.broadcast_to(nsa_pe_cmp[:, None], (NSA_CMP_BLOCK, G, 2, HEAD_DIM)).reshape(1, -1)
    pe_rows = jnp.broadcast_to(pe_rows, (8, pe_rows.shape[1]))
    pe_bias = _pe_bias(pe_rows, wbig)
    NC = T // NSA_CMP_BLOCK
    kvc_p = _compress_prompt(cmp_p.reshape(B, NC, -1), wbig, pe_bias)
    order_p = _even_odd(NC)
    kvc_p = kvc_p[:, order_p]

    qpos = jnp.arange(T, dtype=i32)
    dist = qpos[:, None] - (order_p * NSA_CMP_BLOCK + NSA_CMP_BLOCK - 1)[None, :]
    bias_cmp_p = _bias_tiles(table, _masked_bucket(dist, dist >= 0).T, NSA_HEADS)
    n_sel = T // NSA_SEL_BLOCK
    blk = jnp.arange(n_sel, dtype=i32)[None, :]
    cur = (qpos // NSA_SEL_BLOCK)[:, None]
    forced = (blk == 0) | (blk == cur) | (blk == cur - 1)
    causal = blk * NSA_SEL_BLOCK <= qpos[:, None]
    code_p = jnp.where(causal, jnp.where(forced, SELECT_BIG, 0.0), -SELECT_BIG).astype(F32).reshape(nq, tq, n_sel)
    assert n_sel <= LANES and n_sel == NC // 2, "selection blocks must fit one lane tile"
    o_cmp_p, sel_rows_p = _cmp_select_t(q_stack_p, kvc_p, kvc_p.transpose(0, 2, 1), bias_cmp_p,
                                        code_p.transpose(0, 2, 1), tq)

    onehot_p = (jnp.arange(LANES, dtype=i32)[None, :] == (qpos // NSA_SEL_BLOCK)[:, None]).astype(BF16)
    ql = jnp.arange(tq, dtype=i32)[:, None]
    d_near = ql + tq - jnp.arange(2 * tq, dtype=i32)[None, :]
    bk_near = jnp.concatenate([_masked_bucket(d_near, d_near >= 0), jnp.full((tq, 2 * tq), REL_BUCKETS - 1, i32)], axis=0)
    near_far = _bias_tiles(table, bk_near, NSA_HEADS)
    bias_near_p = _group_rows(near_far[:, :tq], G)
    bias_far_p = _group_rows(near_far[:, tq:, :LANES], G)
    tk_sel = 512 if T % 512 == 0 else tq
    o_sel_st = _sel_prompt(q_stack_p, sel_rows_p, selb_p, onehot_p, bias_near_p, bias_far_p, tq, tk_sel)

    def window_bias(window, n_heads, n_groups):
        nw = -(-window // tq)
        dw = ql + nw * tq - jnp.arange((nw + 1) * tq, dtype=i32)[None, :]
        t = _bias_tiles(table, _masked_bucket(dw, (dw >= 0) & (dw < window)), n_heads)
        return _group_rows(t, n_groups), nw

    bias_win_p, nw0 = window_bias(NSA_WINDOW, NSA_HEADS, G)
    no_sink0 = jnp.full((G, R * tq, 1), NEG, F32)
    o_win_st = _banded(q_stack_p, winb_p, bias_win_p, no_sink0, tq, nw0)

    o_sel_p, o_win_p = o_sel_st, o_win_st

    pps = 8 if n_pages % 8 == 0 else n_pages
    srow = jnp.arange(S_PAD, dtype=i32)
    s_real = jnp.minimum(srow, S - 1)
    q_cat = qcat_s.reshape(Bd, S * MLA_HEADS, mla_w)
    mla_new = jnp.pad(mla_s.reshape(Bd, S, mla_w), ((0, 0), (0, S_PAD - S), (0, 0)))
    s_of_row = jnp.repeat(jnp.arange(S, dtype=i32), MLA_HEADS)[:, None]
    new_mask = jnp.where((srow[None, :] <= s_of_row) & (srow[None, :] < S), 0.0, NEG).astype(F32)
    pps_m = 32 if n_pages % 32 == 0 else (16 if n_pages % 16 == 0 else pps)
    o_full_s = _mla_decode(page_table, q_cat, cache_mla.transpose(0, 2, 1), mla_new, new_mask, pps_m)
    o_full_s = o_full_s.reshape(1, NS, MLA_HEADS * (mla_w - MLA_ROPE))

    pps_c = 32 if n_pages % 32 == 0 else pps_m
    feat = G * 2 * HEAD_DIM
    cache_cmp_t = cache_nsa_cmp.transpose(0, 2, 3, 4, 1).reshape(n_pool, feat, PAGE)
    w_head = wbig.reshape(NSA_CMP_BLOCK, feat, feat)[:, :LANES, :LANES]
    kvc_s = _compress_paged(page_table, cache_cmp_t, w_head, pe_bias, pps_c)
    NCs = PAST // NSA_CMP_BLOCK
    order_s = _even_odd(NCs)
    kvc_s = kvc_s[:, order_s]
    q_stack_s = _stack_heads_sample(qn_s[0], G, R, S, S_PAD)
    pos_s = PAST + s_real
    dist_s = pos_s[:, None] - (order_s * NSA_CMP_BLOCK + NSA_CMP_BLOCK - 1)[None, :]
    bias_cmp_s = _group_rows(_bias_tiles(table, _masked_bucket(dist_s, dist_s >= 0), NSA_HEADS), G)[None]
    n_past_blk = PAST // NSA_SEL_BLOCK
    n_sel_s = n_past_blk + -(-S // NSA_SEL_BLOCK)
    sel_lanes = -(-n_sel_s // LANES) * LANES
    blk_s = jnp.arange(sel_lanes, dtype=i32)[None, :]
    cur_s = (pos_s // NSA_SEL_BLOCK)[:, None]
    forced_s = (blk_s == 0) | (blk_s == cur_s) | (blk_s == cur_s - 1)
    causal_s = (blk_s * NSA_SEL_BLOCK <= pos_s[:, None]) & (blk_s < n_sel_s)
    code_s = jnp.where(causal_s, jnp.where(forced_s, SELECT_BIG, 0.0), -SELECT_BIG).astype(F32)[None]
    o_cmp_ss, sel_mask_s = _cmp_select(q_stack_s[:, :, None], kvc_s, bias_cmp_s, code_s, S_PAD, n_sel_s, PAST)
    o_cmp_s = _unstack_heads_sample(o_cmp_ss[:, :, 0], R, S, S_PAD)

    assert n_past_blk <= LANES and PAGE >= REL_MAX_DISTANCE, "past selection blocks must fit one lane tile"
    sel_past = jnp.pad(sel_mask_s[:, :, 0, :, :n_past_blk], ((0, 0),) * 3 + ((0, LANES - n_past_blk),))
    q_aug_s = jnp.concatenate([q_stack_s, jnp.tile(sel_past, (1, 1, R, 1))], axis=-1)
    key_blk_s = (jnp.arange(PAST, dtype=i32) // NSA_SEL_BLOCK).reshape(n_pages, 1, PAGE)
    onehot_s = (jnp.arange(LANES, dtype=i32)[None, :, None] == key_blk_s).astype(BF16)
    d_last = pos_s[:, None] - (PAST - PAGE + jnp.arange(PAGE, dtype=i32))[None, :]
    bk_last = jnp.concatenate([_masked_bucket(d_last, d_last >= 0), jnp.full((S_PAD, PAGE), REL_BUCKETS - 1, i32)], axis=0)
    last_far = _bias_tiles(table, bk_last, NSA_HEADS)
    bias_last_s = _group_rows(last_far[:, :S_PAD], G)
    bias_far_s = _group_rows(last_far[:, S_PAD:], G)
    d_new = s_real[:, None] - srow[None, :]
    bk_new = _masked_bucket(d_new, (d_new >= 0) & (srow[None, :] < S))
    nbias_nsa = _group_rows(_bias_tiles(table, bk_new, NSA_HEADS), G)
    sel_new8 = jnp.pad(sel_s.reshape(Bd, S, -1), ((0, 0), (0, S_PAD - S), (0, 0)))
    cache_sel_t = cache_nsa_sel.transpose(0, 2, 3, 4, 1).reshape(n_pool, feat, PAGE)
    flat = lambda a: a.reshape(-1, a.shape[-1])
    o_sel_ss = _sel_decode(page_table, q_aug_s, onehot_s, flat(bias_last_s), flat(bias_far_s), cache_sel_t, sel_new8,
                           flat(nbias_nsa), pps_m)
    o_sel_s = _unstack_heads_sample(o_sel_ss, R, S, S_PAD)

    def state_bias(wb, window, n_heads, n_groups):
        d = wb + s_real[:, None] - jnp.arange(wb, dtype=i32)[None, :]
        return _group_rows(_bias_tiles(table, _masked_bucket(d, d < window), n_heads), n_groups)

    wb0 = state_nsa_win.shape[1]
    win_new8 = jnp.pad(win_s.reshape(Bd, S, -1), ((0, 0), (0, S_PAD - S), (0, 0)))
    no_sink0_s = jnp.full((G, R * S_PAD, 1), NEG, F32)
    o_win_ss = _win_decode(q_stack_s, state_nsa_win.transpose(0, 2, 3, 4, 1).reshape(Bd, feat, wb0), win_new8,
                           state_bias(wb0, NSA_WINDOW, NSA_HEADS, G), nbias_nsa, no_sink0_s)
    o_win_s = _unstack_heads_sample(o_win_ss, R, S, S_PAD)

    eg = _gate_expand()
    w_out0 = w_out_0.astype(BF16)
    f = w_ffn_out_0.shape[0]
    ffn0 = (w_ffn_in_0[:, :f].astype(BF16), w_ffn_in_0[:, f:].astype(BF16), w_ffn_out_0.astype(BF16))
    y1_p = _out_ffn(x_prompt, [o_a_p, o_cmp_p, o_sel_p, o_win_p, gate_p], [eg, w_out0], mods0_p[2:], ffn0, None,
                    "ab_prompt", tm_p)
    y1_s = _out_ffn(xs, [o_full_s, o_cmp_s, o_sel_s, o_win_s, gate_s], [eg, w0["wuv_bd"], w_out0], mods0_s[2:], ffn0,
                    None, "ab_sample", tm_s)

    nq1 = SWA_HEADS * HEAD_DIM
    w1 = jnp.concatenate([w_in_1[:, :nq1] * ATTN_SCALE, w_in_1[:, nq1:]], axis=1).astype(BF16)
    q1_p, kv1_p, kvb1_p = _proj1(y1_p, mods1_p[0], mods1_p[1], w1, nq1, tm_p, True)
    q1_s, kv1_s, _ = _proj1(y1_s, mods1_s[0], mods1_s[1], w1, nq1, tm_s, False)
    bias_swa_p, nw1 = window_bias(SWA_WINDOW, SWA_HEADS, G1)
    sink_p = jnp.repeat(swa_sinks.astype(F32).reshape(G1, R1), tq, axis=1).reshape(G1, R1 * tq, 1)
    o_c_p = _banded(q1_p, kvb1_p, bias_swa_p, sink_p, tq, nw1)

    wb1 = state_swa.shape[1]
    kv_new8 = jnp.pad(kv1_s.reshape(Bd, S, -1), ((0, 0), (0, S_PAD - S), (0, 0)))
    d_new1 = s_real[:, None] - srow[None, :]
    nbias_swa = _group_rows(_bias_tiles(table, _masked_bucket(d_new1, (d_new1 >= 0) & (srow[None, :] < S)), SWA_HEADS),
                            G1)
    sink_s = jnp.repeat(swa_sinks.astype(F32).reshape(G1, R1), S_PAD, axis=1).reshape(G1, R1 * S_PAD, 1)
    state_swa_t = state_swa.transpose(0, 2, 3, 4, 1).reshape(Bd, G1 * 2 * HEAD_DIM, wb1)
    o_c_ss = _win_decode(_stack_heads_sample(q1_s[0], G1, R1, S, S_PAD), state_swa_t, kv_new8,
                         state_bias(wb1, SWA_WINDOW, SWA_HEADS, G1), nbias_swa, sink_s)
    o_c_s = _unstack_heads_sample(o_c_ss, R1, S, S_PAD)

    w_out1 = w_out_1.astype(BF16)
    f1 = w_ffn_out_1.shape[0]
    ffn1 = (w_ffn_in_1[:, :f1].astype(BF16), w_ffn_in_1[:, f1:].astype(BF16), w_ffn_out_1.astype(BF16))
    gain = final_norm.reshape(1, D).astype(F32)
    y_prompt = _out_ffn(y1_p, [o_c_p], [w_out1], mods1_p[2:], ffn1, gain, "c", tm_p)
    y_sample = _out_ffn(y1_s, [o_c_s], [w_out1], mods1_s[2:], ffn1, gain, "c", tm_s).reshape(Bd, S, D)

    row5 = lambda a, lead, g: a.reshape(lead + (g, 2, HEAD_DIM))
    win_p5 = row5(win_p, (B, T), G)
    kv1_p5 = row5(kv1_p, (B, T), G1)
    win_s5 = row5(win_s, (Bd, S), G)
    kv1_s5 = row5(kv1_s, (Bd, S), G1)
    return (y_prompt, y_sample, mla_p, mla_s.reshape(Bd, S, mla_w),
            row5(cmp_p, (B, T), G), row5(cmp_s, (Bd, S), G), row5(sel_p, (B, T), G), row5(sel_s, (Bd, S), G),
            win_p5[:, T - min(NSA_WINDOW, T):], jnp.concatenate([state_nsa_win, win_s5], axis=1)[:, S:],
            kv1_p5[:, T - min(SWA_WINDOW, T):], jnp.concatenate([state_swa, kv1_s5], axis=1)[:, S:])
```
